```python
import math
import jax, jax.numpy as jnp
from jax import lax
import numpy as np


D_MODEL = 1024
BATCH = 8
SEQ = 4096
DEPTH = 2

DN_ALPHA = (2.0 * DEPTH) ** 0.25
DN_BETA = (8.0 * DEPTH) ** -0.25
LN_EPS = 1e-5

DIFF_HEAD_DIM = 64
DIFF_V_DIM = 2 * DIFF_HEAD_DIM
DIFF_HEADS = (D_MODEL // 2) // DIFF_V_DIM
DIFF_QK_WIDTH = DIFF_HEADS * 2 * DIFF_HEAD_DIM
DIFF_V_WIDTH = DIFF_HEADS * DIFF_V_DIM
Q_BLOCK = 128

RET_QK_DIM = 64
RET_V_DIM = 128
RET_HEADS = (D_MODEL // 2) // RET_V_DIM
RET_QK_WIDTH = RET_HEADS * RET_QK_DIM
RET_V_WIDTH = RET_HEADS * RET_V_DIM
RET_CHUNK = 128

EVEN_SPLIT_WIDTHS = (DIFF_QK_WIDTH, DIFF_QK_WIDTH, DIFF_V_WIDTH, RET_QK_WIDTH, RET_QK_WIDTH, RET_V_WIDTH, RET_V_WIDTH)
EVEN_IN_WIDTH = sum(EVEN_SPLIT_WIDTHS)

RWKV_HEAD_DIM = 64
RWKV_HEADS = D_MODEL // RWKV_HEAD_DIM
DECAY_LORA = max(32, int(round(D_MODEL ** 0.5 * 1.8 / 32)) * 32)
ICLR_LORA = max(32, int(round(D_MODEL ** 0.5 * 1.8 / 32)) * 32)
GATE_LORA = max(32, int(round(D_MODEL ** 0.8 * 0.6 / 32)) * 32)
RWKV_GN_EPS = 64e-5

FFN_HIDDEN = -(-(8 * D_MODEL // 3) // 256) * 256
CONV_WIDTH = 3

kernel_name = 'hybrid_diffattn_retention_rwkv7_convglu'


def layer_norm(x, g, b):
    xf = x.astype(jnp.float32)
    mu = jnp.mean(xf, axis=-1, keepdims=True)
    var = jnp.mean(jnp.square(xf - mu), axis=-1, keepdims=True)
    return ((xf - mu) * lax.rsqrt(var + LN_EPS) * g.astype(jnp.float32) + b.astype(jnp.float32)).astype(x.dtype)


def group_norm(y, eps):
    mu = jnp.mean(y, axis=-1, keepdims=True)
    var = jnp.mean(jnp.square(y - mu), axis=-1, keepdims=True)
    return (y - mu) * lax.rsqrt(var + eps)


def alibi_slopes(n_heads):
    return 2.0 ** (-8.0 * jnp.arange(1, n_heads + 1, dtype=jnp.float32) / n_heads)


def diff_attention(q, k, v, lam, subln_g, lambda_init):
    B, S = q.shape[0], q.shape[1]
    n_blk = S // Q_BLOCK
    scale = DIFF_HEAD_DIM ** -0.5
    kt = jnp.transpose(k, (0, 2, 3, 1, 4))
    vt = jnp.transpose(v, (0, 2, 1, 3))
    qb = q.reshape(B, n_blk, Q_BLOCK, DIFF_HEADS, 2, DIFF_HEAD_DIM).transpose(1, 0, 3, 4, 2, 5)
    slopes = alibi_slopes(DIFF_HEADS)[None, :, None, None, None]
    key_pos = jnp.arange(S)

    def one_block(args):
        q_blk, blk = args
        q_pos = blk * Q_BLOCK + jnp.arange(Q_BLOCK)
        dist = (q_pos[:, None] - key_pos[None, :]).astype(jnp.float32)
        s = jnp.einsum('bhmqd,bhmkd->bhmqk', q_blk, kt).astype(jnp.float32) * scale
        s = jnp.where(dist >= 0, s - slopes * dist, -jnp.inf)
        p = jax.nn.softmax(s, axis=-1)
        a = p[:, :, 0] - lam * p[:, :, 1]
        return jnp.einsum('bhqk,bhkv->bhqv', a.astype(v.dtype), vt)

    o = lax.map(one_block, (qb, jnp.arange(n_blk)))
    o = o.transpose(1, 0, 3, 2, 4).reshape(B, S, DIFF_HEADS, DIFF_V_DIM).astype(jnp.float32)
    o = o * lax.rsqrt(jnp.mean(jnp.square(o), axis=-1, keepdims=True) + LN_EPS)
    o = o * subln_g.astype(jnp.float32) * (1.0 - lambda_init)
    return o.reshape(B, S, DIFF_V_WIDTH).astype(v.dtype)


def retention(q, k, v, g):
    B, S = q.shape[0], q.shape[1]
    C = RET_CHUNK
    n = S // C
    f32 = jnp.float32
    log_gamma = jnp.log1p(-(2.0 ** (-5.0 - jnp.arange(RET_HEADS, dtype=f32))))

    def chunks(t):
        return t.astype(f32).reshape(B, n, C, RET_HEADS, t.shape[-1]).transpose(1, 0, 3, 2, 4)

    qc = chunks(q)
    kc = chunks(k) * (RET_QK_DIM ** -0.5)
    vc = chunks(v)
    idx = jnp.arange(C, dtype=f32)
    rel = idx[:, None] - idx[None, :]
    decay = jnp.where(rel >= 0, jnp.exp(log_gamma[:, None, None] * jnp.maximum(rel, 0.0)), 0.0)
    scores = jnp.einsum('nbhid,nbhjd->nbhij', qc, kc) * decay[None, None]
    inner = jnp.einsum('nbhij,nbhjv->nbhiv', scores, vc)
    q_decay = jnp.exp(log_gamma[:, None] * (idx + 1.0))[None, :, :, None]
    k_decay = jnp.exp(log_gamma[:, None] * (C - 1.0 - idx))[None, :, :, None]
    chunk_decay = jnp.exp(log_gamma * C)[None, :, None, None]

    def step(R, xs):
        q_n, k_n, v_n = xs
        cross = jnp.einsum('bhid,bhdv->bhiv', q_n * q_decay, R)
        R = R * chunk_decay + jnp.einsum('bhjd,bhjv->bhdv', k_n * k_decay, v_n)
        return R, cross

    R0 = jnp.zeros((B, RET_HEADS, RET_QK_DIM, RET_V_DIM), f32)
    _, cross = lax.scan(step, R0, (qc, kc, vc))
    y = (inner + cross).transpose(1, 0, 3, 2, 4).reshape(B, S, RET_HEADS, RET_V_DIM)
    y = group_norm(y, LN_EPS).reshape(B, S, RET_V_WIDTH)
    return (jax.nn.silu(g.astype(f32)) * y).astype(v.dtype)


def wkv7(r, w, k, v, a, b):
    def step(state, xs):
        r_t, w_t, k_t, v_t, a_t, b_t = xs
        sa = jnp.einsum('bhij,bhj->bhi', state, a_t)
        state = state * w_t[:, :, None, :] + sa[..., :, None] * b_t[..., None, :] + v_t[..., :, None] * k_t[..., None, :]
        y = jnp.einsum('bhij,bhj->bhi', state, r_t)
        return state, y

    B, _, H, N = r.shape
    xs = tuple(jnp.moveaxis(t.astype(jnp.float32), 1, 0) for t in (r, w, k, v, a, b))
    s0 = jnp.zeros((B, H, N, N), jnp.float32)
    _, y = lax.scan(step, s0, xs)
    return jnp.moveaxis(y, 0, 1)


def rwkv7_time_mix(x, mu, w_rkv, w0, w1, w2, a0, a1, a2, g1, g2, k_k, k_a, r_k, lnx_g, lnx_b, w_out):
    B, S, D = x.shape
    H, N = RWKV_HEADS, RWKV_HEAD_DIM
    f32 = jnp.float32
    x_prev = jnp.pad(x, ((0, 0), (1, 0), (0, 0)))[:, :-1]
    xx = x_prev - x
    mix = x[None] + xx[None] * mu[:, None, None, :]
    rkv = jnp.einsum('nbsd,nde->nbse', mix[:3], w_rkv)
    r, k, v = rkv[0], rkv[1], rkv[2]
    xw, xa, xg = mix[3], mix[4], mix[5]
    w = -jax.nn.softplus(-(w0 + jnp.tanh(xw @ w1) @ w2).astype(f32)) - 0.5
    decay = jnp.exp(-jnp.exp(w))
    a = jax.nn.sigmoid((a0 + (xa @ a1) @ a2).astype(f32))
    g = jax.nn.sigmoid(xg @ g1) @ g2
    kk = (k * k_k).astype(f32).reshape(B, S, H, N)
    kk = kk / jnp.maximum(jnp.sqrt(jnp.sum(jnp.square(kk), axis=-1, keepdims=True)), 1e-12)
    k = k.astype(f32) * (1.0 + (a - 1.0) * k_a.astype(f32))
    rh = r.astype(f32).reshape(B, S, H, N)
    kh = k.reshape(B, S, H, N)
    vh = v.astype(f32).reshape(B, S, H, N)
    ah = a.reshape(B, S, H, N)
    y = wkv7(rh, decay.reshape(B, S, H, N), kh, vh, -kk, kk * ah)
    y = group_norm(y, RWKV_GN_EPS).reshape(B, S, D) * lnx_g.astype(f32) + lnx_b.astype(f32)
    bonus = jnp.sum(rh * kh * r_k.astype(f32).reshape(H, N), axis=-1, keepdims=True) * vh
    y = (y + bonus.reshape(B, S, D)).astype(x.dtype)
    return (y * g) @ w_out


def conv_glu_ffn(x, w_up, conv_w, conv_b, w_down):
    S = x.shape[1]
    u, v = jnp.split(x @ w_up, 2, axis=-1)
    up = jnp.pad(u, ((0, 0), (CONV_WIDTH - 1, 0), (0, 0)))
    c = conv_b + sum(up[:, j:j + S] * conv_w[j] for j in range(CONV_WIDTH))
    return (jax.nn.gelu(c, approximate=False) * v) @ w_down


def setup_inputs(seed: int = 0) -> dict:
    key = jax.random.key(seed)
    ks = iter(jax.random.split(key, 40))
    f32 = jnp.float32
    D, F = D_MODEL, FFN_HIDDEN
    NE, NO = (DEPTH + 1) // 2, DEPTH // 2

    def nrm(shape, scale):
        return jax.random.normal(next(ks), shape, f32) * scale

    return {
        'x': nrm((BATCH, SEQ, D), 1.0),
        'ev_w_in': nrm((NE, D, EVEN_IN_WIDTH), D ** -0.5),
        'ev_lambda': nrm((NE, 4, DIFF_HEAD_DIM), 0.1),
        'ev_subln_g': 1.0 + nrm((NE, DIFF_V_DIM), 0.02),
        'ev_w_out': nrm((NE, D, D), DN_BETA * D ** -0.5),
        'od_mu': jax.random.uniform(next(ks), (NO, 6, D), f32),
        'od_w_rkv': nrm((NO, 3, D, D), D ** -0.5),
        'od_w0': jax.random.uniform(next(ks), (NO, D), f32, -4.0, 2.0),
        'od_w1': nrm((NO, D, DECAY_LORA), D ** -0.5),
        'od_w2': nrm((NO, DECAY_LORA, D), 0.1 * DECAY_LORA ** -0.5),
        'od_a0': nrm((NO, D), 0.1),
        'od_a1': nrm((NO, D, ICLR_LORA), D ** -0.5),
        'od_a2': nrm((NO, ICLR_LORA, D), 0.5 * ICLR_LORA ** -0.5),
        'od_g1': nrm((NO, D, GATE_LORA), D ** -0.5),
        'od_g2': nrm((NO, GATE_LORA, D), GATE_LORA ** -0.5),
        'od_k_k': 0.85 + nrm((NO, D), 0.02),
        'od_k_a': 1.0 + nrm((NO, D), 0.02),
        'od_r_k': nrm((NO, D), 0.1),
        'od_lnx_g': 1.0 + nrm((NO, D), 0.02),
        'od_lnx_b': nrm((NO, D), 0.02),
        'od_w_out': nrm((NO, D, D), DN_BETA * D ** -0.5),
        'ln_mix_g': 1.0 + nrm((DEPTH, D), 0.02),
        'ln_mix_b': nrm((DEPTH, D), 0.02),
        'ffn_w_up': nrm((DEPTH, D, 2 * F), D ** -0.5),
        'ffn_conv_w': nrm((DEPTH, CONV_WIDTH, F), CONV_WIDTH ** -0.5),
        'ffn_conv_b': nrm((DEPTH, F), 0.02),
        'ffn_w_down': nrm((DEPTH, F, D), DN_BETA * F ** -0.5),
        'ln_ffn_g': 1.0 + nrm((DEPTH, D), 0.02),
        'ln_ffn_b': nrm((DEPTH, D), 0.02),
    }


def reference(x, ev_w_in, ev_lambda, ev_subln_g, ev_w_out, od_mu, od_w_rkv, od_w0, od_w1, od_w2, od_a0, od_a1, od_a2, od_g1, od_g2, od_k_k, od_k_a, od_r_k, od_lnx_g, od_lnx_b, od_w_out, ln_mix_g, ln_mix_b, ffn_w_up, ffn_conv_w, ffn_conv_b, ffn_w_down, ln_ffn_g, ln_ffn_b):
    B, S, _ = x.shape
    offsets = np.cumsum(EVEN_SPLIT_WIDTHS)[:-1].tolist()
    for i in range(DEPTH):
        j = i // 2
        if i % 2 == 0:
            h = x @ ev_w_in[j]
            dq, dk, dv, rq, rk, rv, rg = jnp.split(h, offsets, axis=-1)
            lambda_init = 0.8 - 0.6 * math.exp(-0.3 * i)
            lp = ev_lambda[j].astype(jnp.float32)
            lam = jnp.exp(jnp.sum(lp[0] * lp[1])) - jnp.exp(jnp.sum(lp[2] * lp[3])) + lambda_init
            a_out = diff_attention(dq.reshape(B, S, DIFF_HEADS, 2, DIFF_HEAD_DIM),
                                   dk.reshape(B, S, DIFF_HEADS, 2, DIFF_HEAD_DIM),
                                   dv.reshape(B, S, DIFF_HEADS, DIFF_V_DIM),
                                   lam, ev_subln_g[j], lambda_init)
            b_out = retention(rq.reshape(B, S, RET_HEADS, RET_QK_DIM),
                              rk.reshape(B, S, RET_HEADS, RET_QK_DIM),
                              rv.reshape(B, S, RET_HEADS, RET_V_DIM), rg)
            mix = jnp.concatenate([a_out, b_out], axis=-1) @ ev_w_out[j]
        else:
            mix = rwkv7_time_mix(x, od_mu[j], od_w_rkv[j], od_w0[j], od_w1[j], od_w2[j],
                                 od_a0[j], od_a1[j], od_a2[j], od_g1[j], od_g2[j],
                                 od_k_k[j], od_k_a[j], od_r_k[j], od_lnx_g[j], od_lnx_b[j], od_w_out[j])
        x = layer_norm(DN_ALPHA * x + mix, ln_mix_g[i], ln_mix_b[i])
        ffn = conv_glu_ffn(x, ffn_w_up[i], ffn_conv_w[i], ffn_conv_b[i], ffn_w_down[i])
        x = layer_norm(DN_ALPHA * x + ffn, ln_ffn_g[i], ln_ffn_b[i])
    return x
```

```python
import functools
import math

import jax
import jax.numpy as jnp
from jax import lax
from jax.experimental import pallas as pl
from jax.experimental.pallas import tpu as pltpu

F32 = jnp.float32
BF16 = jnp.bfloat16

D_MODEL = 1024
DEPTH = 2
DN_ALPHA = (2.0 * DEPTH) ** 0.25
LN_EPS = 1e-5

DIFF_HEAD_DIM = 64
DIFF_HEADS = 4
RET_QK_DIM = 64
RET_V_DIM = 128
RET_HEADS = 4
RET_CHUNK = 128
EVEN_IN_WIDTH = 3072
EVEN_QKV_WIDTH = 2560
RET_GATE_WIDTH = 512

RWKV_HEAD_DIM = 64
RWKV_GN_EPS = 64e-5
GATE_LORA = 160
GATE_LORA_PAD = 256
WKV_CHUNK = 64
WKV_TIME_BLOCK = 512

FFN_HIDDEN = 2816
FFN_CHUNK = 256
CONV_HALO = 16

LANE = 128
VMEM_LIMIT = 56 * 1024 * 1024


def _cparams(sem):
    return pltpu.CompilerParams(dimension_semantics=sem, vmem_limit_bytes=VMEM_LIMIT)


def _dot(a, b):
    return jnp.dot(a, b, preferred_element_type=F32)


def _dot_nt(a, b):
    return lax.dot_general(a, b, (((1,), (1,)), ((), ())), preferred_element_type=F32)


def _dot_tn(a, b):
    return lax.dot_general(a, b, (((0,), (0,)), ((), ())), preferred_element_type=F32)


def _split2(x):
    hi = x.astype(BF16)
    mid = (x - hi.astype(F32)).astype(BF16)
    return hi, mid


def _dot3(a, b, dot=_dot):
    ah, am = _split2(a)
    bh, bm = _split2(b)
    return dot(ah, bh) + (dot(ah, bm) + dot(am, bh))


def _layer_norm(z, g, b):
    mu = jnp.mean(z, axis=-1, keepdims=True)
    zc = z - mu
    var = jnp.mean(zc * zc, axis=-1, keepdims=True)
    return zc * lax.rsqrt(var + LN_EPS) * g + b


def _in_proj_kernel(x_ref, w_ref, qkv_ref, gate_ref):
    h = _dot(x_ref[...].astype(BF16), w_ref[...])
    qkv_ref[...] = h[:, :EVEN_QKV_WIDTH].astype(BF16)
    gate_ref[...] = h[:, EVEN_QKV_WIDTH:]


def _in_proj(x2d, w_bf16, tm=512):
    m = x2d.shape[0]
    return pl.pallas_call(
        _in_proj_kernel,
        grid=(m // tm,),
        in_specs=[pl.BlockSpec((tm, D_MODEL), lambda i: (i, 0)),
                  pl.BlockSpec((D_MODEL, EVEN_IN_WIDTH), lambda i: (0, 0))],
        out_specs=[pl.BlockSpec((tm, EVEN_QKV_WIDTH), lambda i: (i, 0)),
                   pl.BlockSpec((tm, RET_GATE_WIDTH), lambda i: (i, 0))],
        out_shape=[jax.ShapeDtypeStruct((m, EVEN_QKV_WIDTH), BF16),
                   jax.ShapeDtypeStruct((m, RET_GATE_WIDTH), F32)],
        compiler_params=_cparams(("parallel",)),
        name="in_proj",
    )(x2d, w_bf16)


def _diff_attn_kernel(lam_ref, g_ref, q_ref, k_ref, v_ref, o_ref, *, tq, lambda_init):
    h = pl.program_id(1)
    qi = pl.program_id(2)
    tk = tq
    slope = jnp.where(h == 0, 2.0 ** -2, jnp.where(h == 1, 2.0 ** -4,
                      jnp.where(h == 2, 2.0 ** -6, 2.0 ** -8))).astype(F32)
    lp = lam_ref[...]
    lam = (jnp.exp(jnp.sum(lp[0:1] * lp[1:2], axis=-1, keepdims=True))
           - jnp.exp(jnp.sum(lp[2:3] * lp[3:4], axis=-1, keepdims=True)) + lambda_init)

    lane = lax.broadcasted_iota(jnp.int32, (1, LANE), 1)
    q = q_ref[0] * jnp.asarray(DIFF_HEAD_DIM ** -0.5, BF16)
    zero = jnp.zeros_like(q)
    q1 = jnp.where(lane < DIFF_HEAD_DIM, q, zero)
    q2 = jnp.where(lane >= DIFF_HEAD_DIM, q, zero)
    col = lax.broadcasted_iota(jnp.int32, (1, tk), 1).astype(F32)

    def scores(j):
        start = pl.multiple_of(j * tk, tk)
        k = k_ref[0, pl.ds(start, tk), :]
        v = v_ref[0, pl.ds(start, tk), :]
        bias = (col + ((j - qi) * tk).astype(F32)) * slope
        return _dot_nt(q1, k) + bias, _dot_nt(q2, k) + bias, v

    def update(carry, s, v):
        m, l, acc = carry
        m_new = jnp.maximum(m, jnp.max(s, axis=-1, keepdims=True))
        alpha = jnp.exp(m - m_new)
        p = jnp.exp(s - m_new)
        l = alpha * l + jnp.sum(p, axis=-1, keepdims=True)
        acc = alpha * acc + _dot(p.astype(BF16), v)
        return m_new, l, acc

    def body(j, carry):
        c1, c2 = carry
        s1, s2, v = scores(j)
        return update(c1, s1, v), update(c2, s2, v)

    init = (jnp.full((tq, 1), -1e30, F32), jnp.zeros((tq, 1), F32), jnp.zeros((tq, LANE), F32))
    c1, c2 = lax.fori_loop(0, qi, body, (init, init))
    s1, s2, v = scores(qi)
    row_i = lax.broadcasted_iota(jnp.int32, (tq, tk), 0)
    col_i = lax.broadcasted_iota(jnp.int32, (tq, tk), 1)
    keep = col_i <= row_i
    c1 = update(c1, jnp.where(keep, s1, -1e30), v)
    c2 = update(c2, jnp.where(keep, s2, -1e30), v)
    o = c1[2] / c1[1] - lam * (c2[2] / c2[1])
    o = o * lax.rsqrt(jnp.mean(o * o, axis=-1, keepdims=True) + LN_EPS)
    o_ref[0] = (o * g_ref[...] * (1.0 - lambda_init)).astype(o_ref.dtype)


def _diff_attention(qkv, lam_p, subln_g, lambda_init, tq=256):
    b, s, _ = qkv.shape
    kern = functools.partial(_diff_attn_kernel, tq=tq, lambda_init=lambda_init)
    return pl.pallas_call(
        kern,
        grid=(b, DIFF_HEADS, s // tq),
        in_specs=[pl.BlockSpec((4, DIFF_HEAD_DIM), lambda bi, h, i: (0, 0)),
                  pl.BlockSpec((1, LANE), lambda bi, h, i: (0, 0)),
                  pl.BlockSpec((1, tq, LANE), lambda bi, h, i: (bi, i, h)),
                  pl.BlockSpec((1, s, LANE), lambda bi, h, i: (bi, 0, DIFF_HEADS + h)),
                  pl.BlockSpec((1, s, LANE), lambda bi, h, i: (bi, 0, 2 * DIFF_HEADS + h))],
        out_specs=pl.BlockSpec((1, tq, LANE), lambda bi, h, i: (bi, i, h)),
        out_shape=jax.ShapeDtypeStruct((b, s, DIFF_HEADS * LANE), BF16),
        compiler_params=_cparams(("parallel", "parallel", "arbitrary")),
        name="diff_attn",
    )(lam_p, subln_g.reshape(1, LANE), qkv, qkv, qkv)


def _retention_kernel(q_ref, k_ref, v_ref, g_ref, o_ref, state_ref):
    c = pl.program_id(1)
    C = RET_CHUNK

    @pl.when(c == 0)
    def _():
        state_ref[...] = jnp.zeros_like(state_ref)

    lane = lax.broadcasted_iota(jnp.int32, (1, LANE), 1)
    row128 = lax.broadcasted_iota(jnp.int32, (LANE, 1), 0)
    ri = lax.broadcasted_iota(jnp.int32, (C, C), 0)
    ci = lax.broadcasted_iota(jnp.int32, (C, C), 1)
    rel = (ri - ci).astype(F32)
    idx = lax.broadcasted_iota(jnp.int32, (C, 1), 0).astype(F32)
    q_all = q_ref[0].astype(F32)
    k_all = k_ref[0].astype(F32) * (RET_QK_DIM ** -0.5)
    for h in range(RET_HEADS):
        log_gamma = math.log1p(-(2.0 ** (-5.0 - h)))
        pair, half = divmod(h, 2)
        in_head = (lane >= half * RET_QK_DIM) & (lane < (half + 1) * RET_QK_DIM)
        q = jnp.where(in_head, q_all[:, pair * LANE:(pair + 1) * LANE], 0.0)
        k = k_all[:, pair * LANE:(pair + 1) * LANE]
        v = v_ref[0, :, h * RET_V_DIM:(h + 1) * RET_V_DIM]
        decay = jnp.where(rel >= 0, jnp.exp(log_gamma * jnp.maximum(rel, 0.0)), 0.0)
        sc = _dot_nt(q.astype(BF16), k.astype(BF16)) * decay
        inner = _dot(sc.astype(BF16), v)
        q_decay = jnp.exp(log_gamma * (idx + 1.0))
        k_decay = jnp.exp(log_gamma * (C - 1.0 - idx))
        state = state_ref[h]
        cross = _dot((q * q_decay).astype(BF16), state.astype(BF16))
        upd = _dot_tn((k * k_decay).astype(BF16), v)
        in_rows = (row128 >= half * RET_QK_DIM) & (row128 < (half + 1) * RET_QK_DIM)
        state_ref[h] = state * math.exp(log_gamma * C) + jnp.where(in_rows, upd, 0.0)
        y = inner + cross
        mu = jnp.mean(y, axis=-1, keepdims=True)
        yc = y - mu
        var = jnp.mean(yc * yc, axis=-1, keepdims=True)
        y = yc * lax.rsqrt(var + LN_EPS)
        g = g_ref[0, :, h * RET_V_DIM:(h + 1) * RET_V_DIM]
        o_ref[0, :, h * RET_V_DIM:(h + 1) * RET_V_DIM] = (g * jax.nn.sigmoid(g) * y).astype(o_ref.dtype)


def _retention(qkv, gate):
    b, s, _ = qkv.shape
    C = RET_CHUNK
    qk_w = RET_HEADS * RET_QK_DIM
    v_w = RET_HEADS * RET_V_DIM
    return pl.pallas_call(
        _retention_kernel,
        grid=(b, s // C),
        in_specs=[pl.BlockSpec((1, C, qk_w), lambda bi, c: (bi, c, 1536 // qk_w)),
                  pl.BlockSpec((1, C, qk_w), lambda bi, c: (bi, c, 1792 // qk_w)),
                  pl.BlockSpec((1, C, v_w), lambda bi, c: (bi, c, 2048 // v_w)),
                  pl.BlockSpec((1, C, v_w), lambda bi, c: (bi, c, 0))],
        out_specs=pl.BlockSpec((1, C, v_w), lambda bi, c: (bi, c, 0)),
        out_shape=jax.ShapeDtypeStruct((b, s, v_w), BF16),
        scratch_shapes=[pltpu.VMEM((RET_HEADS, LANE, RET_V_DIM), F32)],
        compiler_params=_cparams(("parallel", "arbitrary")),
        name="retention",
    )(qkv, qkv, qkv, gate)


def _even_out_kernel(a_ref, b_ref, x_ref, wa_ref, wb_ref, g_ref, beta_ref, o_ref):
    mix = _dot(a_ref[...], wa_ref[...]) + _dot(b_ref[...], wb_ref[...])
    o_ref[...] = _layer_norm(DN_ALPHA * x_ref[...] + mix, g_ref[...], beta_ref[...])


def _even_out(a2d, b2d, x2d, w_out_bf16, ln_g, ln_b, tm=512):
    m = x2d.shape[0]
    half = D_MODEL // 2
    row = lambda i: (i, 0)
    fixed = lambda i: (0, 0)
    return pl.pallas_call(
        _even_out_kernel,
        grid=(m // tm,),
        in_specs=[pl.BlockSpec((tm, half), row), pl.BlockSpec((tm, half), row),
                  pl.BlockSpec((tm, D_MODEL), row),
                  pl.BlockSpec((half, D_MODEL), fixed), pl.BlockSpec((half, D_MODEL), lambda i: (1, 0)),
                  pl.BlockSpec((1, D_MODEL), fixed), pl.BlockSpec((1, D_MODEL), fixed)],
        out_specs=pl.BlockSpec((tm, D_MODEL), row),
        out_shape=jax.ShapeDtypeStruct((m, D_MODEL), F32),
        compiler_params=_cparams(("parallel",)),
        name="even_out_ln",
    )(a2d, b2d, x2d, w_out_bf16, w_out_bf16, ln_g.reshape(1, -1), ln_b.reshape(1, -1))


def _odd_out_kernel(y_ref, gate_ref, x_ref, w_ref, g_ref, beta_ref, o_ref):
    mix = _dot((y_ref[...] * gate_ref[...]).astype(BF16), w_ref[...])
    o_ref[...] = _layer_norm(DN_ALPHA * x_ref[...] + mix, g_ref[...], beta_ref[...])


def _odd_out(y2d, gate2d, x2d, w_out_bf16, ln_g, ln_b, tm=512):
    m = x2d.shape[0]
    row = lambda i: (i, 0)
    fixed = lambda i: (0, 0)
    return pl.pallas_call(
        _odd_out_kernel,
        grid=(m // tm,),
        in_specs=[pl.BlockSpec((tm, D_MODEL), row), pl.BlockSpec((tm, D_MODEL), row),
                  pl.BlockSpec((tm, D_MODEL), row),
                  pl.BlockSpec((D_MODEL, D_MODEL), fixed),
                  pl.BlockSpec((1, D_MODEL), fixed), pl.BlockSpec((1, D_MODEL), fixed)],
        out_specs=pl.BlockSpec((tm, D_MODEL), row),
        out_shape=jax.ShapeDtypeStruct((m, D_MODEL), F32),
        compiler_params=_cparams(("parallel",)),
        name="odd_out_ln",
    )(y2d, gate2d, x2d, w_out_bf16, ln_g.reshape(1, -1), ln_b.reshape(1, -1))


def _ffn_kernel(x_ref, halo_ref, wu_ref, cw_ref, cb_ref, wd_ref, g_ref, beta_ref, o_ref, acc_ref, *, tm):
    i = pl.program_id(1)
    x = x_ref[0]
    halo = jnp.where(i > 0, halo_ref[0], 0.0)
    xe = jnp.concatenate([halo, x], axis=0).astype(BF16)
    n_chunks = FFN_HIDDEN // FFN_CHUNK
    for c in range(n_chunks):
        lo = c * FFN_CHUNK
        ue = _dot(xe, wu_ref[:, lo:lo + FFN_CHUNK])
        gate = _dot(xe[CONV_HALO:], wu_ref[:, FFN_HIDDEN + lo:FFN_HIDDEN + lo + FFN_CHUNK])
        cw = cw_ref[:, lo:lo + FFN_CHUNK]
        conv = (cb_ref[:, lo:lo + FFN_CHUNK]
                + ue[CONV_HALO - 2:CONV_HALO - 2 + tm] * cw[0:1]
                + ue[CONV_HALO - 1:CONV_HALO - 1 + tm] * cw[1:2]
                + ue[CONV_HALO:] * cw[2:3])
        act = 0.5 * conv * (1.0 + lax.erf(conv * (2.0 ** -0.5)))
        part = _dot((act * gate).astype(BF16), wd_ref[lo:lo + FFN_CHUNK, :])
        if c == 0:
            acc_ref[...] = part
        else:
            acc_ref[...] += part
    o_ref[0] = _layer_norm(DN_ALPHA * x + acc_ref[...], g_ref[...], beta_ref[...])


def _ffn(x, w_up_bf16, conv_w, conv_b, w_down_bf16, ln_g, ln_b, tm=512):
    b, s, _ = x.shape
    kern = functools.partial(_ffn_kernel, tm=tm)
    fixed = lambda bi, i: (0, 0)
    halo_blocks = tm // CONV_HALO
    single = pl.Buffered(1)
    return pl.pallas_call(
        kern,
        grid=(b, s // tm),
        in_specs=[pl.BlockSpec((1, tm, D_MODEL), lambda bi, i: (bi, i, 0)),
                  pl.BlockSpec((1, CONV_HALO, D_MODEL),
                               lambda bi, i: (bi, jnp.maximum(i * halo_blocks - 1, 0), 0)),
                  pl.BlockSpec((D_MODEL, 2 * FFN_HIDDEN), fixed, pipeline_mode=single),
                  pl.BlockSpec((3, FFN_HIDDEN), fixed),
                  pl.BlockSpec((1, FFN_HIDDEN), fixed),
                  pl.BlockSpec((FFN_HIDDEN, D_MODEL), fixed, pipeline_mode=single),
                  pl.BlockSpec((1, D_MODEL), fixed), pl.BlockSpec((1, D_MODEL), fixed)],
        out_specs=pl.BlockSpec((1, tm, D_MODEL), lambda bi, i: (bi, i, 0)),
        out_shape=jax.ShapeDtypeStruct((b, s, D_MODEL), F32),
        scratch_shapes=[pltpu.VMEM((tm, D_MODEL), F32)],
        compiler_params=_cparams(("parallel", "arbitrary")),
        name="conv_glu_ffn_ln",
    )(x, x, w_up_bf16, conv_w, conv_b.reshape(1, -1), w_down_bf16, ln_g.reshape(1, -1), ln_b.reshape(1, -1))


def _rwkv_proj_kernel(x_ref, halo_ref, mu_ref, wrkv_ref, w0_ref, w1_ref, w2_ref, a0_ref, a1_ref, a2_ref,
                      g1_ref, g2_ref, r_ref, k_ref, v_ref, wl_ref, a_ref, g_ref, *, tm):
    i = pl.program_id(1)
    x = x_ref[0]
    prev_row = jnp.where(i > 0, halo_ref[0, 7:8, :], 0.0)
    row = lax.broadcasted_iota(jnp.int32, (tm, 1), 0)
    x_prev = jnp.where(row == 0, prev_row, pltpu.roll(x, 1, 0))
    xx = x_prev - x

    def mix(n):
        return (x + xx * mu_ref[n:n + 1, :]).astype(BF16)

    r_ref[0] = _dot(mix(0), wrkv_ref[0])
    k_ref[0] = _dot(mix(1), wrkv_ref[1])
    v_ref[0] = _dot(mix(2), wrkv_ref[2])
    lw = jnp.tanh(_dot(mix(3), w1_ref[...])).astype(BF16)
    z = -(w0_ref[...] + _dot(lw, w2_ref[...]))
    softplus = jnp.maximum(z, 0.0) + jnp.log1p(jnp.exp(-jnp.abs(z)))
    wl_ref[0] = -jnp.exp(-softplus - 0.5)
    la = _dot(mix(4), a1_ref[...]).astype(BF16)
    a_ref[0] = jax.nn.sigmoid(a0_ref[...] + _dot(la, a2_ref[...]))
    lg = jax.nn.sigmoid(_dot(mix(5), g1_ref[...])).astype(BF16)
    g_ref[0] = _dot(lg, g2_ref[...])


def _rwkv_proj(x, mu, w_rkv, w0, w1, w2, a0, a1, a2, g1, g2, tm=512):
    b, s, d = x.shape
    kern = functools.partial(_rwkv_proj_kernel, tm=tm)
    fixed2 = lambda bi, i: (0, 0)
    tile = pl.BlockSpec((1, tm, d), lambda bi, i: (bi, i, 0))
    lora = w1.shape[1]
    out = jax.ShapeDtypeStruct((b, s, d), F32)
    return pl.pallas_call(
        kern,
        grid=(b, s // tm),
        in_specs=[tile,
                  pl.BlockSpec((1, 8, d), lambda bi, i: (bi, jnp.maximum(i * (tm // 8) - 1, 0), 0)),
                  pl.BlockSpec((6, d), fixed2),
                  pl.BlockSpec((3, d, d), lambda bi, i: (0, 0, 0)),
                  pl.BlockSpec((1, d), fixed2), pl.BlockSpec((d, lora), fixed2), pl.BlockSpec((lora, d), fixed2),
                  pl.BlockSpec((1, d), fixed2), pl.BlockSpec((d, lora), fixed2), pl.BlockSpec((lora, d), fixed2),
                  pl.BlockSpec((d, GATE_LORA_PAD), fixed2), pl.BlockSpec((GATE_LORA_PAD, d), fixed2)],
        out_specs=[tile] * 6,
        out_shape=[out] * 6,
        compiler_params=_cparams(("parallel", "arbitrary")),
        name="rwkv_proj",
    )(x, x, mu, w_rkv, w0.reshape(1, -1), w1, w2, a0.reshape(1, -1), a1, a2, g1, g2)


def _wkv_chunk(r, k, v, wl, lr, kk_scale, ka, rk, ln_g, ln_b, state, masks):
    tri_incl_bf16, strict, incl = masks
    T = WKV_CHUNK
    kk = k * kk_scale
    kk = kk / jnp.maximum(jnp.sqrt(jnp.sum(kk * kk, axis=-1, keepdims=True)), 1e-12)
    k = k * (1.0 + (lr - 1.0) * ka)
    a_vec = -kk
    b_vec = kk * lr

    w_hi = wl.astype(BF16)
    w_r1 = wl - w_hi.astype(F32)
    w_mid = w_r1.astype(BF16)
    w_lo = (w_r1 - w_mid.astype(F32)).astype(BF16)
    cl = _dot(tri_incl_bf16, w_hi) + (_dot(tri_incl_bf16, w_mid) + _dot(tri_incl_bf16, w_lo))
    cl_last = cl[T - 1:T, :]
    g_inv = jnp.exp(-cl)
    a_hat = a_vec * jnp.exp(cl - wl)
    b_hat = b_vec * g_inv
    k_hat = k * g_inv
    r_hat = r * jnp.exp(cl)
    to_end = jnp.exp(cl_last - cl)

    ar = jnp.concatenate([a_hat, r_hat], axis=0)
    m_b = _dot3(ar, b_hat, _dot_nt)
    m_k = _dot3(ar, k_hat, _dot_nt)
    a_ab = jnp.where(strict, m_b[:T], 0.0)
    a_ak = jnp.where(strict, m_k[:T], 0.0)
    a_rb = jnp.where(incl, m_b[T:], 0.0)
    a_rk = jnp.where(incl, m_k[T:], 0.0)

    ar_s = _dot3(ar, state, _dot_nt)
    u = ar_s[:T] + _dot3(a_ak, v)
    p = a_ab
    n_levels = int(math.log2(T))
    for level in range(n_levels):
        u = u + _dot3(p, u)
        if level + 1 < n_levels:
            p = _dot3(p, p)
    y = ar_s[T:] + _dot3(a_rb, u) + _dot3(a_rk, v)

    uv = jnp.concatenate([u, v], axis=0)
    bk = jnp.concatenate([b_vec * to_end, k * to_end], axis=0)
    new_state = state * jnp.exp(cl_last) + _dot3(uv, bk, _dot_tn)

    mu = jnp.mean(y, axis=-1, keepdims=True)
    yc = y - mu
    var = jnp.mean(yc * yc, axis=-1, keepdims=True)
    y = yc * lax.rsqrt(var + RWKV_GN_EPS) * ln_g + ln_b
    y = y + jnp.sum(r * k * rk, axis=-1, keepdims=True) * v
    return y, new_state


def _wkv_kernel(r_ref, k_ref, v_ref, wl_ref, a_ref, kk_ref, ka_ref, rk_ref, lg_ref, lb_ref, o_ref, state_ref):
    @pl.when(pl.program_id(2) == 0)
    def _():
        state_ref[...] = jnp.zeros_like(state_ref)

    T = WKV_CHUNK
    N = RWKV_HEAD_DIM
    ri = lax.broadcasted_iota(jnp.int32, (T, T), 0)
    ci = lax.broadcasted_iota(jnp.int32, (T, T), 1)
    incl = ci <= ri
    strict = ci < ri
    tri_incl_bf16 = jnp.where(incl, 1.0, 0.0).astype(BF16)
    masks = (tri_incl_bf16, strict, incl)
    heads = LANE // N

    def chunk_body(c, _):
        rows = pl.ds(pl.multiple_of(c * T, T), T)
        ys = []
        for hh in range(heads):
            cols = slice(hh * N, (hh + 1) * N)
            y, new_state = _wkv_chunk(
                r_ref[0, rows, cols], k_ref[0, rows, cols], v_ref[0, rows, cols],
                wl_ref[0, rows, cols], a_ref[0, rows, cols],
                kk_ref[:, cols], ka_ref[:, cols], rk_ref[:, cols], lg_ref[:, cols], lb_ref[:, cols],
                state_ref[hh], masks)
            state_ref[hh] = new_state
            ys.append(y)
        o_ref[0, rows, :] = jnp.concatenate(ys, axis=-1)
        return 0

    lax.fori_loop(0, WKV_TIME_BLOCK // T, chunk_body, 0)


def _wkv(r, k, v, wl, a, k_k, k_a, r_k, lnx_g, lnx_b):
    b, s, d = r.shape
    tb = WKV_TIME_BLOCK
    tile = pl.BlockSpec((1, tb, LANE), lambda bi, h, t: (bi, t, h))
    vec = pl.BlockSpec((1, LANE), lambda bi, h, t: (0, h))
    return pl.pallas_call(
        _wkv_kernel,
        grid=(b, d // LANE, s // tb),
        in_specs=[tile] * 5 + [vec] * 5,
        out_specs=tile,
        out_shape=jax.ShapeDtypeStruct((b, s, d), F32),
        scratch_shapes=[pltpu.VMEM((LANE // RWKV_HEAD_DIM, RWKV_HEAD_DIM, RWKV_HEAD_DIM), F32)],
        compiler_params=_cparams(("parallel", "parallel", "arbitrary")),
        name="wkv7",
    )(r, k, v, wl, a, *(t.reshape(1, -1) for t in (k_k, k_a, r_k, lnx_g, lnx_b)))


def kernel(x, ev_w_in, ev_lambda, ev_subln_g, ev_w_out, od_mu, od_w_rkv, od_w0, od_w1, od_w2, od_a0, od_a1, od_a2, od_g1, od_g2, od_k_k, od_k_a, od_r_k, od_lnx_g, od_lnx_b, od_w_out, ln_mix_g, ln_mix_b, ffn_w_up, ffn_conv_w, ffn_conv_b, ffn_w_down, ln_ffn_g, ln_ffn_b):
    b, s, d = x.shape
    m = b * s
    bf = lambda t: t.astype(BF16)

    qkv, gate = _in_proj(x.reshape(m, d), bf(ev_w_in[0]))
    qkv = qkv.reshape(b, s, EVEN_QKV_WIDTH)
    lambda_init = 0.8 - 0.6 * math.exp(-0.3 * 0)
    a_out = _diff_attention(qkv, ev_lambda[0], ev_subln_g[0], lambda_init)
    b_out = _retention(qkv, gate.reshape(b, s, RET_GATE_WIDTH))
    x1 = _even_out(a_out.reshape(m, -1), b_out.reshape(m, -1), x.reshape(m, d), bf(ev_w_out[0]),
                   ln_mix_g[0], ln_mix_b[0]).reshape(b, s, d)
    x2 = _ffn(x1, bf(ffn_w_up[0]), ffn_conv_w[0], ffn_conv_b[0], bf(ffn_w_down[0]), ln_ffn_g[0], ln_ffn_b[0])

    pad = GATE_LORA_PAD - GATE_LORA
    g1 = jnp.pad(bf(od_g1[0]), ((0, 0), (0, pad)))
    g2 = jnp.pad(bf(od_g2[0]), ((0, pad), (0, 0)))
    r, k, v, wl, a, g = _rwkv_proj(x2, od_mu[0], bf(od_w_rkv[0]), od_w0[0], bf(od_w1[0]), bf(od_w2[0]),
                                   od_a0[0], bf(od_a1[0]), bf(od_a2[0]), g1, g2)
    y = _wkv(r, k, v, wl, a, od_k_k[0], od_k_a[0], od_r_k[0], od_lnx_g[0], od_lnx_b[0])
    x3 = _odd_out(y.reshape(m, d), g.reshape(m, d), x2.reshape(m, d), bf(od_w_out[0]),
                  ln_mix_g[1], ln_mix_b[1]).reshape(b, s, d)
    x4 = _ffn(x3, bf(ffn_w_up[1]), ffn_conv_w[1], ffn_conv_b[1], bf(ffn_w_down[1]), ln_ffn_g[1], ln_ffn_b[1])
    return x4
```

```python
import functools
import math

import jax
import jax.numpy as jnp
from jax import lax
from jax.experimental import pallas as pl
from jax.experimental.pallas import tpu as pltpu

F32 = jnp.float32
BF16 = jnp.bfloat16

D_MODEL = 1024
DEPTH = 2
DN_ALPHA = (2.0 * DEPTH) ** 0.25
LN_EPS = 1e-5

DIFF_HEAD_DIM = 64
DIFF_HEADS = 4
RET_QK_DIM = 64
RET_V_DIM = 128
RET_HEADS = 4
RET_CHUNK = 128
EVEN_IN_WIDTH = 3072
EVEN_QKV_WIDTH = 2560
RET_GATE_WIDTH = 512

RWKV_HEAD_DIM = 64
RWKV_GN_EPS = 64e-5
GATE_LORA = 160
GATE_LORA_PAD = 256
WKV_CHUNK = 64
WKV_TIME_BLOCK = 512

FFN_HIDDEN = 2816
FFN_CHUNK = 256
CONV_HALO = 16

LANE = 128
VMEM_LIMIT = 56 * 1024 * 1024


def _cparams(sem):
    return pltpu.CompilerParams(dimension_semantics=sem, vmem_limit_bytes=VMEM_LIMIT)


def _dot(a, b):
    return jnp.dot(a, b, preferred_element_type=F32)


def _dot_nt(a, b):
    return lax.dot_general(a, b, (((1,), (1,)), ((), ())), preferred_element_type=F32)


def _dot_tn(a, b):
    return lax.dot_general(a, b, (((0,), (0,)), ((), ())), preferred_element_type=F32)


def _split2(x):
    hi = x.astype(BF16)
    mid = (x - hi.astype(F32)).astype(BF16)
    return hi, mid


def _dot3(a, b, dot=_dot):
    ah, am = _split2(a)
    bh, bm = _split2(b)
    return dot(ah, bh) + (dot(ah, bm) + dot(am, bh))


def _layer_norm(z, g, b):
    mu = jnp.mean(z, axis=-1, keepdims=True)
    zc = z - mu
    var = jnp.mean(zc * zc, axis=-1, keepdims=True)
    return zc * lax.rsqrt(var + LN_EPS) * g + b


def _in_proj_kernel(x_ref, w_ref, qkv_ref, gate_ref):
    h = _dot(x_ref[...].astype(BF16), w_ref[...])
    qkv_ref[...] = h[:, :EVEN_QKV_WIDTH].astype(BF16)
    gate_ref[...] = h[:, EVEN_QKV_WIDTH:]


def _in_proj(x2d, w_bf16, tm=512):
    m = x2d.shape[0]
    return pl.pallas_call(
        _in_proj_kernel,
        grid=(m // tm,),
        in_specs=[pl.BlockSpec((tm, D_MODEL), lambda i: (i, 0)),
                  pl.BlockSpec((D_MODEL, EVEN_IN_WIDTH), lambda i: (0, 0))],
        out_specs=[pl.BlockSpec((tm, EVEN_QKV_WIDTH), lambda i: (i, 0)),
                   pl.BlockSpec((tm, RET_GATE_WIDTH), lambda i: (i, 0))],
        out_shape=[jax.ShapeDtypeStruct((m, EVEN_QKV_WIDTH), BF16),
                   jax.ShapeDtypeStruct((m, RET_GATE_WIDTH), F32)],
        compiler_params=_cparams(("parallel",)),
        name="in_proj",
    )(x2d, w_bf16)


def _diff_attn_kernel(lam_ref, g_ref, q_ref, k_ref, v_ref, o_ref, *, tq, lambda_init):
    h = pl.program_id(1)
    qi = pl.program_id(2)
    tk = tq
    slope = jnp.where(h == 0, 2.0 ** -2, jnp.where(h == 1, 2.0 ** -4,
                      jnp.where(h == 2, 2.0 ** -6, 2.0 ** -8))).astype(F32)
    lp = lam_ref[...]
    lam = (jnp.exp(jnp.sum(lp[0:1] * lp[1:2], axis=-1, keepdims=True))
           - jnp.exp(jnp.sum(lp[2:3] * lp[3:4], axis=-1, keepdims=True)) + lambda_init)

    lane = lax.broadcasted_iota(jnp.int32, (1, LANE), 1)
    q = q_ref[0] * jnp.asarray(DIFF_HEAD_DIM ** -0.5, BF16)
    zero = jnp.zeros_like(q)
    q1 = jnp.where(lane < DIFF_HEAD_DIM, q, zero)
    q2 = jnp.where(lane >= DIFF_HEAD_DIM, q, zero)
    col = lax.broadcasted_iota(jnp.int32, (1, tk), 1).astype(F32)

    def scores(j):
        start = pl.multiple_of(j * tk, tk)
        k = k_ref[0, pl.ds(start, tk), :]
        v = v_ref[0, pl.ds(start, tk), :]
        bias = (col + ((j - qi) * tk).astype(F32)) * slope
        return _dot_nt(q1, k) + bias, _dot_nt(q2, k) + bias, v

    def update(carry, s, v):
        m, l, acc = carry
        m_new = jnp.maximum(m, jnp.max(s, axis=-1, keepdims=True))
        alpha = jnp.exp(m - m_new)
        p = jnp.exp(s - m_new)
        l = alpha * l + jnp.sum(p, axis=-1, keepdims=True)
        acc = alpha * acc + _dot(p.astype(BF16), v)
        return m_new, l, acc

    def body(j, carry):
        c1, c2 = carry
        s1, s2, v = scores(j)
        return update(c1, s1, v), update(c2, s2, v)

    init = (jnp.full((tq, 1), -1e30, F32), jnp.zeros((tq, 1), F32), jnp.zeros((tq, LANE), F32))
    c1, c2 = lax.fori_loop(0, qi, body, (init, init))
    s1, s2, v = scores(qi)
    row_i = lax.broadcasted_iota(jnp.int32, (tq, tk), 0)
    col_i = lax.broadcasted_iota(jnp.int32, (tq, tk), 1)
    keep = col_i <= row_i
    c1 = update(c1, jnp.where(keep, s1, -1e30), v)
    c2 = update(c2, jnp.where(keep, s2, -1e30), v)
    o = c1[2] / c1[1] - lam * (c2[2] / c2[1])
    o = o * lax.rsqrt(jnp.mean(o * o, axis=-1, keepdims=True) + LN_EPS)
    o_ref[0] = (o * g_ref[...] * (1.0 - lambda_init)).astype(o_ref.dtype)


def _diff_attention(qkv, lam_p, subln_g, lambda_init, tq=256):
    b, s, _ = qkv.shape
    kern = functools.partial(_diff_attn_kernel, tq=tq, lambda_init=lambda_init)
    return pl.pallas_call(
        kern,
        grid=(b, DIFF_HEADS, s // tq),
        in_specs=[pl.BlockSpec((4, DIFF_HEAD_DIM), lambda bi, h, i: (0, 0)),
                  pl.BlockSpec((1, LANE), lambda bi, h, i: (0, 0)),
                  pl.BlockSpec((1, tq, LANE), lambda bi, h, i: (bi, i, h)),
                  pl.BlockSpec((1, s, LANE), lambda bi, h, i: (bi, 0, DIFF_HEADS + h)),
                  pl.BlockSpec((1, s, LANE), lambda bi, h, i: (bi, 0, 2 * DIFF_HEADS + h))],
        out_specs=pl.BlockSpec((1, tq, LANE), lambda bi, h, i: (bi, i, h)),
        out_shape=jax.ShapeDtypeStruct((b, s, DIFF_HEADS * LANE), BF16),
        compiler_params=_cparams(("parallel", "parallel", "arbitrary")),
        name="diff_attn",
    )(lam_p, subln_g.reshape(1, LANE), qkv, qkv, qkv)


def _retention_kernel(q_ref, k_ref, v_ref, g_ref, o_ref, state_ref):
    c = pl.program_id(1)
    C = RET_CHUNK

    @pl.when(c == 0)
    def _():
        state_ref[...] = jnp.zeros_like(state_ref)

    lane = lax.broadcasted_iota(jnp.int32, (1, LANE), 1)
    row128 = lax.broadcasted_iota(jnp.int32, (LANE, 1), 0)
    ri = lax.broadcasted_iota(jnp.int32, (C, C), 0)
    ci = lax.broadcasted_iota(jnp.int32, (C, C), 1)
    rel = (ri - ci).astype(F32)
    idx = lax.broadcasted_iota(jnp.int32, (C, 1), 0).astype(F32)
    q_all = q_ref[0].astype(F32)
    k_all = k_ref[0].astype(F32) * (RET_QK_DIM ** -0.5)
    for h in range(RET_HEADS):
        log_gamma = math.log1p(-(2.0 ** (-5.0 - h)))
        pair, half = divmod(h, 2)
        in_head = (lane >= half * RET_QK_DIM) & (lane < (half + 1) * RET_QK_DIM)
        q = jnp.where(in_head, q_all[:, pair * LANE:(pair + 1) * LANE], 0.0)
        k = k_all[:, pair * LANE:(pair + 1) * LANE]
        v = v_ref[0, :, h * RET_V_DIM:(h + 1) * RET_V_DIM]
        decay = jnp.where(rel >= 0, jnp.exp(log_gamma * jnp.maximum(rel, 0.0)), 0.0)
        sc = _dot_nt(q.astype(BF16), k.astype(BF16)) * decay
        inner = _dot(sc.astype(BF16), v)
        q_decay = jnp.exp(log_gamma * (idx + 1.0))
        k_decay = jnp.exp(log_gamma * (C - 1.0 - idx))
        state = state_ref[h]
        cross = _dot((q * q_decay).astype(BF16), state.astype(BF16))
        upd = _dot_tn((k * k_decay).astype(BF16), v)
        in_rows = (row128 >= half * RET_QK_DIM) & (row128 < (half + 1) * RET_QK_DIM)
        state_ref[h] = state * math.exp(log_gamma * C) + jnp.where(in_rows, upd, 0.0)
        y = inner + cross
        mu = jnp.mean(y, axis=-1, keepdims=True)
        yc = y - mu
        var = jnp.mean(yc * yc, axis=-1, keepdims=True)
        y = yc * lax.rsqrt(var + LN_EPS)
        g = g_ref[0, :, h * RET_V_DIM:(h + 1) * RET_V_DIM]
        o_ref[0, :, h * RET_V_DIM:(h + 1) * RET_V_DIM] = (g * jax.nn.sigmoid(g) * y).astype(o_ref.dtype)


def _retention(qkv, gate):
    b, s, _ = qkv.shape
    C = RET_CHUNK
    qk_w = RET_HEADS * RET_QK_DIM
    v_w = RET_HEADS * RET_V_DIM
    return pl.pallas_call(
        _retention_kernel,
        grid=(b, s // C),
        in_specs=[pl.BlockSpec((1, C, qk_w), lambda bi, c: (bi, c, 1536 // qk_w)),
                  pl.BlockSpec((1, C, qk_w), lambda bi, c: (bi, c, 1792 // qk_w)),
                  pl.BlockSpec((1, C, v_w), lambda bi, c: (bi, c, 2048 // v_w)),
                  pl.BlockSpec((1, C, v_w), lambda bi, c: (bi, c, 0))],
        out_specs=pl.BlockSpec((1, C, v_w), lambda bi, c: (bi, c, 0)),
        out_shape=jax.ShapeDtypeStruct((b, s, v_w), BF16),
        scratch_shapes=[pltpu.VMEM((RET_HEADS, LANE, RET_V_DIM), F32)],
        compiler_params=_cparams(("parallel", "arbitrary")),
        name="retention",
    )(qkv, qkv, qkv, gate)


def _even_out_kernel(a_ref, b_ref, x_ref, wa_ref, wb_ref, g_ref, beta_ref, o_ref):
    mix = _dot(a_ref[...], wa_ref[...]) + _dot(b_ref[...], wb_ref[...])
    o_ref[...] = _layer_norm(DN_ALPHA * x_ref[...] + mix, g_ref[...], beta_ref[...])


def _even_out(a2d, b2d, x2d, w_out_bf16, ln_g, ln_b, tm=512):
    m = x2d.shape[0]
    half = D_MODEL // 2
    row = lambda i: (i, 0)
    fixed = lambda i: (0, 0)
    return pl.pallas_call(
        _even_out_kernel,
        grid=(m // tm,),
        in_specs=[pl.BlockSpec((tm, half), row), pl.BlockSpec((tm, half), row),
                  pl.BlockSpec((tm, D_MODEL), row),
                  pl.BlockSpec((half, D_MODEL), fixed), pl.BlockSpec((half, D_MODEL), lambda i: (1, 0)),
                  pl.BlockSpec((1, D_MODEL), fixed), pl.BlockSpec((1, D_MODEL), fixed)],
        out_specs=pl.BlockSpec((tm, D_MODEL), row),
        out_shape=jax.ShapeDtypeStruct((m, D_MODEL), F32),
        compiler_params=_cparams(("parallel",)),
        name="even_out_ln",
    )(a2d, b2d, x2d, w_out_bf16, w_out_bf16, ln_g.reshape(1, -1), ln_b.reshape(1, -1))


def _odd_out_kernel(y_ref, gate_ref, x_ref, w_ref, g_ref, beta_ref, o_ref):
    mix = _dot((y_ref[...] * gate_ref[...]).astype(BF16), w_ref[...])
    o_ref[...] = _layer_norm(DN_ALPHA * x_ref[...] + mix, g_ref[...], beta_ref[...])


def _odd_out(y2d, gate2d, x2d, w_out_bf16, ln_g, ln_b, tm=512):
    m = x2d.shape[0]
    row = lambda i: (i, 0)
    fixed = lambda i: (0, 0)
    return pl.pallas_call(
        _odd_out_kernel,
        grid=(m // tm,),
        in_specs=[pl.BlockSpec((tm, D_MODEL), row), pl.BlockSpec((tm, D_MODEL), row),
                  pl.BlockSpec((tm, D_MODEL), row),
                  pl.BlockSpec((D_MODEL, D_MODEL), fixed),
                  pl.BlockSpec((1, D_MODEL), fixed), pl.BlockSpec((1, D_MODEL), fixed)],
        out_specs=pl.BlockSpec((tm, D_MODEL), row),
        out_shape=jax.ShapeDtypeStruct((m, D_MODEL), F32),
        compiler_params=_cparams(("parallel",)),
        name="odd_out_ln",
    )(y2d, gate2d, x2d, w_out_bf16, ln_g.reshape(1, -1), ln_b.reshape(1, -1))


def _ffn_kernel(x_ref, halo_ref, wu_ref, cw_ref, cb_ref, wd_ref, g_ref, beta_ref, o_ref, acc_ref, *, tm):
    i = pl.program_id(1)
    x = x_ref[0]
    halo = jnp.where(i > 0, halo_ref[0], 0.0)
    xe = jnp.concatenate([halo, x], axis=0).astype(BF16)
    n_chunks = FFN_HIDDEN // FFN_CHUNK
    for c in range(n_chunks):
        lo = c * FFN_CHUNK
        ue = _dot(xe, wu_ref[:, lo:lo + FFN_CHUNK])
        gate = _dot(xe[CONV_HALO:], wu_ref[:, FFN_HIDDEN + lo:FFN_HIDDEN + lo + FFN_CHUNK])
        cw = cw_ref[:, lo:lo + FFN_CHUNK]
        conv = (cb_ref[:, lo:lo + FFN_CHUNK]
                + ue[CONV_HALO - 2:CONV_HALO - 2 + tm] * cw[0:1]
                + ue[CONV_HALO - 1:CONV_HALO - 1 + tm] * cw[1:2]
                + ue[CONV_HALO:] * cw[2:3])
        act = 0.5 * conv * (1.0 + lax.erf(conv * (2.0 ** -0.5)))
        part = _dot((act * gate).astype(BF16), wd_ref[lo:lo + FFN_CHUNK, :])
        if c == 0:
            acc_ref[...] = part
        else:
            acc_ref[...] += part
    o_ref[0] = _layer_norm(DN_ALPHA * x + acc_ref[...], g_ref[...], beta_ref[...])


def _ffn(x, w_up_bf16, conv_w, conv_b, w_down_bf16, ln_g, ln_b, tm=512):
    b, s, _ = x.shape
    kern = functools.partial(_ffn_kernel, tm=tm)
    fixed = lambda bi, i: (0, 0)
    halo_blocks = tm // CONV_HALO
    single = pl.Buffered(1)
    return pl.pallas_call(
        kern,
        grid=(b, s // tm),
        in_specs=[pl.BlockSpec((1, tm, D_MODEL), lambda bi, i: (bi, i, 0)),
                  pl.BlockSpec((1, CONV_HALO, D_MODEL),
                               lambda bi, i: (bi, jnp.maximum(i * halo_blocks - 1, 0), 0)),
                  pl.BlockSpec((D_MODEL, 2 * FFN_HIDDEN), fixed, pipeline_mode=single),
                  pl.BlockSpec((3, FFN_HIDDEN), fixed),
                  pl.BlockSpec((1, FFN_HIDDEN), fixed),
                  pl.BlockSpec((FFN_HIDDEN, D_MODEL), fixed, pipeline_mode=single),
                  pl.BlockSpec((1, D_MODEL), fixed), pl.BlockSpec((1, D_MODEL), fixed)],
        out_specs=pl.BlockSpec((1, tm, D_MODEL), lambda bi, i: (bi, i, 0)),
        out_shape=jax.ShapeDtypeStruct((b, s, D_MODEL), F32),
        scratch_shapes=[pltpu.VMEM((tm, D_MODEL), F32)],
        compiler_params=_cparams(("parallel", "arbitrary")),
        name="conv_glu_ffn_ln",
    )(x, x, w_up_bf16, conv_w, conv_b.reshape(1, -1), w_down_bf16, ln_g.reshape(1, -1), ln_b.reshape(1, -1))


def _rwkv_proj_kernel(x_ref, halo_ref, mu_ref, wrkv_ref, w0_ref, w1_ref, w2_ref, a0_ref, a1_ref, a2_ref,
                      g1_ref, g2_ref, r_ref, k_ref, v_ref, wl_ref, a_ref, g_ref, *, tm):
    i = pl.program_id(1)
    x = x_ref[0]
    prev_row = jnp.where(i > 0, halo_ref[0, 7:8, :], 0.0)
    row = lax.broadcasted_iota(jnp.int32, (tm, 1), 0)
    x_prev = jnp.where(row == 0, prev_row, pltpu.roll(x, 1, 0))
    xx = x_prev - x

    def mix(n):
        return (x + xx * mu_ref[n:n + 1, :]).astype(BF16)

    r_ref[0] = _dot(mix(0), wrkv_ref[0])
    k_ref[0] = _dot(mix(1), wrkv_ref[1])
    v_ref[0] = _dot(mix(2), wrkv_ref[2])
    lw = jnp.tanh(_dot(mix(3), w1_ref[...])).astype(BF16)
    z = -(w0_ref[...] + _dot(lw, w2_ref[...]))
    softplus = jnp.maximum(z, 0.0) + jnp.log1p(jnp.exp(-jnp.abs(z)))
    wl_ref[0] = -jnp.exp(-softplus - 0.5)
    la = _dot(mix(4), a1_ref[...]).astype(BF16)
    a_ref[0] = jax.nn.sigmoid(a0_ref[...] + _dot(la, a2_ref[...]))
    lg = jax.nn.sigmoid(_dot(mix(5), g1_ref[...])).astype(BF16)
    g_ref[0] = _dot(lg, g2_ref[...])


def _rwkv_proj(x, mu, w_rkv, w0, w1, w2, a0, a1, a2, g1, g2, tm=512):
    b, s, d = x.shape
    kern = functools.partial(_rwkv_proj_kernel, tm=tm)
    fixed2 = lambda bi, i: (0, 0)
    tile = pl.BlockSpec((1, tm, d), lambda bi, i: (bi, i, 0))
    lora = w1.shape[1]
    out = jax.ShapeDtypeStruct((b, s, d), F32)
    return pl.pallas_call(
        kern,
        grid=(b, s // tm),
        in_specs=[tile,
                  pl.BlockSpec((1, 8, d), lambda bi, i: (bi, jnp.maximum(i * (tm // 8) - 1, 0), 0)),
                  pl.BlockSpec((6, d), fixed2),
                  pl.BlockSpec((3, d, d), lambda bi, i: (0, 0, 0)),
                  pl.BlockSpec((1, d), fixed2), pl.BlockSpec((d, lora), fixed2), pl.BlockSpec((lora, d), fixed2),
                  pl.BlockSpec((1, d), fixed2), pl.BlockSpec((d, lora), fixed2), pl.BlockSpec((lora, d), fixed2),
                  pl.BlockSpec((d, GATE_LORA_PAD), fixed2), pl.BlockSpec((GATE_LORA_PAD, d), fixed2)],
        out_specs=[tile] * 6,
        out_shape=[out] * 6,
        compiler_params=_cparams(("parallel", "arbitrary")),
        name="rwkv_proj",
    )(x, x, mu, w_rkv, w0.reshape(1, -1), w1, w2, a0.reshape(1, -1), a1, a2, g1, g2)


def _each(f, *lists):
    return [f(*xs) for xs in zip(*lists)]


def _wkv_prepare(r, k, v, wl, lr, kk_scale, ka, rk, masks):
    tri_incl_bf16, strict, incl, eye = masks
    T = WKV_CHUNK
    bf = lambda t: t.astype(BF16)
    kk = _each(lambda k_, s_: k_ * s_, k, kk_scale)
    kk = _each(lambda t: t / jnp.maximum(jnp.sqrt(jnp.sum(t * t, axis=-1, keepdims=True)), 1e-12), kk)
    k = _each(lambda k_, lr_, ka_: k_ * (1.0 + (lr_ - 1.0) * ka_), k, lr, ka)
    b_vec = _each(lambda kk_, lr_: kk_ * lr_, kk, lr)

    w_hi = _each(bf, wl)
    w_r1 = _each(lambda w_, h_: w_ - h_.astype(F32), wl, w_hi)
    w_mid = _each(bf, w_r1)
    w_lo = _each(lambda r1, m_: (r1 - m_.astype(F32)).astype(BF16), w_r1, w_mid)
    cl = _each(lambda h_, m_, l_: _dot(tri_incl_bf16, h_) + (_dot(tri_incl_bf16, m_) + _dot(tri_incl_bf16, l_)),
               w_hi, w_mid, w_lo)
    cl_last = _each(lambda c_: c_[T - 1:T, :], cl)
    g_inv = _each(lambda c_: jnp.exp(-c_), cl)
    a_hat = _each(lambda kk_, c_, w_: -kk_ * jnp.exp(c_ - w_), kk, cl, wl)
    b_hat = _each(lambda b_, g_: b_ * g_, b_vec, g_inv)
    k_hat = _each(lambda k_, g_: bf(k_ * g_), k, g_inv)
    r_hat = _each(lambda r_, c_: r_ * jnp.exp(c_), r, cl)
    to_end = _each(lambda l_, c_: jnp.exp(l_ - c_), cl_last, cl)
    b_end = _each(lambda b_, e_: b_ * e_, b_vec, to_end)
    k_end = _each(lambda k_, e_: bf(k_ * e_), k, to_end)

    a_ab = _each(lambda a_, b_: jnp.where(strict, _dot3(a_, b_, _dot_nt), 0.0), a_hat, b_hat)
    ar = _each(lambda a_, r_: bf(jnp.concatenate([a_, r_], axis=0)), a_hat, r_hat)
    m_k = _each(_dot_nt, ar, k_hat)
    a_ak = _each(lambda m_: bf(jnp.where(strict, m_[:T], 0.0)), m_k)
    a_rk = _each(lambda m_: bf(jnp.where(incl, m_[T:], 0.0)), m_k)
    a_rb = _each(lambda ar_, b_: bf(jnp.where(incl, _dot_nt(ar_[T:], bf(b_)), 0.0)), ar, b_hat)

    p = a_ab
    minv = _each(lambda p_: jnp.where(eye, 1.0, 0.0) + p_, p)
    for _ in range(int(math.log2(T)) - 1):
        p = _each(lambda p_: _dot3(p_, p_), p)
        minv = _each(lambda m_, p_: m_ + _dot3(p_, m_), minv, p)

    v_bf16 = _each(bf, v)
    a_t = _each(_dot3, minv, a_hat)
    akv = _each(lambda a_, v_: bf(_dot(a_, v_)), a_ak, v_bf16)
    w_bf16 = _each(lambda m_, x_: bf(_dot(bf(m_), x_)), minv, akv)
    r_t = _each(lambda r_, a_, t_: r_ + _dot(a_, bf(t_)), r_hat, a_rb, a_t)
    y0 = _each(lambda arb_, w_, ark_, v_: _dot(arb_, w_) + _dot(ark_, v_), a_rb, w_bf16, a_rk, v_bf16)
    trans = _each(lambda l_, t_, b_: jnp.where(eye, jnp.exp(l_), 0.0) + _dot3(t_, b_, _dot_tn),
                  cl_last, a_t, b_end)
    s_add = _each(lambda w_, b_, v_, k_: _dot_tn(w_, bf(b_)) + _dot_tn(v_, k_),
                  w_bf16, b_end, v_bf16, k_end)
    bonus = _each(lambda r_, k_, rk_, v_: jnp.sum(r_ * k_ * rk_, axis=-1, keepdims=True) * v_, r, k, rk, v)
    return trans, s_add, r_t, y0, bonus


def _wkv_kernel(r_ref, k_ref, v_ref, wl_ref, a_ref, kk_ref, ka_ref, rk_ref, lg_ref, lb_ref, o_ref,
                state_ref, trans_ref, sadd_ref, rt_ref, y0_ref, bonus_ref, *, heads, unroll):
    @pl.when(pl.program_id(2) == 0)
    def _():
        state_ref[...] = jnp.zeros_like(state_ref)

    T = WKV_CHUNK
    N = RWKV_HEAD_DIM
    n_chunks = WKV_TIME_BLOCK // T
    ri = lax.broadcasted_iota(jnp.int32, (T, T), 0)
    ci = lax.broadcasted_iota(jnp.int32, (T, T), 1)
    incl = ci <= ri
    strict = ci < ri
    eye = ci == ri
    tri_incl_bf16 = jnp.where(incl, 1.0, 0.0).astype(BF16)
    masks = (tri_incl_bf16, strict, incl, eye)

    def prepare_body(it, _):
        problems = [(it * unroll + cc, hh) for cc in range(unroll) for hh in range(heads)]

        def tiles(ref):
            return [ref[0, pl.ds(pl.multiple_of(c * T, T), T), hh * N:(hh + 1) * N] for c, hh in problems]

        def vecs(ref):
            return [ref[:, hh * N:(hh + 1) * N] for _, hh in problems]

        outs = _wkv_prepare(tiles(r_ref), tiles(k_ref), tiles(v_ref), tiles(wl_ref), tiles(a_ref),
                            vecs(kk_ref), vecs(ka_ref), vecs(rk_ref), masks)
        for ref, vals in zip((trans_ref, sadd_ref, rt_ref, y0_ref, bonus_ref), outs):
            for (c, hh), val in zip(problems, vals):
                ref[c, hh] = val
        return 0

    lax.fori_loop(0, n_chunks // unroll, prepare_body, 0)

    def scan_body(c, _):
        rows = pl.ds(pl.multiple_of(c * T, T), T)
        ys = []
        for hh in range(heads):
            cols = slice(hh * N, (hh + 1) * N)
            state = state_ref[hh]
            y = _dot_nt(rt_ref[c, hh].astype(BF16), state.astype(BF16)) + y0_ref[c, hh]
            state_ref[hh] = _dot3(state, trans_ref[c, hh]) + sadd_ref[c, hh]
            mu = jnp.mean(y, axis=-1, keepdims=True)
            yc = y - mu
            var = jnp.mean(yc * yc, axis=-1, keepdims=True)
            y = yc * lax.rsqrt(var + RWKV_GN_EPS) * lg_ref[:, cols] + lb_ref[:, cols]
            ys.append(y + bonus_ref[c, hh])
        o_ref[0, rows, :] = jnp.concatenate(ys, axis=-1)
        return 0

    lax.fori_loop(0, n_chunks, scan_body, 0)


def _wkv(r, k, v, wl, a, k_k, k_a, r_k, lnx_g, lnx_b, heads=4, unroll=2):
    b, s, d = r.shape
    tb = WKV_TIME_BLOCK
    width = heads * RWKV_HEAD_DIM
    n_chunks = tb // WKV_CHUNK
    tile = pl.BlockSpec((1, tb, width), lambda bi, h, t: (bi, t, h))
    vec = pl.BlockSpec((1, width), lambda bi, h, t: (0, h))
    per_chunk = pltpu.VMEM((n_chunks, heads, WKV_CHUNK, RWKV_HEAD_DIM), F32)
    kern = functools.partial(_wkv_kernel, heads=heads, unroll=unroll)
    return pl.pallas_call(
        kern,
        grid=(b, d // width, s // tb),
        in_specs=[tile] * 5 + [vec] * 5,
        out_specs=tile,
        out_shape=jax.ShapeDtypeStruct((b, s, d), F32),
        scratch_shapes=[pltpu.VMEM((heads, RWKV_HEAD_DIM, RWKV_HEAD_DIM), F32)] + [per_chunk] * 5,
        compiler_params=_cparams(("parallel", "parallel", "arbitrary")),
        name="wkv7",
    )(r, k, v, wl, a, *(t.reshape(1, -1) for t in (k_k, k_a, r_k, lnx_g, lnx_b)))


def kernel(x, ev_w_in, ev_lambda, ev_subln_g, ev_w_out, od_mu, od_w_rkv, od_w0, od_w1, od_w2, od_a0, od_a1, od_a2, od_g1, od_g2, od_k_k, od_k_a, od_r_k, od_lnx_g, od_lnx_b, od_w_out, ln_mix_g, ln_mix_b, ffn_w_up, ffn_conv_w, ffn_conv_b, ffn_w_down, ln_ffn_g, ln_ffn_b):
    b, s, d = x.shape
    m = b * s
    bf = lambda t: t.astype(BF16)

    qkv, gate = _in_proj(x.reshape(m, d), bf(ev_w_in[0]))
    qkv = qkv.reshape(b, s, EVEN_QKV_WIDTH)
    lambda_init = 0.8 - 0.6 * math.exp(-0.3 * 0)
    a_out = _diff_attention(qkv, ev_lambda[0], ev_subln_g[0], lambda_init)
    b_out = _retention(qkv, gate.reshape(b, s, RET_GATE_WIDTH))
    x1 = _even_out(a_out.reshape(m, -1), b_out.reshape(m, -1), x.reshape(m, d), bf(ev_w_out[0]),
                   ln_mix_g[0], ln_mix_b[0]).reshape(b, s, d)
    x2 = _ffn(x1, bf(ffn_w_up[0]), ffn_conv_w[0], ffn_conv_b[0], bf(ffn_w_down[0]), ln_ffn_g[0], ln_ffn_b[0])

    pad = GATE_LORA_PAD - GATE_LORA
    g1 = jnp.pad(bf(od_g1[0]), ((0, 0), (0, pad)))
    g2 = jnp.pad(bf(od_g2[0]), ((0, pad), (0, 0)))
    r, k, v, wl, a, g = _rwkv_proj(x2, od_mu[0], bf(od_w_rkv[0]), od_w0[0], bf(od_w1[0]), bf(od_w2[0]),
                                   od_a0[0], bf(od_a1[0]), bf(od_a2[0]), g1, g2)
    y = _wkv(r, k, v, wl, a, od_k_k[0], od_k_a[0], od_r_k[0], od_lnx_g[0], od_lnx_b[0])
    x3 = _odd_out(y.reshape(m, d), g.reshape(m, d), x2.reshape(m, d), bf(od_w_out[0]),
                  ln_mix_g[1], ln_mix_b[1]).reshape(b, s, d)
    x4 = _ffn(x3, bf(ffn_w_up[1]), ffn_conv_w[1], ffn_conv_b[1], bf(ffn_w_down[1]), ln_ffn_g[1], ln_ffn_b[1])
    return x4
```

```python
import functools
import math

import jax
import jax.numpy as jnp
from jax import lax
from jax.experimental import pallas as pl
from jax.experimental.pallas import tpu as pltpu

F32 = jnp.float32
BF16 = jnp.bfloat16

D_MODEL = 1024
DEPTH = 2
DN_ALPHA = (2.0 * DEPTH) ** 0.25
LN_EPS = 1e-5

DIFF_HEAD_DIM = 64
DIFF_HEADS = 4
RET_QK_DIM = 64
RET_V_DIM = 128
RET_HEADS = 4
RET_CHUNK = 128
EVEN_IN_WIDTH = 3072
EVEN_QKV_WIDTH = 2560
RET_GATE_WIDTH = 512

RWKV_HEAD_DIM = 64
RWKV_GN_EPS = 64e-5
GATE_LORA = 160
GATE_LORA_PAD = 256
WKV_CHUNK = 64
WKV_INV_BASE = 8
WKV_TIME_BLOCK = 1024

FFN_HIDDEN = 2816
FFN_CHUNK = 256
CONV_HALO = 16

LANE = 128
VMEM_LIMIT = 56 * 1024 * 1024


def _cparams(sem):
    return pltpu.CompilerParams(dimension_semantics=sem, vmem_limit_bytes=VMEM_LIMIT)


def _dot(a, b):
    return jnp.dot(a, b, preferred_element_type=F32)


def _dot_nt(a, b):
    return lax.dot_general(a, b, (((1,), (1,)), ((), ())), preferred_element_type=F32)


def _dot_tn(a, b):
    return lax.dot_general(a, b, (((0,), (0,)), ((), ())), preferred_element_type=F32)


def _split2(x):
    hi = x.astype(BF16)
    mid = (x - hi.astype(F32)).astype(BF16)
    return hi, mid


def _dot3(a, b, dot=_dot):
    ah, am = _split2(a)
    bh, bm = _split2(b)
    return dot(ah, bh) + (dot(ah, bm) + dot(am, bh))


def _layer_norm(z, g, b):
    mu = jnp.mean(z, axis=-1, keepdims=True)
    zc = z - mu
    var = jnp.mean(zc * zc, axis=-1, keepdims=True)
    return zc * lax.rsqrt(var + LN_EPS) * g + b


def _in_proj_kernel(x_ref, w_ref, qkv_ref, gate_ref):
    h = _dot(x_ref[...].astype(BF16), w_ref[...])
    qkv_ref[...] = h[:, :EVEN_QKV_WIDTH].astype(BF16)
    gate_ref[...] = h[:, EVEN_QKV_WIDTH:]


def _in_proj(x2d, w_bf16, tm=512):
    m = x2d.shape[0]
    return pl.pallas_call(
        _in_proj_kernel,
        grid=(m // tm,),
        in_specs=[pl.BlockSpec((tm, D_MODEL), lambda i: (i, 0)),
                  pl.BlockSpec((D_MODEL, EVEN_IN_WIDTH), lambda i: (0, 0))],
        out_specs=[pl.BlockSpec((tm, EVEN_QKV_WIDTH), lambda i: (i, 0)),
                   pl.BlockSpec((tm, RET_GATE_WIDTH), lambda i: (i, 0))],
        out_shape=[jax.ShapeDtypeStruct((m, EVEN_QKV_WIDTH), BF16),
                   jax.ShapeDtypeStruct((m, RET_GATE_WIDTH), F32)],
        compiler_params=_cparams(("parallel",)),
        name="in_proj",
    )(x2d, w_bf16)


def _diff_attn_kernel(lam_ref, g_ref, q_ref, k_ref, v_ref, o_ref, *, tq, lambda_init):
    h = pl.program_id(1)
    qi = pl.program_id(2)
    tk = tq
    slope = jnp.where(h == 0, 2.0 ** -2, jnp.where(h == 1, 2.0 ** -4,
                      jnp.where(h == 2, 2.0 ** -6, 2.0 ** -8))).astype(F32)
    lp = lam_ref[...]
    lam = (jnp.exp(jnp.sum(lp[0:1] * lp[1:2], axis=-1, keepdims=True))
           - jnp.exp(jnp.sum(lp[2:3] * lp[3:4], axis=-1, keepdims=True)) + lambda_init)

    lane = lax.broadcasted_iota(jnp.int32, (1, LANE), 1)
    q = q_ref[0] * jnp.asarray(DIFF_HEAD_DIM ** -0.5, BF16)
    zero = jnp.zeros_like(q)
    q_pos = jnp.broadcast_to(jnp.where(lane < 2, slope, 0.0).astype(BF16), (tq, LANE))
    q1 = jnp.concatenate([jnp.where(lane < DIFF_HEAD_DIM, q, zero), q_pos], axis=1)
    q2 = jnp.concatenate([jnp.where(lane >= DIFF_HEAD_DIM, q, zero), q_pos], axis=1)
    key_off = lax.broadcasted_iota(jnp.int32, (tk, LANE), 0)
    k_pos = jnp.where(lane == 0, key_off & 255, jnp.where(lane == 1, key_off & ~255, 0)).astype(F32).astype(BF16)

    def scores(j):
        start = pl.multiple_of(j * tk, tk)
        k = jnp.concatenate([k_ref[0, pl.ds(start, tk), :], k_pos], axis=1)
        v = v_ref[0, pl.ds(start, tk), :]
        return _dot_nt(q1, k), _dot_nt(q2, k), v

    def update(carry, s, v, shift):
        m, l, acc = carry
        m_new = jnp.maximum(m, jnp.max(s, axis=-1, keepdims=True) + shift)
        alpha = jnp.exp(m - m_new)
        p = jnp.exp(s - (m_new - shift))
        l = alpha * l + jnp.sum(p, axis=-1, keepdims=True)
        acc = alpha * acc + _dot(p.astype(BF16), v)
        return m_new, l, acc

    def body(j, carry):
        c1, c2 = carry
        s1, s2, v = scores(j)
        shift = ((j - qi) * tk).astype(F32) * slope
        return update(c1, s1, v, shift), update(c2, s2, v, shift)

    init = (jnp.full((tq, 1), -1e30, F32), jnp.zeros((tq, 1), F32), jnp.zeros((tq, LANE), F32))
    c1, c2 = lax.fori_loop(0, qi, body, (init, init))
    s1, s2, v = scores(qi)
    row_i = lax.broadcasted_iota(jnp.int32, (tq, tk), 0)
    col_i = lax.broadcasted_iota(jnp.int32, (tq, tk), 1)
    keep = col_i <= row_i
    c1 = update(c1, jnp.where(keep, s1, -1e30), v, 0.0)
    c2 = update(c2, jnp.where(keep, s2, -1e30), v, 0.0)
    o = c1[2] / c1[1] - lam * (c2[2] / c2[1])
    o = o * lax.rsqrt(jnp.mean(o * o, axis=-1, keepdims=True) + LN_EPS)
    o_ref[0] = (o * g_ref[...] * (1.0 - lambda_init)).astype(o_ref.dtype)


def _diff_attention(qkv, lam_p, subln_g, lambda_init, tq=512):
    b, s, _ = qkv.shape
    kern = functools.partial(_diff_attn_kernel, tq=tq, lambda_init=lambda_init)
    return pl.pallas_call(
        kern,
        grid=(b, DIFF_HEADS, s // tq),
        in_specs=[pl.BlockSpec((4, DIFF_HEAD_DIM), lambda bi, h, i: (0, 0)),
                  pl.BlockSpec((1, LANE), lambda bi, h, i: (0, 0)),
                  pl.BlockSpec((1, tq, LANE), lambda bi, h, i: (bi, i, h)),
                  pl.BlockSpec((1, s, LANE), lambda bi, h, i: (bi, 0, DIFF_HEADS + h)),
                  pl.BlockSpec((1, s, LANE), lambda bi, h, i: (bi, 0, 2 * DIFF_HEADS + h))],
        out_specs=pl.BlockSpec((1, tq, LANE), lambda bi, h, i: (bi, i, h)),
        out_shape=jax.ShapeDtypeStruct((b, s, DIFF_HEADS * LANE), BF16),
        compiler_params=_cparams(("parallel", "parallel", "arbitrary")),
        name="diff_attn",
    )(lam_p, subln_g.reshape(1, LANE), qkv, qkv, qkv)


def _retention_kernel(q_ref, k_ref, v_ref, g_ref, o_ref, state_ref):
    c = pl.program_id(1)
    C = RET_CHUNK

    @pl.when(c == 0)
    def _():
        state_ref[...] = jnp.zeros_like(state_ref)

    lane = lax.broadcasted_iota(jnp.int32, (1, LANE), 1)
    row128 = lax.broadcasted_iota(jnp.int32, (LANE, 1), 0)
    ri = lax.broadcasted_iota(jnp.int32, (C, C), 0)
    ci = lax.broadcasted_iota(jnp.int32, (C, C), 1)
    rel = (ri - ci).astype(F32)
    idx = lax.broadcasted_iota(jnp.int32, (C, 1), 0).astype(F32)
    q_all = q_ref[0].astype(F32)
    k_all = k_ref[0].astype(F32) * (RET_QK_DIM ** -0.5)
    for h in range(RET_HEADS):
        log_gamma = math.log1p(-(2.0 ** (-5.0 - h)))
        pair, half = divmod(h, 2)
        in_head = (lane >= half * RET_QK_DIM) & (lane < (half + 1) * RET_QK_DIM)
        q = jnp.where(in_head, q_all[:, pair * LANE:(pair + 1) * LANE], 0.0)
        k = k_all[:, pair * LANE:(pair + 1) * LANE]
        v = v_ref[0, :, h * RET_V_DIM:(h + 1) * RET_V_DIM]
        decay = jnp.where(rel >= 0, jnp.exp(log_gamma * jnp.maximum(rel, 0.0)), 0.0)
        sc = _dot_nt(q.astype(BF16), k.astype(BF16)) * decay
        inner = _dot(sc.astype(BF16), v)
        q_decay = jnp.exp(log_gamma * (idx + 1.0))
        k_decay = jnp.exp(log_gamma * (C - 1.0 - idx))
        state = state_ref[h]
        cross = _dot((q * q_decay).astype(BF16), state.astype(BF16))
        upd = _dot_tn((k * k_decay).astype(BF16), v)
        in_rows = (row128 >= half * RET_QK_DIM) & (row128 < (half + 1) * RET_QK_DIM)
        state_ref[h] = state * math.exp(log_gamma * C) + jnp.where(in_rows, upd, 0.0)
        y = inner + cross
        mu = jnp.mean(y, axis=-1, keepdims=True)
        yc = y - mu
        var = jnp.mean(yc * yc, axis=-1, keepdims=True)
        y = yc * lax.rsqrt(var + LN_EPS)
        g = g_ref[0, :, h * RET_V_DIM:(h + 1) * RET_V_DIM]
        o_ref[0, :, h * RET_V_DIM:(h + 1) * RET_V_DIM] = (g * jax.nn.sigmoid(g) * y).astype(o_ref.dtype)


def _retention(qkv, gate):
    b, s, _ = qkv.shape
    C = RET_CHUNK
    qk_w = RET_HEADS * RET_QK_DIM
    v_w = RET_HEADS * RET_V_DIM
    return pl.pallas_call(
        _retention_kernel,
        grid=(b, s // C),
        in_specs=[pl.BlockSpec((1, C, qk_w), lambda bi, c: (bi, c, 1536 // qk_w)),
                  pl.BlockSpec((1, C, qk_w), lambda bi, c: (bi, c, 1792 // qk_w)),
                  pl.BlockSpec((1, C, v_w), lambda bi, c: (bi, c, 2048 // v_w)),
                  pl.BlockSpec((1, C, v_w), lambda bi, c: (bi, c, 0))],
        out_specs=pl.BlockSpec((1, C, v_w), lambda bi, c: (bi, c, 0)),
        out_shape=jax.ShapeDtypeStruct((b, s, v_w), BF16),
        scratch_shapes=[pltpu.VMEM((RET_HEADS, LANE, RET_V_DIM), F32)],
        compiler_params=_cparams(("parallel", "arbitrary")),
        name="retention",
    )(qkv, qkv, qkv, gate)


def _even_out_kernel(a_ref, b_ref, x_ref, wa_ref, wb_ref, g_ref, beta_ref, o_ref):
    mix = _dot(a_ref[...], wa_ref[...]) + _dot(b_ref[...], wb_ref[...])
    o_ref[...] = _layer_norm(DN_ALPHA * x_ref[...] + mix, g_ref[...], beta_ref[...])


def _even_out(a2d, b2d, x2d, w_out_bf16, ln_g, ln_b, tm=512):
    m = x2d.shape[0]
    half = D_MODEL // 2
    row = lambda i: (i, 0)
    fixed = lambda i: (0, 0)
    return pl.pallas_call(
        _even_out_kernel,
        grid=(m // tm,),
        in_specs=[pl.BlockSpec((tm, half), row), pl.BlockSpec((tm, half), row),
                  pl.BlockSpec((tm, D_MODEL), row),
                  pl.BlockSpec((half, D_MODEL), fixed), pl.BlockSpec((half, D_MODEL), lambda i: (1, 0)),
                  pl.BlockSpec((1, D_MODEL), fixed), pl.BlockSpec((1, D_MODEL), fixed)],
        out_specs=pl.BlockSpec((tm, D_MODEL), row),
        out_shape=jax.ShapeDtypeStruct((m, D_MODEL), F32),
        compiler_params=_cparams(("parallel",)),
        name="even_out_ln",
    )(a2d, b2d, x2d, w_out_bf16, w_out_bf16, ln_g.reshape(1, -1), ln_b.reshape(1, -1))


def _odd_out_kernel(y_ref, gate_ref, x_ref, w_ref, g_ref, beta_ref, o_ref):
    mix = _dot((y_ref[...] * gate_ref[...]).astype(BF16), w_ref[...])
    o_ref[...] = _layer_norm(DN_ALPHA * x_ref[...] + mix, g_ref[...], beta_ref[...])


def _odd_out(y2d, gate2d, x2d, w_out_bf16, ln_g, ln_b, tm=512):
    m = x2d.shape[0]
    row = lambda i: (i, 0)
    fixed = lambda i: (0, 0)
    return pl.pallas_call(
        _odd_out_kernel,
        grid=(m // tm,),
        in_specs=[pl.BlockSpec((tm, D_MODEL), row), pl.BlockSpec((tm, D_MODEL), row),
                  pl.BlockSpec((tm, D_MODEL), row),
                  pl.BlockSpec((D_MODEL, D_MODEL), fixed),
                  pl.BlockSpec((1, D_MODEL), fixed), pl.BlockSpec((1, D_MODEL), fixed)],
        out_specs=pl.BlockSpec((tm, D_MODEL), row),
        out_shape=jax.ShapeDtypeStruct((m, D_MODEL), F32),
        compiler_params=_cparams(("parallel",)),
        name="odd_out_ln",
    )(y2d, gate2d, x2d, w_out_bf16, ln_g.reshape(1, -1), ln_b.reshape(1, -1))


def _ffn_kernel(x_ref, halo_ref, wu_ref, cw_ref, cb_ref, wd_ref, g_ref, beta_ref, o_ref, acc_ref, *, tm):
    i = pl.program_id(1)
    x = x_ref[0]
    halo = jnp.where(i > 0, halo_ref[0], 0.0)
    xe = jnp.concatenate([halo, x], axis=0).astype(BF16)
    n_chunks = FFN_HIDDEN // FFN_CHUNK
    for c in range(n_chunks):
        lo = c * FFN_CHUNK
        ue = _dot(xe, wu_ref[:, lo:lo + FFN_CHUNK])
        gate = _dot(xe[CONV_HALO:], wu_ref[:, FFN_HIDDEN + lo:FFN_HIDDEN + lo + FFN_CHUNK])
        cw = cw_ref[:, lo:lo + FFN_CHUNK]
        conv = (cb_ref[:, lo:lo + FFN_CHUNK]
                + ue[CONV_HALO - 2:CONV_HALO - 2 + tm] * cw[0:1]
                + ue[CONV_HALO - 1:CONV_HALO - 1 + tm] * cw[1:2]
                + ue[CONV_HALO:] * cw[2:3])
        act = 0.5 * conv * (1.0 + lax.erf(conv * (2.0 ** -0.5)))
        part = _dot((act * gate).astype(BF16), wd_ref[lo:lo + FFN_CHUNK, :])
        if c == 0:
            acc_ref[...] = part
        else:
            acc_ref[...] += part
    o_ref[0] = _layer_norm(DN_ALPHA * x + acc_ref[...], g_ref[...], beta_ref[...])


def _ffn(x, w_up_bf16, conv_w, conv_b, w_down_bf16, ln_g, ln_b, tm=512):
    b, s, _ = x.shape
    kern = functools.partial(_ffn_kernel, tm=tm)
    fixed = lambda bi, i: (0, 0)
    halo_blocks = tm // CONV_HALO
    single = pl.Buffered(1)
    return pl.pallas_call(
        kern,
        grid=(b, s // tm),
        in_specs=[pl.BlockSpec((1, tm, D_MODEL), lambda bi, i: (bi, i, 0)),
                  pl.BlockSpec((1, CONV_HALO, D_MODEL),
                               lambda bi, i: (bi, jnp.maximum(i * halo_blocks - 1, 0), 0)),
                  pl.BlockSpec((D_MODEL, 2 * FFN_HIDDEN), fixed, pipeline_mode=single),
                  pl.BlockSpec((3, FFN_HIDDEN), fixed),
                  pl.BlockSpec((1, FFN_HIDDEN), fixed),
                  pl.BlockSpec((FFN_HIDDEN, D_MODEL), fixed, pipeline_mode=single),
                  pl.BlockSpec((1, D_MODEL), fixed), pl.BlockSpec((1, D_MODEL), fixed)],
        out_specs=pl.BlockSpec((1, tm, D_MODEL), lambda bi, i: (bi, i, 0)),
        out_shape=jax.ShapeDtypeStruct((b, s, D_MODEL), F32),
        scratch_shapes=[pltpu.VMEM((tm, D_MODEL), F32)],
        compiler_params=_cparams(("parallel", "arbitrary")),
        name="conv_glu_ffn_ln",
    )(x, x, w_up_bf16, conv_w, conv_b.reshape(1, -1), w_down_bf16, ln_g.reshape(1, -1), ln_b.reshape(1, -1))


def _rwkv_proj_kernel(x_ref, halo_ref, mu_ref, wrkv_ref, w0_ref, w1_ref, w2_ref, a0_ref, a1_ref, a2_ref,
                      g1_ref, g2_ref, r_ref, k_ref, v_ref, wl_ref, a_ref, g_ref, *, tm):
    i = pl.program_id(1)
    x = x_ref[0]
    prev_row = jnp.where(i > 0, halo_ref[0, 7:8, :], 0.0)
    row = lax.broadcasted_iota(jnp.int32, (tm, 1), 0)
    x_prev = jnp.where(row == 0, prev_row, pltpu.roll(x, 1, 0))
    xx = x_prev - x

    def mix(n):
        return (x + xx * mu_ref[n:n + 1, :]).astype(BF16)

    r_ref[0] = _dot(mix(0), wrkv_ref[0])
    k_ref[0] = _dot(mix(1), wrkv_ref[1])
    v_ref[0] = _dot(mix(2), wrkv_ref[2])
    lw = jnp.tanh(_dot(mix(3), w1_ref[...])).astype(BF16)
    z = -(w0_ref[...] + _dot(lw, w2_ref[...]))
    softplus = jnp.maximum(z, 0.0) + jnp.log1p(jnp.exp(-jnp.abs(z)))
    wl_ref[0] = -jnp.exp(-softplus - 0.5)
    la = _dot(mix(4), a1_ref[...]).astype(BF16)
    a_ref[0] = jax.nn.sigmoid(a0_ref[...] + _dot(la, a2_ref[...]))
    lg = jax.nn.sigmoid(_dot(mix(5), g1_ref[...])).astype(BF16)
    g_ref[0] = _dot(lg, g2_ref[...])


def _rwkv_proj(x, mu, w_rkv, w0, w1, w2, a0, a1, a2, g1, g2, tm=512):
    b, s, d = x.shape
    kern = functools.partial(_rwkv_proj_kernel, tm=tm)
    fixed2 = lambda bi, i: (0, 0)
    tile = pl.BlockSpec((1, tm, d), lambda bi, i: (bi, i, 0))
    lora = w1.shape[1]
    out = jax.ShapeDtypeStruct((b, s, d), F32)
    return pl.pallas_call(
        kern,
        grid=(b, s // tm),
        in_specs=[tile,
                  pl.BlockSpec((1, 8, d), lambda bi, i: (bi, jnp.maximum(i * (tm // 8) - 1, 0), 0)),
                  pl.BlockSpec((6, d), fixed2),
                  pl.BlockSpec((3, d, d), lambda bi, i: (0, 0, 0)),
                  pl.BlockSpec((1, d), fixed2), pl.BlockSpec((d, lora), fixed2), pl.BlockSpec((lora, d), fixed2),
                  pl.BlockSpec((1, d), fixed2), pl.BlockSpec((d, lora), fixed2), pl.BlockSpec((lora, d), fixed2),
                  pl.BlockSpec((d, GATE_LORA_PAD), fixed2), pl.BlockSpec((GATE_LORA_PAD, d), fixed2)],
        out_specs=[tile] * 6,
        out_shape=[out] * 6,
        compiler_params=_cparams(("parallel", "arbitrary")),
        name="rwkv_proj",
    )(x, x, mu, w_rkv, w0.reshape(1, -1), w1, w2, a0.reshape(1, -1), a1, a2, g1, g2)


def _each(f, *lists):
    return [f(*xs) for xs in zip(*lists)]


def _wkv_prepare(r, k, v, wl, lr, kk_scale, ka, rk, masks):
    tri_incl_bf16, strict, incl, eye = masks
    T = WKV_CHUNK
    bf = lambda t: t.astype(BF16)
    kk = _each(lambda k_, s_: k_ * s_, k, kk_scale)
    kk = _each(lambda t: t / jnp.maximum(jnp.sqrt(jnp.sum(t * t, axis=-1, keepdims=True)), 1e-12), kk)
    k = _each(lambda k_, lr_, ka_: k_ * (1.0 + (lr_ - 1.0) * ka_), k, lr, ka)
    b_vec = _each(lambda kk_, lr_: kk_ * lr_, kk, lr)

    w_hi = _each(bf, wl)
    w_r1 = _each(lambda w_, h_: w_ - h_.astype(F32), wl, w_hi)
    w_mid = _each(bf, w_r1)
    w_lo = _each(lambda r1, m_: (r1 - m_.astype(F32)).astype(BF16), w_r1, w_mid)
    cl = _each(lambda h_, m_, l_: _dot(tri_incl_bf16, h_) + (_dot(tri_incl_bf16, m_) + _dot(tri_incl_bf16, l_)),
               w_hi, w_mid, w_lo)
    cl_last = _each(lambda c_: c_[T - 1:T, :], cl)
    g_inv = _each(lambda c_: jnp.exp(-c_), cl)
    a_hat = _each(lambda kk_, c_, w_: -kk_ * jnp.exp(c_ - w_), kk, cl, wl)
    b_hat = _each(lambda b_, g_: bf(b_ * g_), b_vec, g_inv)
    k_hat = _each(lambda k_, g_: bf(k_ * g_), k, g_inv)
    r_hat = _each(lambda r_, c_: r_ * jnp.exp(c_), r, cl)
    to_end = _each(lambda l_, c_: jnp.exp(l_ - c_), cl_last, cl)
    b_end = _each(lambda b_, e_: bf(b_ * e_), b_vec, to_end)
    k_end = _each(lambda k_, e_: bf(k_ * e_), k, to_end)

    ar = _each(lambda a_, r_: bf(jnp.concatenate([a_, r_], axis=0)), a_hat, r_hat)
    m_b = _each(_dot_nt, ar, b_hat)
    m_k = _each(_dot_nt, ar, k_hat)
    a_ab = _each(lambda m_: jnp.where(strict, m_[:T], 0.0), m_b)
    a_rb = _each(lambda m_: bf(jnp.where(incl, m_[T:], 0.0)), m_b)
    a_ak = _each(lambda m_: bf(jnp.where(strict, m_[:T], 0.0)), m_k)
    a_rk = _each(lambda m_: bf(jnp.where(incl, m_[T:], 0.0)), m_k)

    ri = lax.broadcasted_iota(jnp.int32, (T, T), 0)
    ci = lax.broadcasted_iota(jnp.int32, (T, T), 1)
    in_base = (ri // WKV_INV_BASE) == (ci // WKV_INV_BASE)
    p = _each(lambda a_: jnp.where(in_base, a_, 0.0), a_ab)
    minv = _each(lambda p_: jnp.where(eye, 1.0, 0.0) + p_, p)
    for _ in range(int(math.log2(WKV_INV_BASE)) - 1):
        p = _each(lambda p_: _dot(bf(p_), bf(p_)), p)
        minv = _each(lambda m_, p_: m_ + _dot(bf(p_), bf(m_)), minv, p)
    size = 2 * WKV_INV_BASE
    while size <= T:
        off = ((ri // size) == (ci // size)) & ((ri // (size // 2)) != (ci // (size // 2)))
        a_off = _each(lambda a_: bf(jnp.where(off, a_, 0.0)), a_ab)
        minv = _each(lambda m_, a_: m_ + _dot(bf(m_), bf(_dot(a_, bf(m_)))), minv, a_off)
        size *= 2

    v_bf16 = _each(bf, v)
    minv_bf16 = _each(bf, minv)
    a_t = _each(lambda m_, a_: bf(_dot(m_, a_[:T])), minv_bf16, ar)
    akv = _each(lambda a_, v_: bf(_dot(a_, v_)), a_ak, v_bf16)
    w_bf16 = _each(lambda m_, x_: bf(_dot(m_, x_)), minv_bf16, akv)
    r_t = _each(lambda r_, a_, t_: bf(r_ + _dot(a_, t_)), r_hat, a_rb, a_t)
    y0 = _each(lambda arb_, w_, ark_, v_: _dot(arb_, w_) + _dot(ark_, v_), a_rb, w_bf16, a_rk, v_bf16)
    trans = _each(lambda l_, t_, b_: bf(jnp.where(eye, jnp.exp(l_), 0.0) + _dot_tn(t_, b_)),
                  cl_last, a_t, b_end)
    s_add = _each(lambda w_, b_, v_, k_: _dot_tn(w_, b_) + _dot_tn(v_, k_),
                  w_bf16, b_end, v_bf16, k_end)
    bonus = _each(lambda r_, k_, rk_, v_: jnp.sum(r_ * k_ * rk_, axis=-1, keepdims=True) * v_, r, k, rk, v)
    return trans, s_add, r_t, y0, bonus


def _wkv_kernel(r_ref, k_ref, v_ref, wl_ref, a_ref, kk_ref, ka_ref, rk_ref, lg_ref, lb_ref, o_ref,
                state_ref, trans_ref, sadd_ref, rt_ref, y0_ref, bonus_ref, *, heads, unroll):
    @pl.when(pl.program_id(2) == 0)
    def _():
        state_ref[...] = jnp.zeros_like(state_ref)

    T = WKV_CHUNK
    N = RWKV_HEAD_DIM
    n_chunks = WKV_TIME_BLOCK // T
    ri = lax.broadcasted_iota(jnp.int32, (T, T), 0)
    ci = lax.broadcasted_iota(jnp.int32, (T, T), 1)
    incl = ci <= ri
    strict = ci < ri
    eye = ci == ri
    tri_incl_bf16 = jnp.where(incl, 1.0, 0.0).astype(BF16)
    masks = (tri_incl_bf16, strict, incl, eye)

    head_cols = [slice(hh * N, (hh + 1) * N) for hh in range(heads)]

    def prepare(group):
        problems = [(group * unroll + cc, hh) for cc in range(unroll) for hh in range(heads)]

        def tiles(ref):
            return [ref[0, pl.ds(pl.multiple_of(c * T, T), T), head_cols[hh]] for c, hh in problems]

        def vecs(ref):
            return [ref[:, head_cols[hh]] for _, hh in problems]

        outs = _wkv_prepare(tiles(r_ref), tiles(k_ref), tiles(v_ref), tiles(wl_ref), tiles(a_ref),
                            vecs(kk_ref), vecs(ka_ref), vecs(rk_ref), masks)
        for ref, vals in zip((trans_ref, sadd_ref, rt_ref, y0_ref, bonus_ref), outs):
            for (c, hh), val in zip(problems, vals):
                ref[c, hh] = val

    def scan(group):
        for cc in range(unroll):
            c = group * unroll + cc
            state = [state_ref[hh] for hh in range(heads)]
            state_bf16 = _each(lambda s_: s_.astype(BF16), state)
            y = [_dot_nt(rt_ref[c, hh], state_bf16[hh]) + y0_ref[c, hh] for hh in range(heads)]
            new_state = [_dot(state_bf16[hh], trans_ref[c, hh]) + sadd_ref[c, hh] for hh in range(heads)]
            for hh in range(heads):
                state_ref[hh] = new_state[hh]
            mu = _each(lambda y_: jnp.mean(y_, axis=-1, keepdims=True), y)
            yc = _each(lambda y_, m_: y_ - m_, y, mu)
            var = _each(lambda c_: jnp.mean(c_ * c_, axis=-1, keepdims=True), yc)
            out = [yc[hh] * lax.rsqrt(var[hh] + RWKV_GN_EPS) * lg_ref[:, head_cols[hh]] + lb_ref[:, head_cols[hh]]
                   + bonus_ref[c, hh] for hh in range(heads)]
            o_ref[0, pl.ds(pl.multiple_of(c * T, T), T), :] = jnp.concatenate(out, axis=-1)

    n_groups = n_chunks // unroll
    prepare(0)

    def pipelined(group, _):
        scan(group - 1)
        prepare(group)
        return 0

    lax.fori_loop(1, n_groups, pipelined, 0)
    scan(n_groups - 1)


def _wkv(r, k, v, wl, a, k_k, k_a, r_k, lnx_g, lnx_b, heads=4, unroll=4):
    b, s, d = r.shape
    tb = WKV_TIME_BLOCK
    width = heads * RWKV_HEAD_DIM
    n_chunks = tb // WKV_CHUNK
    tile = pl.BlockSpec((1, tb, width), lambda bi, h, t: (bi, t, h))
    vec = pl.BlockSpec((1, width), lambda bi, h, t: (0, h))
    per_chunk = lambda dt: pltpu.VMEM((n_chunks, heads, WKV_CHUNK, RWKV_HEAD_DIM), dt)
    kern = functools.partial(_wkv_kernel, heads=heads, unroll=unroll)
    return pl.pallas_call(
        kern,
        grid=(b, d // width, s // tb),
        in_specs=[tile] * 5 + [vec] * 5,
        out_specs=tile,
        out_shape=jax.ShapeDtypeStruct((b, s, d), F32),
        scratch_shapes=[pltpu.VMEM((heads, RWKV_HEAD_DIM, RWKV_HEAD_DIM), F32),
                        per_chunk(BF16), per_chunk(F32), per_chunk(BF16), per_chunk(F32), per_chunk(F32)],
        compiler_params=_cparams(("parallel", "parallel", "arbitrary")),
        name="wkv7",
    )(r, k, v, wl, a, *(t.reshape(1, -1) for t in (k_k, k_a, r_k, lnx_g, lnx_b)))


def kernel(x, ev_w_in, ev_lambda, ev_subln_g, ev_w_out, od_mu, od_w_rkv, od_w0, od_w1, od_w2, od_a0, od_a1, od_a2, od_g1, od_g2, od_k_k, od_k_a, od_r_k, od_lnx_g, od_lnx_b, od_w_out, ln_mix_g, ln_mix_b, ffn_w_up, ffn_conv_w, ffn_conv_b, ffn_w_down, ln_ffn_g, ln_ffn_b):
    b, s, d = x.shape
    m = b * s
    bf = lambda t: t.astype(BF16)

    qkv, gate = _in_proj(x.reshape(m, d), bf(ev_w_in[0]))
    qkv = qkv.reshape(b, s, EVEN_QKV_WIDTH)
    lambda_init = 0.8 - 0.6 * math.exp(-0.3 * 0)
    a_out = _diff_attention(qkv, ev_lambda[0], ev_subln_g[0], lambda_init)
    b_out = _retention(qkv, gate.reshape(b, s, RET_GATE_WIDTH))
    x1 = _even_out(a_out.reshape(m, -1), b_out.reshape(m, -1), x.reshape(m, d), bf(ev_w_out[0]),
                   ln_mix_g[0], ln_mix_b[0]).reshape(b, s, d)
    x2 = _ffn(x1, bf(ffn_w_up[0]), ffn_conv_w[0], ffn_conv_b[0], bf(ffn_w_down[0]), ln_ffn_g[0], ln_ffn_b[0])

    pad = GATE_LORA_PAD - GATE_LORA
    g1 = jnp.pad(bf(od_g1[0]), ((0, 0), (0, pad)))
    g2 = jnp.pad(bf(od_g2[0]), ((0, pad), (0, 0)))
    r, k, v, wl, a, g = _rwkv_proj(x2, od_mu[0], bf(od_w_rkv[0]), od_w0[0], bf(od_w1[0]), bf(od_w2[0]),
                                   od_a0[0], bf(od_a1[0]), bf(od_a2[0]), g1, g2)
    y = _wkv(r, k, v, wl, a, od_k_k[0], od_k_a[0], od_r_k[0], od_lnx_g[0], od_lnx_b[0])
    x3 = _odd_out(y.reshape(m, d), g.reshape(m, d), x2.reshape(m, d), bf(od_w_out[0]),
                  ln_mix_g[1], ln_mix_b[1]).reshape(b, s, d)
    x4 = _ffn(x3, bf(ffn_w_up[1]), ffn_conv_w[1], ffn_conv_b[1], bf(ffn_w_down[1]), ln_ffn_g[1], ln_ffn_b[1])
    return x4
```

```python
import functools
import math

import jax
import jax.numpy as jnp
from jax import lax
from jax.experimental import pallas as pl
from jax.experimental.pallas import tpu as pltpu

F32 = jnp.float32
BF16 = jnp.bfloat16

D_MODEL = 1024
DEPTH = 2
DN_ALPHA = (2.0 * DEPTH) ** 0.25
LN_EPS = 1e-5

DIFF_HEAD_DIM = 64
DIFF_HEADS = 4
RET_QK_DIM = 64
RET_V_DIM = 128
RET_HEADS = 4
RET_CHUNK = 128
EVEN_IN_WIDTH = 3072
EVEN_QKV_WIDTH = 2560
RET_GATE_WIDTH = 512

RWKV_HEAD_DIM = 64
RWKV_GN_EPS = 64e-5
GATE_LORA = 160
GATE_LORA_PAD = 256
WKV_CHUNK = 64
WKV_INV_BASE = 8
WKV_TIME_BLOCK = 1024

FFN_HIDDEN = 2816
FFN_CHUNK = 256
CONV_HALO = 16

LANE = 128
VMEM_LIMIT = 56 * 1024 * 1024


def _cparams(sem):
    return pltpu.CompilerParams(dimension_semantics=sem, vmem_limit_bytes=VMEM_LIMIT)


def _dot(a, b):
    return jnp.dot(a, b, preferred_element_type=F32)


def _dot_nt(a, b):
    return lax.dot_general(a, b, (((1,), (1,)), ((), ())), preferred_element_type=F32)


def _dot_tn(a, b):
    return lax.dot_general(a, b, (((0,), (0,)), ((), ())), preferred_element_type=F32)


def _split2(x):
    hi = x.astype(BF16)
    mid = (x - hi.astype(F32)).astype(BF16)
    return hi, mid


def _dot3(a, b, dot=_dot):
    ah, am = _split2(a)
    bh, bm = _split2(b)
    return dot(ah, bh) + (dot(ah, bm) + dot(am, bh))


def _layer_norm(z, g, b):
    mu = jnp.mean(z, axis=-1, keepdims=True)
    zc = z - mu
    var = jnp.mean(zc * zc, axis=-1, keepdims=True)
    return zc * lax.rsqrt(var + LN_EPS) * g + b


def _in_proj_kernel(x_ref, w_ref, qkv_ref, gate_ref):
    h = _dot(x_ref[...].astype(BF16), w_ref[...])
    qkv_ref[...] = h[:, :EVEN_QKV_WIDTH].astype(BF16)
    gate_ref[...] = h[:, EVEN_QKV_WIDTH:]


def _in_proj(x2d, w_bf16, tm=512):
    m = x2d.shape[0]
    return pl.pallas_call(
        _in_proj_kernel,
        grid=(m // tm,),
        in_specs=[pl.BlockSpec((tm, D_MODEL), lambda i: (i, 0)),
                  pl.BlockSpec((D_MODEL, EVEN_IN_WIDTH), lambda i: (0, 0))],
        out_specs=[pl.BlockSpec((tm, EVEN_QKV_WIDTH), lambda i: (i, 0)),
                   pl.BlockSpec((tm, RET_GATE_WIDTH), lambda i: (i, 0))],
        out_shape=[jax.ShapeDtypeStruct((m, EVEN_QKV_WIDTH), BF16),
                   jax.ShapeDtypeStruct((m, RET_GATE_WIDTH), F32)],
        compiler_params=_cparams(("parallel",)),
        name="in_proj",
    )(x2d, w_bf16)


def _diff_attn_kernel(lam_ref, g_ref, q_ref, k_ref, v_ref, o_ref, *, tq, lambda_init):
    h = pl.program_id(1)
    qi = pl.program_id(2)
    tk = tq
    slope = jnp.where(h == 0, 2.0 ** -2, jnp.where(h == 1, 2.0 ** -4,
                      jnp.where(h == 2, 2.0 ** -6, 2.0 ** -8))).astype(F32)
    lp = lam_ref[...]
    lam = (jnp.exp(jnp.sum(lp[0:1] * lp[1:2], axis=-1, keepdims=True))
           - jnp.exp(jnp.sum(lp[2:3] * lp[3:4], axis=-1, keepdims=True)) + lambda_init)

    lane = lax.broadcasted_iota(jnp.int32, (1, LANE), 1)
    q = q_ref[0] * jnp.asarray(DIFF_HEAD_DIM ** -0.5, BF16)
    zero = jnp.zeros_like(q)
    q_pos = jnp.broadcast_to(jnp.where(lane < 2, slope, 0.0).astype(BF16), (tq, LANE))
    q1 = jnp.concatenate([jnp.where(lane < DIFF_HEAD_DIM, q, zero), q_pos], axis=1)
    q2 = jnp.concatenate([jnp.where(lane >= DIFF_HEAD_DIM, q, zero), q_pos], axis=1)
    key_off = lax.broadcasted_iota(jnp.int32, (tk, LANE), 0)
    k_pos = jnp.where(lane == 0, key_off & 255, jnp.where(lane == 1, key_off & ~255, 0)).astype(F32).astype(BF16)

    def scores(j):
        start = pl.multiple_of(j * tk, tk)
        k = jnp.concatenate([k_ref[0, pl.ds(start, tk), :], k_pos], axis=1)
        v = v_ref[0, pl.ds(start, tk), :]
        return _dot_nt(q1, k), _dot_nt(q2, k), v

    def update(carry, s, v, shift):
        m, l, acc = carry
        m_new = jnp.maximum(m, jnp.max(s, axis=-1, keepdims=True) + shift)
        alpha = jnp.exp(m - m_new)
        p = jnp.exp(s - (m_new - shift))
        l = alpha * l + jnp.sum(p, axis=-1, keepdims=True)
        acc = alpha * acc + _dot(p.astype(BF16), v)
        return m_new, l, acc

    def body(j, carry):
        c1, c2 = carry
        s1, s2, v = scores(j)
        shift = ((j - qi) * tk).astype(F32) * slope
        return update(c1, s1, v, shift), update(c2, s2, v, shift)

    init = (jnp.full((tq, 1), -1e30, F32), jnp.zeros((tq, 1), F32), jnp.zeros((tq, LANE), F32))
    c1, c2 = lax.fori_loop(0, qi, body, (init, init))
    s1, s2, v = scores(qi)
    row_i = lax.broadcasted_iota(jnp.int32, (tq, tk), 0)
    col_i = lax.broadcasted_iota(jnp.int32, (tq, tk), 1)
    keep = col_i <= row_i
    c1 = update(c1, jnp.where(keep, s1, -1e30), v, 0.0)
    c2 = update(c2, jnp.where(keep, s2, -1e30), v, 0.0)
    o = c1[2] / c1[1] - lam * (c2[2] / c2[1])
    o = o * lax.rsqrt(jnp.mean(o * o, axis=-1, keepdims=True) + LN_EPS)
    o_ref[0] = (o * g_ref[...] * (1.0 - lambda_init)).astype(o_ref.dtype)


def _diff_attention(qkv, lam_p, subln_g, lambda_init, tq=512):
    b, s, _ = qkv.shape
    kern = functools.partial(_diff_attn_kernel, tq=tq, lambda_init=lambda_init)
    return pl.pallas_call(
        kern,
        grid=(b, DIFF_HEADS, s // tq),
        in_specs=[pl.BlockSpec((4, DIFF_HEAD_DIM), lambda bi, h, i: (0, 0)),
                  pl.BlockSpec((1, LANE), lambda bi, h, i: (0, 0)),
                  pl.BlockSpec((1, tq, LANE), lambda bi, h, i: (bi, i, h)),
                  pl.BlockSpec((1, s, LANE), lambda bi, h, i: (bi, 0, DIFF_HEADS + h)),
                  pl.BlockSpec((1, s, LANE), lambda bi, h, i: (bi, 0, 2 * DIFF_HEADS + h))],
        out_specs=pl.BlockSpec((1, tq, LANE), lambda bi, h, i: (bi, i, h)),
        out_shape=jax.ShapeDtypeStruct((b, s, DIFF_HEADS * LANE), BF16),
        compiler_params=_cparams(("parallel", "parallel", "arbitrary")),
        name="diff_attn",
    )(lam_p, subln_g.reshape(1, LANE), qkv, qkv, qkv)


def _retention_kernel(q_ref, k_ref, v_ref, g_ref, o_ref, state_ref, *, chunks_per_step):
    C = RET_CHUNK

    @pl.when(pl.program_id(1) == 0)
    def _():
        state_ref[...] = jnp.zeros_like(state_ref)

    lane = lax.broadcasted_iota(jnp.int32, (1, LANE), 1)
    row128 = lax.broadcasted_iota(jnp.int32, (LANE, 1), 0)
    ri = lax.broadcasted_iota(jnp.int32, (C, C), 0)
    ci = lax.broadcasted_iota(jnp.int32, (C, C), 1)
    rel = (ri - ci).astype(F32)
    idx = lax.broadcasted_iota(jnp.int32, (C, 1), 0).astype(F32)
    heads = range(RET_HEADS)
    log_gamma = [math.log1p(-(2.0 ** (-5.0 - h))) for h in heads]
    in_head = [(lane >= (h % 2) * RET_QK_DIM) & (lane < (h % 2 + 1) * RET_QK_DIM) for h in heads]
    in_rows = [(row128 >= (h % 2) * RET_QK_DIM) & (row128 < (h % 2 + 1) * RET_QK_DIM) for h in heads]
    decay = [jnp.where(rel >= 0, jnp.exp(lg * jnp.maximum(rel, 0.0)), 0.0) for lg in log_gamma]
    q_decay = [jnp.exp(lg * (idx + 1.0)) for lg in log_gamma]
    k_decay = [jnp.exp(lg * (C - 1.0 - idx)) for lg in log_gamma]
    v_cols = [slice(h * RET_V_DIM, (h + 1) * RET_V_DIM) for h in heads]

    def chunk_body(c, _):
        rows = pl.ds(pl.multiple_of(c * C, C), C)
        q_all = q_ref[0, rows, :].astype(F32)
        k_all = k_ref[0, rows, :].astype(F32) * (RET_QK_DIM ** -0.5)
        q = [jnp.where(in_head[h], q_all[:, (h // 2) * LANE:(h // 2 + 1) * LANE], 0.0) for h in heads]
        k = [k_all[:, (h // 2) * LANE:(h // 2 + 1) * LANE] for h in heads]
        v = [v_ref[0, rows, v_cols[h]] for h in heads]
        sc = [_dot_nt(q[h].astype(BF16), k[h].astype(BF16)) * decay[h] for h in heads]
        inner = [_dot(sc[h].astype(BF16), v[h]) for h in heads]
        state = [state_ref[h] for h in heads]
        cross = [_dot((q[h] * q_decay[h]).astype(BF16), state[h].astype(BF16)) for h in heads]
        upd = [_dot_tn((k[h] * k_decay[h]).astype(BF16), v[h]) for h in heads]
        for h in heads:
            state_ref[h] = state[h] * math.exp(log_gamma[h] * C) + jnp.where(in_rows[h], upd[h], 0.0)
        y = [inner[h] + cross[h] for h in heads]
        mu = [jnp.mean(y_, axis=-1, keepdims=True) for y_ in y]
        yc = [y_ - m_ for y_, m_ in zip(y, mu)]
        var = [jnp.mean(c_ * c_, axis=-1, keepdims=True) for c_ in yc]
        for h in heads:
            g = g_ref[0, rows, v_cols[h]]
            o_ref[0, rows, v_cols[h]] = (g * jax.nn.sigmoid(g)
                                         * (yc[h] * lax.rsqrt(var[h] + LN_EPS))).astype(o_ref.dtype)
        return 0

    lax.fori_loop(0, chunks_per_step, chunk_body, 0)


def _retention(qkv, gate, chunks_per_step=4):
    b, s, _ = qkv.shape
    C = RET_CHUNK * chunks_per_step
    qk_w = RET_HEADS * RET_QK_DIM
    v_w = RET_HEADS * RET_V_DIM
    return pl.pallas_call(
        functools.partial(_retention_kernel, chunks_per_step=chunks_per_step),
        grid=(b, s // C),
        in_specs=[pl.BlockSpec((1, C, qk_w), lambda bi, c: (bi, c, 1536 // qk_w)),
                  pl.BlockSpec((1, C, qk_w), lambda bi, c: (bi, c, 1792 // qk_w)),
                  pl.BlockSpec((1, C, v_w), lambda bi, c: (bi, c, 2048 // v_w)),
                  pl.BlockSpec((1, C, v_w), lambda bi, c: (bi, c, 0))],
        out_specs=pl.BlockSpec((1, C, v_w), lambda bi, c: (bi, c, 0)),
        out_shape=jax.ShapeDtypeStruct((b, s, v_w), BF16),
        scratch_shapes=[pltpu.VMEM((RET_HEADS, LANE, RET_V_DIM), F32)],
        compiler_params=_cparams(("parallel", "arbitrary")),
        name="retention",
    )(qkv, qkv, qkv, gate)


def _even_out_kernel(a_ref, b_ref, x_ref, wa_ref, wb_ref, g_ref, beta_ref, o_ref):
    mix = _dot(a_ref[...], wa_ref[...]) + _dot(b_ref[...], wb_ref[...])
    o_ref[...] = _layer_norm(DN_ALPHA * x_ref[...] + mix, g_ref[...], beta_ref[...])


def _even_out(a2d, b2d, x2d, w_out_bf16, ln_g, ln_b, tm=512):
    m = x2d.shape[0]
    half = D_MODEL // 2
    row = lambda i: (i, 0)
    fixed = lambda i: (0, 0)
    return pl.pallas_call(
        _even_out_kernel,
        grid=(m // tm,),
        in_specs=[pl.BlockSpec((tm, half), row), pl.BlockSpec((tm, half), row),
                  pl.BlockSpec((tm, D_MODEL), row),
                  pl.BlockSpec((half, D_MODEL), fixed), pl.BlockSpec((half, D_MODEL), lambda i: (1, 0)),
                  pl.BlockSpec((1, D_MODEL), fixed), pl.BlockSpec((1, D_MODEL), fixed)],
        out_specs=pl.BlockSpec((tm, D_MODEL), row),
        out_shape=jax.ShapeDtypeStruct((m, D_MODEL), F32),
        compiler_params=_cparams(("parallel",)),
        name="even_out_ln",
    )(a2d, b2d, x2d, w_out_bf16, w_out_bf16, ln_g.reshape(1, -1), ln_b.reshape(1, -1))


def _odd_out_kernel(y_ref, x_ref, w_ref, g_ref, beta_ref, o_ref):
    mix = _dot(y_ref[...], w_ref[...])
    o_ref[...] = _layer_norm(DN_ALPHA * x_ref[...] + mix, g_ref[...], beta_ref[...])


def _odd_out(y2d, x2d, w_out_bf16, ln_g, ln_b, tm=512):
    m = x2d.shape[0]
    row = lambda i: (i, 0)
    fixed = lambda i: (0, 0)
    return pl.pallas_call(
        _odd_out_kernel,
        grid=(m // tm,),
        in_specs=[pl.BlockSpec((tm, D_MODEL), row),
                  pl.BlockSpec((tm, D_MODEL), row),
                  pl.BlockSpec((D_MODEL, D_MODEL), fixed),
                  pl.BlockSpec((1, D_MODEL), fixed), pl.BlockSpec((1, D_MODEL), fixed)],
        out_specs=pl.BlockSpec((tm, D_MODEL), row),
        out_shape=jax.ShapeDtypeStruct((m, D_MODEL), F32),
        compiler_params=_cparams(("parallel",)),
        name="odd_out_ln",
    )(y2d, x2d, w_out_bf16, ln_g.reshape(1, -1), ln_b.reshape(1, -1))


def _ffn_kernel(x_ref, halo_ref, wu_ref, cw_ref, cb_ref, wd_ref, g_ref, beta_ref, o_ref, act_ref, *, tm):
    i = pl.program_id(1)
    x = x_ref[0]
    halo = jnp.where(i > 0, halo_ref[0], 0.0)
    xe = jnp.concatenate([halo, x], axis=0).astype(BF16)
    n_chunks = FFN_HIDDEN // FFN_CHUNK
    for c in range(n_chunks):
        lo = c * FFN_CHUNK
        ue = _dot(xe, wu_ref[:, lo:lo + FFN_CHUNK])
        gate = _dot(xe[CONV_HALO:], wu_ref[:, FFN_HIDDEN + lo:FFN_HIDDEN + lo + FFN_CHUNK])
        cw = cw_ref[:, lo:lo + FFN_CHUNK]
        conv = (cb_ref[:, lo:lo + FFN_CHUNK]
                + ue[CONV_HALO - 2:CONV_HALO - 2 + tm] * cw[0:1]
                + ue[CONV_HALO - 1:CONV_HALO - 1 + tm] * cw[1:2]
                + ue[CONV_HALO:] * cw[2:3])
        act = 0.5 * conv * (1.0 + lax.erf(conv * (2.0 ** -0.5)))
        act_ref[:, lo:lo + FFN_CHUNK] = (act * gate).astype(BF16)
    ffn = _dot(act_ref[...], wd_ref[...])
    o_ref[0] = _layer_norm(DN_ALPHA * x + ffn, g_ref[...], beta_ref[...])


def _ffn(x, w_up_bf16, conv_w, conv_b, w_down_bf16, ln_g, ln_b, tm=512):
    b, s, _ = x.shape
    kern = functools.partial(_ffn_kernel, tm=tm)
    fixed = lambda bi, i: (0, 0)
    halo_blocks = tm // CONV_HALO
    single = pl.Buffered(1)
    return pl.pallas_call(
        kern,
        grid=(b, s // tm),
        in_specs=[pl.BlockSpec((1, tm, D_MODEL), lambda bi, i: (bi, i, 0)),
                  pl.BlockSpec((1, CONV_HALO, D_MODEL),
                               lambda bi, i: (bi, jnp.maximum(i * halo_blocks - 1, 0), 0)),
                  pl.BlockSpec((D_MODEL, 2 * FFN_HIDDEN), fixed, pipeline_mode=single),
                  pl.BlockSpec((3, FFN_HIDDEN), fixed),
                  pl.BlockSpec((1, FFN_HIDDEN), fixed),
                  pl.BlockSpec((FFN_HIDDEN, D_MODEL), fixed, pipeline_mode=single),
                  pl.BlockSpec((1, D_MODEL), fixed), pl.BlockSpec((1, D_MODEL), fixed)],
        out_specs=pl.BlockSpec((1, tm, D_MODEL), lambda bi, i: (bi, i, 0)),
        out_shape=jax.ShapeDtypeStruct((b, s, D_MODEL), F32),
        scratch_shapes=[pltpu.VMEM((tm, FFN_HIDDEN), BF16)],
        compiler_params=_cparams(("parallel", "arbitrary")),
        name="conv_glu_ffn_ln",
    )(x, x, w_up_bf16, conv_w, conv_b.reshape(1, -1), w_down_bf16, ln_g.reshape(1, -1), ln_b.reshape(1, -1))


def _rwkv_proj_kernel(x_ref, halo_ref, mu_ref, wrkv_ref, w0_ref, w1_ref, w2_ref, a0_ref, a1_ref, a2_ref,
                      g1_ref, g2_ref, r_ref, k_ref, v_ref, wl_ref, a_ref, g_ref, *, tm):
    i = pl.program_id(1)
    x = x_ref[0]
    prev_row = jnp.where(i > 0, halo_ref[0, 7:8, :], 0.0)
    row = lax.broadcasted_iota(jnp.int32, (tm, 1), 0)
    x_prev = jnp.where(row == 0, prev_row, pltpu.roll(x, 1, 0))
    xx = x_prev - x

    def mix(n):
        return (x + xx * mu_ref[n:n + 1, :]).astype(BF16)

    r_ref[0] = _dot(mix(0), wrkv_ref[0]).astype(r_ref.dtype)
    k_ref[0] = _dot(mix(1), wrkv_ref[1]).astype(k_ref.dtype)
    v_ref[0] = _dot(mix(2), wrkv_ref[2]).astype(v_ref.dtype)
    lw = jnp.tanh(_dot(mix(3), w1_ref[...])).astype(BF16)
    u = w0_ref[...] + _dot(lw, w2_ref[...])
    wl_ref[0] = -math.exp(-0.5) * jax.nn.sigmoid(u)
    la = _dot(mix(4), a1_ref[...]).astype(BF16)
    a_ref[0] = jax.nn.sigmoid(a0_ref[...] + _dot(la, a2_ref[...])).astype(a_ref.dtype)
    lg = jax.nn.sigmoid(_dot(mix(5), g1_ref[...])).astype(BF16)
    g_ref[0] = _dot(lg, g2_ref[...]).astype(g_ref.dtype)


def _rwkv_proj(x, mu, w_rkv, w0, w1, w2, a0, a1, a2, g1, g2, tm=512):
    b, s, d = x.shape
    kern = functools.partial(_rwkv_proj_kernel, tm=tm)
    fixed2 = lambda bi, i: (0, 0)
    tile = pl.BlockSpec((1, tm, d), lambda bi, i: (bi, i, 0))
    lora = w1.shape[1]
    out = jax.ShapeDtypeStruct((b, s, d), F32)
    half = jax.ShapeDtypeStruct((b, s, d), BF16)
    return pl.pallas_call(
        kern,
        grid=(b, s // tm),
        in_specs=[tile,
                  pl.BlockSpec((1, 8, d), lambda bi, i: (bi, jnp.maximum(i * (tm // 8) - 1, 0), 0)),
                  pl.BlockSpec((6, d), fixed2),
                  pl.BlockSpec((3, d, d), lambda bi, i: (0, 0, 0)),
                  pl.BlockSpec((1, d), fixed2), pl.BlockSpec((d, lora), fixed2), pl.BlockSpec((lora, d), fixed2),
                  pl.BlockSpec((1, d), fixed2), pl.BlockSpec((d, lora), fixed2), pl.BlockSpec((lora, d), fixed2),
                  pl.BlockSpec((d, GATE_LORA_PAD), fixed2), pl.BlockSpec((GATE_LORA_PAD, d), fixed2)],
        out_specs=[tile] * 6,
        out_shape=[half, half, half, out, half, half],
        compiler_params=_cparams(("parallel", "arbitrary")),
        name="rwkv_proj",
    )(x, x, mu, w_rkv, w0.reshape(1, -1), w1, w2, a0.reshape(1, -1), a1, a2, g1, g2)


def _each(f, *lists):
    return [f(*xs) for xs in zip(*lists)]


def _wkv_prepare(r, k, v, wl, lr, kk_scale, ka, rk, masks):
    tri_incl_bf16, strict, incl, eye = masks
    T = WKV_CHUNK
    bf = lambda t: t.astype(BF16)
    kk = _each(lambda k_, s_: k_ * s_, k, kk_scale)
    kk = _each(lambda t: t / jnp.maximum(jnp.sqrt(jnp.sum(t * t, axis=-1, keepdims=True)), 1e-12), kk)
    k = _each(lambda k_, lr_, ka_: k_ * (1.0 + (lr_ - 1.0) * ka_), k, lr, ka)
    b_vec = _each(lambda kk_, lr_: kk_ * lr_, kk, lr)

    w_hi = _each(bf, wl)
    w_r1 = _each(lambda w_, h_: w_ - h_.astype(F32), wl, w_hi)
    w_mid = _each(bf, w_r1)
    w_lo = _each(lambda r1, m_: (r1 - m_.astype(F32)).astype(BF16), w_r1, w_mid)
    cl = _each(lambda h_, m_, l_: _dot(tri_incl_bf16, h_) + (_dot(tri_incl_bf16, m_) + _dot(tri_incl_bf16, l_)),
               w_hi, w_mid, w_lo)
    cl_last = _each(lambda c_: c_[T - 1:T, :], cl)
    g_inv = _each(lambda c_: jnp.exp(-c_), cl)
    a_hat = _each(lambda kk_, c_, w_: -kk_ * jnp.exp(c_ - w_), kk, cl, wl)
    b_hat = _each(lambda b_, g_: bf(b_ * g_), b_vec, g_inv)
    k_hat = _each(lambda k_, g_: bf(k_ * g_), k, g_inv)
    r_hat = _each(lambda r_, c_: r_ * jnp.exp(c_), r, cl)
    to_end = _each(lambda l_, c_: jnp.exp(l_ - c_), cl_last, cl)
    b_end = _each(lambda b_, e_: bf(b_ * e_), b_vec, to_end)
    k_end = _each(lambda k_, e_: bf(k_ * e_), k, to_end)

    ar = _each(lambda a_, r_: bf(jnp.concatenate([a_, r_], axis=0)), a_hat, r_hat)
    m_b = _each(_dot_nt, ar, b_hat)
    m_k = _each(_dot_nt, ar, k_hat)
    a_ab = _each(lambda m_: jnp.where(strict, m_[:T], 0.0), m_b)
    a_rb = _each(lambda m_: bf(jnp.where(incl, m_[T:], 0.0)), m_b)
    a_ak = _each(lambda m_: bf(jnp.where(strict, m_[:T], 0.0)), m_k)
    a_rk = _each(lambda m_: bf(jnp.where(incl, m_[T:], 0.0)), m_k)

    ri = lax.broadcasted_iota(jnp.int32, (T, T), 0)
    ci = lax.broadcasted_iota(jnp.int32, (T, T), 1)
    in_base = (ri // WKV_INV_BASE) == (ci // WKV_INV_BASE)
    p = _each(lambda a_: jnp.where(in_base, a_, 0.0), a_ab)
    minv = _each(lambda p_: jnp.where(eye, 1.0, 0.0) + p_, p)
    for _ in range(int(math.log2(WKV_INV_BASE)) - 1):
        p = _each(lambda p_: _dot(bf(p_), bf(p_)), p)
        minv = _each(lambda m_, p_: m_ + _dot(bf(p_), bf(m_)), minv, p)
    size = 2 * WKV_INV_BASE
    while size <= T:
        off = ((ri // size) == (ci // size)) & ((ri // (size // 2)) != (ci // (size // 2)))
        a_off = _each(lambda a_: bf(jnp.where(off, a_, 0.0)), a_ab)
        minv = _each(lambda m_, a_: m_ + _dot(bf(m_), bf(_dot(a_, bf(m_)))), minv, a_off)
        size *= 2

    v_bf16 = _each(bf, v)
    minv_bf16 = _each(bf, minv)
    a_t = _each(lambda m_, a_: bf(_dot(m_, a_[:T])), minv_bf16, ar)
    akv = _each(lambda a_, v_: bf(_dot(a_, v_)), a_ak, v_bf16)
    w_bf16 = _each(lambda m_, x_: bf(_dot(m_, x_)), minv_bf16, akv)
    r_t = _each(lambda r_, a_, t_: bf(r_ + _dot(a_, t_)), r_hat, a_rb, a_t)
    y0 = _each(lambda arb_, w_, ark_, v_: _dot(arb_, w_) + _dot(ark_, v_), a_rb, w_bf16, a_rk, v_bf16)
    trans = _each(lambda l_, t_, b_: bf(jnp.where(eye, jnp.exp(l_), 0.0) + _dot_tn(t_, b_)),
                  cl_last, a_t, b_end)
    s_add = _each(lambda w_, b_, v_, k_: _dot_tn(w_, b_) + _dot_tn(v_, k_),
                  w_bf16, b_end, v_bf16, k_end)
    bonus = _each(lambda r_, k_, rk_, v_: jnp.sum(r_ * k_ * rk_, axis=-1, keepdims=True) * v_, r, k, rk, v)
    return trans, s_add, r_t, y0, bonus


def _wkv_kernel(r_ref, k_ref, v_ref, wl_ref, a_ref, gate_ref, kk_ref, ka_ref, rk_ref, lg_ref, lb_ref, o_ref,
                state_ref, trans_ref, sadd_ref, rt_ref, y0_ref, bonus_ref, *, heads, unroll):
    @pl.when(pl.program_id(2) == 0)
    def _():
        state_ref[...] = jnp.zeros_like(state_ref)

    T = WKV_CHUNK
    N = RWKV_HEAD_DIM
    n_chunks = WKV_TIME_BLOCK // T
    ri = lax.broadcasted_iota(jnp.int32, (T, T), 0)
    ci = lax.broadcasted_iota(jnp.int32, (T, T), 1)
    incl = ci <= ri
    strict = ci < ri
    eye = ci == ri
    tri_incl_bf16 = jnp.where(incl, 1.0, 0.0).astype(BF16)
    masks = (tri_incl_bf16, strict, incl, eye)

    head_cols = [slice(hh * N, (hh + 1) * N) for hh in range(heads)]

    def prepare(group):
        chunk_ids = [group * unroll + cc for cc in range(unroll)]
        problems = [(cc, hh) for cc in range(unroll) for hh in range(heads)]

        def tiles(ref):
            chunks = [ref[0, pl.ds(pl.multiple_of(c * T, T), T), :].astype(F32) for c in chunk_ids]
            return [chunks[cc][:, head_cols[hh]] for cc, hh in problems]

        def vecs(ref):
            return [ref[:, head_cols[hh]] for _, hh in problems]

        outs = _wkv_prepare(tiles(r_ref), tiles(k_ref), tiles(v_ref), tiles(wl_ref), tiles(a_ref),
                            vecs(kk_ref), vecs(ka_ref), vecs(rk_ref), masks)
        for ref, vals in zip((trans_ref, sadd_ref, rt_ref, y0_ref, bonus_ref), outs):
            for (cc, hh), val in zip(problems, vals):
                ref[chunk_ids[cc], hh] = val

    def scan(group):
        for cc in range(unroll):
            c = group * unroll + cc
            state = [state_ref[hh] for hh in range(heads)]
            state_bf16 = _each(lambda s_: s_.astype(BF16), state)
            y = [_dot_nt(rt_ref[c, hh], state_bf16[hh]) + y0_ref[c, hh] for hh in range(heads)]
            new_state = [_dot(state_bf16[hh], trans_ref[c, hh]) + sadd_ref[c, hh] for hh in range(heads)]
            for hh in range(heads):
                state_ref[hh] = new_state[hh]
            mu = _each(lambda y_: jnp.mean(y_, axis=-1, keepdims=True), y)
            yc = _each(lambda y_, m_: y_ - m_, y, mu)
            var = _each(lambda c_: jnp.mean(c_ * c_, axis=-1, keepdims=True), yc)
            out = [yc[hh] * lax.rsqrt(var[hh] + RWKV_GN_EPS) * lg_ref[:, head_cols[hh]] + lb_ref[:, head_cols[hh]]
                   + bonus_ref[c, hh] for hh in range(heads)]
            rows = pl.ds(pl.multiple_of(c * T, T), T)
            gated = jnp.concatenate(out, axis=-1) * gate_ref[0, rows, :].astype(F32)
            o_ref[0, rows, :] = gated.astype(o_ref.dtype)

    n_groups = n_chunks // unroll
    prepare(0)

    def pipelined(group, _):
        scan(group - 1)
        prepare(group)
        return 0

    lax.fori_loop(1, n_groups, pipelined, 0)
    scan(n_groups - 1)


def _wkv(r, k, v, wl, a, gate, k_k, k_a, r_k, lnx_g, lnx_b, heads=4, unroll=4):
    b, s, d = r.shape
    tb = WKV_TIME_BLOCK
    width = heads * RWKV_HEAD_DIM
    n_chunks = tb // WKV_CHUNK
    tile = pl.BlockSpec((1, tb, width), lambda bi, h, t: (bi, t, h))
    vec = pl.BlockSpec((1, width), lambda bi, h, t: (0, h))
    per_chunk = lambda dt: pltpu.VMEM((n_chunks, heads, WKV_CHUNK, RWKV_HEAD_DIM), dt)
    kern = functools.partial(_wkv_kernel, heads=heads, unroll=unroll)
    return pl.pallas_call(
        kern,
        grid=(b, d // width, s // tb),
        in_specs=[tile] * 6 + [vec] * 5,
        out_specs=tile,
        out_shape=jax.ShapeDtypeStruct((b, s, d), BF16),
        scratch_shapes=[pltpu.VMEM((heads, RWKV_HEAD_DIM, RWKV_HEAD_DIM), F32),
                        per_chunk(BF16), per_chunk(F32), per_chunk(BF16), per_chunk(F32), per_chunk(F32)],
        compiler_params=_cparams(("parallel", "parallel", "arbitrary")),
        name="wkv7",
    )(r, k, v, wl, a, gate, *(t.reshape(1, -1) for t in (k_k, k_a, r_k, lnx_g, lnx_b)))


def kernel(x, ev_w_in, ev_lambda, ev_subln_g, ev_w_out, od_mu, od_w_rkv, od_w0, od_w1, od_w2, od_a0, od_a1, od_a2, od_g1, od_g2, od_k_k, od_k_a, od_r_k, od_lnx_g, od_lnx_b, od_w_out, ln_mix_g, ln_mix_b, ffn_w_up, ffn_conv_w, ffn_conv_b, ffn_w_down, ln_ffn_g, ln_ffn_b):
    b, s, d = x.shape
    m = b * s
    bf = lambda t: t.astype(BF16)

    qkv, gate = _in_proj(x.reshape(m, d), bf(ev_w_in[0]))
    qkv = qkv.reshape(b, s, EVEN_QKV_WIDTH)
    lambda_init = 0.8 - 0.6 * math.exp(-0.3 * 0)
    a_out = _diff_attention(qkv, ev_lambda[0], ev_subln_g[0], lambda_init)
    b_out = _retention(qkv, gate.reshape(b, s, RET_GATE_WIDTH))
    x1 = _even_out(a_out.reshape(m, -1), b_out.reshape(m, -1), x.reshape(m, d), bf(ev_w_out[0]),
                   ln_mix_g[0], ln_mix_b[0]).reshape(b, s, d)
    x2 = _ffn(x1, bf(ffn_w_up[0]), ffn_conv_w[0], ffn_conv_b[0], bf(ffn_w_down[0]), ln_ffn_g[0], ln_ffn_b[0])

    pad = GATE_LORA_PAD - GATE_LORA
    g1 = jnp.pad(bf(od_g1[0]), ((0, 0), (0, pad)))
    g2 = jnp.pad(bf(od_g2[0]), ((0, pad), (0, 0)))
    r, k, v, wl, a, g = _rwkv_proj(x2, od_mu[0], bf(od_w_rkv[0]), od_w0[0], bf(od_w1[0]), bf(od_w2[0]),
                                   od_a0[0], bf(od_a1[0]), bf(od_a2[0]), g1, g2)
    y = _wkv(r, k, v, wl, a, g, od_k_k[0], od_k_a[0], od_r_k[0], od_lnx_g[0], od_lnx_b[0])
    x3 = _odd_out(y.reshape(m, d), x2.reshape(m, d), bf(od_w_out[0]),
                  ln_mix_g[1], ln_mix_b[1]).reshape(b, s, d)
    x4 = _ffn(x3, bf(ffn_w_up[1]), ffn_conv_w[1], ffn_conv_b[1], bf(ffn_w_down[1]), ln_ffn_g[1], ln_ffn_b[1])
    return x4
```

```python
import functools
import math

import jax
import jax.numpy as jnp
from jax import lax
from jax.experimental import pallas as pl
from jax.experimental.pallas import tpu as pltpu

F32 = jnp.float32
BF16 = jnp.bfloat16

D_MODEL = 1024
DEPTH = 2
DN_ALPHA = (2.0 * DEPTH) ** 0.25
LN_EPS = 1e-5

DIFF_HEAD_DIM = 64
DIFF_HEADS = 4
RET_QK_DIM = 64
RET_V_DIM = 128
RET_HEADS = 4
RET_CHUNK = 128
EVEN_IN_WIDTH = 3072
EVEN_QKV_WIDTH = 2560
RET_GATE_WIDTH = 512

RWKV_HEAD_DIM = 64
RWKV_GN_EPS = 64e-5
GATE_LORA = 160
GATE_LORA_PAD = 256
WKV_CHUNK = 64
WKV_INV_BASE = 8
WKV_TIME_BLOCK = 1024

FFN_HIDDEN = 2816
FFN_CHUNK = 256
CONV_HALO = 16

LANE = 128
VMEM_LIMIT = 56 * 1024 * 1024


def _cparams(sem):
    return pltpu.CompilerParams(dimension_semantics=sem, vmem_limit_bytes=VMEM_LIMIT)


def _dot(a, b):
    return jnp.dot(a, b, preferred_element_type=F32)


def _dot_nt(a, b):
    return lax.dot_general(a, b, (((1,), (1,)), ((), ())), preferred_element_type=F32)


def _dot_tn(a, b):
    return lax.dot_general(a, b, (((0,), (0,)), ((), ())), preferred_element_type=F32)


def _layer_norm(z, g, b):
    mu = jnp.mean(z, axis=-1, keepdims=True)
    zc = z - mu
    var = jnp.mean(zc * zc, axis=-1, keepdims=True)
    return zc * lax.rsqrt(var + LN_EPS) * g + b


def _in_proj_kernel(x_ref, w_ref, qkv_ref, gate_ref):
    h = _dot(x_ref[...].astype(BF16), w_ref[...])
    qkv_ref[...] = h[:, :EVEN_QKV_WIDTH].astype(BF16)
    gate_ref[...] = h[:, EVEN_QKV_WIDTH:]


def _in_proj(x2d, w_bf16, tm=512):
    m = x2d.shape[0]
    return pl.pallas_call(
        _in_proj_kernel,
        grid=(m // tm,),
        in_specs=[pl.BlockSpec((tm, D_MODEL), lambda i: (i, 0)),
                  pl.BlockSpec((D_MODEL, EVEN_IN_WIDTH), lambda i: (0, 0))],
        out_specs=[pl.BlockSpec((tm, EVEN_QKV_WIDTH), lambda i: (i, 0)),
                   pl.BlockSpec((tm, RET_GATE_WIDTH), lambda i: (i, 0))],
        out_shape=[jax.ShapeDtypeStruct((m, EVEN_QKV_WIDTH), BF16),
                   jax.ShapeDtypeStruct((m, RET_GATE_WIDTH), F32)],
        compiler_params=_cparams(("parallel",)),
        name="in_proj",
    )(x2d, w_bf16)


def _diff_attn_kernel(lam_ref, g_ref, q_ref, k_ref, v_ref, o_ref, *, tq, lambda_init):
    h = pl.program_id(1)
    qi = pl.program_id(2)
    tk = tq
    slope = jnp.where(h == 0, 2.0 ** -2, jnp.where(h == 1, 2.0 ** -4,
                      jnp.where(h == 2, 2.0 ** -6, 2.0 ** -8))).astype(F32)
    lp = lam_ref[...]
    lam = (jnp.exp(jnp.sum(lp[0:1] * lp[1:2], axis=-1, keepdims=True))
           - jnp.exp(jnp.sum(lp[2:3] * lp[3:4], axis=-1, keepdims=True)) + lambda_init)

    lane = lax.broadcasted_iota(jnp.int32, (1, LANE), 1)
    q = q_ref[0] * jnp.asarray(DIFF_HEAD_DIM ** -0.5, BF16)
    zero = jnp.zeros_like(q)
    q_pos = jnp.broadcast_to(jnp.where(lane < 2, slope, 0.0).astype(BF16), (tq, LANE))
    q1 = jnp.concatenate([jnp.where(lane < DIFF_HEAD_DIM, q, zero), q_pos], axis=1)
    q2 = jnp.concatenate([jnp.where(lane >= DIFF_HEAD_DIM, q, zero), q_pos], axis=1)
    key_off = lax.broadcasted_iota(jnp.int32, (tk, LANE), 0)
    k_pos = jnp.where(lane == 0, key_off & 255, jnp.where(lane == 1, key_off & ~255, 0)).astype(F32).astype(BF16)

    def scores(j):
        start = pl.multiple_of(j * tk, tk)
        k = jnp.concatenate([k_ref[0, pl.ds(start, tk), :], k_pos], axis=1)
        v = v_ref[0, pl.ds(start, tk), :]
        return _dot_nt(q1, k), _dot_nt(q2, k), v

    def update(carry, s, v, shift):
        m, l, acc = carry
        m_new = jnp.maximum(m, jnp.max(s, axis=-1, keepdims=True) + shift)
        alpha = jnp.exp(m - m_new)
        p = jnp.exp(s - (m_new - shift))
        l = alpha * l + jnp.sum(p, axis=-1, keepdims=True)
        acc = alpha * acc + _dot(p.astype(BF16), v)
        return m_new, l, acc

    def body(j, carry):
        c1, c2 = carry
        s1, s2, v = scores(j)
        shift = ((j - qi) * tk).astype(F32) * slope
        return update(c1, s1, v, shift), update(c2, s2, v, shift)

    init = (jnp.full((tq, 1), -1e30, F32), jnp.zeros((tq, 1), F32), jnp.zeros((tq, LANE), F32))
    c1, c2 = lax.fori_loop(0, qi, body, (init, init))
    s1, s2, v = scores(qi)
    row_i = lax.broadcasted_iota(jnp.int32, (tq, tk), 0)
    col_i = lax.broadcasted_iota(jnp.int32, (tq, tk), 1)
    keep = col_i <= row_i
    c1 = update(c1, jnp.where(keep, s1, -1e30), v, 0.0)
    c2 = update(c2, jnp.where(keep, s2, -1e30), v, 0.0)
    o = c1[2] / c1[1] - lam * (c2[2] / c2[1])
    o = o * lax.rsqrt(jnp.mean(o * o, axis=-1, keepdims=True) + LN_EPS)
    o_ref[0] = (o * g_ref[...] * (1.0 - lambda_init)).astype(o_ref.dtype)


def _diff_attention(qkv, lam_p, subln_g, lambda_init, tq=512):
    b, s, _ = qkv.shape
    kern = functools.partial(_diff_attn_kernel, tq=tq, lambda_init=lambda_init)
    return pl.pallas_call(
        kern,
        grid=(b, DIFF_HEADS, s // tq),
        in_specs=[pl.BlockSpec((4, DIFF_HEAD_DIM), lambda bi, h, i: (0, 0)),
                  pl.BlockSpec((1, LANE), lambda bi, h, i: (0, 0)),
                  pl.BlockSpec((1, tq, LANE), lambda bi, h, i: (bi, i, h)),
                  pl.BlockSpec((1, s, LANE), lambda bi, h, i: (bi, 0, DIFF_HEADS + h)),
                  pl.BlockSpec((1, s, LANE), lambda bi, h, i: (bi, 0, 2 * DIFF_HEADS + h))],
        out_specs=pl.BlockSpec((1, tq, LANE), lambda bi, h, i: (bi, i, h)),
        out_shape=jax.ShapeDtypeStruct((b, s, DIFF_HEADS * LANE), BF16),
        compiler_params=_cparams(("parallel", "parallel", "arbitrary")),
        name="diff_attn",
    )(lam_p, subln_g.reshape(1, LANE), qkv, qkv, qkv)


def _retention_kernel(q_ref, k_ref, v_ref, g_ref, o_ref, state_ref, *, chunks_per_step):
    C = RET_CHUNK

    @pl.when(pl.program_id(1) == 0)
    def _():
        state_ref[...] = jnp.zeros_like(state_ref)

    lane = lax.broadcasted_iota(jnp.int32, (1, LANE), 1)
    row128 = lax.broadcasted_iota(jnp.int32, (LANE, 1), 0)
    ri = lax.broadcasted_iota(jnp.int32, (C, C), 0)
    ci = lax.broadcasted_iota(jnp.int32, (C, C), 1)
    rel = (ri - ci).astype(F32)
    idx = lax.broadcasted_iota(jnp.int32, (C, 1), 0).astype(F32)
    heads = range(RET_HEADS)
    log_gamma = [math.log1p(-(2.0 ** (-5.0 - h))) for h in heads]
    in_head = [(lane >= (h % 2) * RET_QK_DIM) & (lane < (h % 2 + 1) * RET_QK_DIM) for h in heads]
    in_rows = [(row128 >= (h % 2) * RET_QK_DIM) & (row128 < (h % 2 + 1) * RET_QK_DIM) for h in heads]
    decay = [jnp.where(rel >= 0, jnp.exp(lg * jnp.maximum(rel, 0.0)), 0.0) for lg in log_gamma]
    q_decay = [jnp.exp(lg * (idx + 1.0)) for lg in log_gamma]
    k_decay = [jnp.exp(lg * (C - 1.0 - idx)) for lg in log_gamma]
    v_cols = [slice(h * RET_V_DIM, (h + 1) * RET_V_DIM) for h in heads]

    def chunk_body(c, _):
        rows = pl.ds(pl.multiple_of(c * C, C), C)
        q_all = q_ref[0, rows, :].astype(F32)
        k_all = k_ref[0, rows, :].astype(F32) * (RET_QK_DIM ** -0.5)
        q = [jnp.where(in_head[h], q_all[:, (h // 2) * LANE:(h // 2 + 1) * LANE], 0.0) for h in heads]
        k = [k_all[:, (h // 2) * LANE:(h // 2 + 1) * LANE] for h in heads]
        v = [v_ref[0, rows, v_cols[h]] for h in heads]
        sc = [_dot_nt(q[h].astype(BF16), k[h].astype(BF16)) * decay[h] for h in heads]
        inner = [_dot(sc[h].astype(BF16), v[h]) for h in heads]
        state = [state_ref[h] for h in heads]
        cross = [_dot((q[h] * q_decay[h]).astype(BF16), state[h].astype(BF16)) for h in heads]
        upd = [_dot_tn((k[h] * k_decay[h]).astype(BF16), v[h]) for h in heads]
        for h in heads:
            state_ref[h] = state[h] * math.exp(log_gamma[h] * C) + jnp.where(in_rows[h], upd[h], 0.0)
        y = [inner[h] + cross[h] for h in heads]
        mu = [jnp.mean(y_, axis=-1, keepdims=True) for y_ in y]
        yc = [y_ - m_ for y_, m_ in zip(y, mu)]
        var = [jnp.mean(c_ * c_, axis=-1, keepdims=True) for c_ in yc]
        for h in heads:
            g = g_ref[0, rows, v_cols[h]]
            o_ref[0, rows, v_cols[h]] = (g * jax.nn.sigmoid(g)
                                         * (yc[h] * lax.rsqrt(var[h] + LN_EPS))).astype(o_ref.dtype)
        return 0

    lax.fori_loop(0, chunks_per_step, chunk_body, 0)


def _retention(qkv, gate, chunks_per_step=4):
    b, s, _ = qkv.shape
    C = RET_CHUNK * chunks_per_step
    qk_w = RET_HEADS * RET_QK_DIM
    v_w = RET_HEADS * RET_V_DIM
    return pl.pallas_call(
        functools.partial(_retention_kernel, chunks_per_step=chunks_per_step),
        grid=(b, s // C),
        in_specs=[pl.BlockSpec((1, C, qk_w), lambda bi, c: (bi, c, 1536 // qk_w)),
                  pl.BlockSpec((1, C, qk_w), lambda bi, c: (bi, c, 1792 // qk_w)),
                  pl.BlockSpec((1, C, v_w), lambda bi, c: (bi, c, 2048 // v_w)),
                  pl.BlockSpec((1, C, v_w), lambda bi, c: (bi, c, 0))],
        out_specs=pl.BlockSpec((1, C, v_w), lambda bi, c: (bi, c, 0)),
        out_shape=jax.ShapeDtypeStruct((b, s, v_w), BF16),
        scratch_shapes=[pltpu.VMEM((RET_HEADS, LANE, RET_V_DIM), F32)],
        compiler_params=_cparams(("parallel", "arbitrary")),
        name="retention",
    )(qkv, qkv, qkv, gate)


def _even_out_kernel(a_ref, b_ref, x_ref, wa_ref, wb_ref, g_ref, beta_ref, o_ref):
    mix = _dot(a_ref[...], wa_ref[...]) + _dot(b_ref[...], wb_ref[...])
    o_ref[...] = _layer_norm(DN_ALPHA * x_ref[...] + mix, g_ref[...], beta_ref[...])


def _even_out(a2d, b2d, x2d, w_out_bf16, ln_g, ln_b, tm=512):
    m = x2d.shape[0]
    half = D_MODEL // 2
    row = lambda i: (i, 0)
    fixed = lambda i: (0, 0)
    return pl.pallas_call(
        _even_out_kernel,
        grid=(m // tm,),
        in_specs=[pl.BlockSpec((tm, half), row), pl.BlockSpec((tm, half), row),
                  pl.BlockSpec((tm, D_MODEL), row),
                  pl.BlockSpec((half, D_MODEL), fixed), pl.BlockSpec((half, D_MODEL), lambda i: (1, 0)),
                  pl.BlockSpec((1, D_MODEL), fixed), pl.BlockSpec((1, D_MODEL), fixed)],
        out_specs=pl.BlockSpec((tm, D_MODEL), row),
        out_shape=jax.ShapeDtypeStruct((m, D_MODEL), F32),
        compiler_params=_cparams(("parallel",)),
        name="even_out_ln",
    )(a2d, b2d, x2d, w_out_bf16, w_out_bf16, ln_g.reshape(1, -1), ln_b.reshape(1, -1))


def _odd_out_kernel(y_ref, x_ref, w_ref, g_ref, beta_ref, o_ref):
    mix = _dot(y_ref[...], w_ref[...])
    o_ref[...] = _layer_norm(DN_ALPHA * x_ref[...] + mix, g_ref[...], beta_ref[...])


def _odd_out(y2d, x2d, w_out_bf16, ln_g, ln_b, tm=512):
    m = x2d.shape[0]
    row = lambda i: (i, 0)
    fixed = lambda i: (0, 0)
    return pl.pallas_call(
        _odd_out_kernel,
        grid=(m // tm,),
        in_specs=[pl.BlockSpec((tm, D_MODEL), row),
                  pl.BlockSpec((tm, D_MODEL), row),
                  pl.BlockSpec((D_MODEL, D_MODEL), fixed),
                  pl.BlockSpec((1, D_MODEL), fixed), pl.BlockSpec((1, D_MODEL), fixed)],
        out_specs=pl.BlockSpec((tm, D_MODEL), row),
        out_shape=jax.ShapeDtypeStruct((m, D_MODEL), F32),
        compiler_params=_cparams(("parallel",)),
        name="odd_out_ln",
    )(y2d, x2d, w_out_bf16, ln_g.reshape(1, -1), ln_b.reshape(1, -1))


def _ffn_kernel(x_ref, halo_ref, wu_ref, cw_ref, cb_ref, wd_ref, g_ref, beta_ref, o_ref, act_ref, *, tm):
    i = pl.program_id(1)
    x = x_ref[0]
    halo = jnp.where(i > 0, halo_ref[0], 0.0)
    xe = jnp.concatenate([halo, x], axis=0).astype(BF16)
    n_chunks = FFN_HIDDEN // FFN_CHUNK
    for c in range(n_chunks):
        lo = c * FFN_CHUNK
        ue = _dot(xe, wu_ref[:, lo:lo + FFN_CHUNK])
        gate = _dot(xe[CONV_HALO:], wu_ref[:, FFN_HIDDEN + lo:FFN_HIDDEN + lo + FFN_CHUNK])
        cw = cw_ref[:, lo:lo + FFN_CHUNK]
        conv = (cb_ref[:, lo:lo + FFN_CHUNK]
                + ue[CONV_HALO - 2:CONV_HALO - 2 + tm] * cw[0:1]
                + ue[CONV_HALO - 1:CONV_HALO - 1 + tm] * cw[1:2]
                + ue[CONV_HALO:] * cw[2:3])
        act = 0.5 * conv * (1.0 + lax.erf(conv * (2.0 ** -0.5)))
        act_ref[:, lo:lo + FFN_CHUNK] = (act * gate).astype(BF16)
    ffn = _dot(act_ref[...], wd_ref[...])
    o_ref[0] = _layer_norm(DN_ALPHA * x + ffn, g_ref[...], beta_ref[...])


def _ffn(x, w_up_bf16, conv_w, conv_b, w_down_bf16, ln_g, ln_b, tm=512):
    b, s, _ = x.shape
    kern = functools.partial(_ffn_kernel, tm=tm)
    fixed = lambda bi, i: (0, 0)
    halo_blocks = tm // CONV_HALO
    single = pl.Buffered(1)
    return pl.pallas_call(
        kern,
        grid=(b, s // tm),
        in_specs=[pl.BlockSpec((1, tm, D_MODEL), lambda bi, i: (bi, i, 0)),
                  pl.BlockSpec((1, CONV_HALO, D_MODEL),
                               lambda bi, i: (bi, jnp.maximum(i * halo_blocks - 1, 0), 0)),
                  pl.BlockSpec((D_MODEL, 2 * FFN_HIDDEN), fixed, pipeline_mode=single),
                  pl.BlockSpec((3, FFN_HIDDEN), fixed),
                  pl.BlockSpec((1, FFN_HIDDEN), fixed),
                  pl.BlockSpec((FFN_HIDDEN, D_MODEL), fixed, pipeline_mode=single),
                  pl.BlockSpec((1, D_MODEL), fixed), pl.BlockSpec((1, D_MODEL), fixed)],
        out_specs=pl.BlockSpec((1, tm, D_MODEL), lambda bi, i: (bi, i, 0)),
        out_shape=jax.ShapeDtypeStruct((b, s, D_MODEL), F32),
        scratch_shapes=[pltpu.VMEM((tm, FFN_HIDDEN), BF16)],
        compiler_params=_cparams(("parallel", "arbitrary")),
        name="conv_glu_ffn_ln",
    )(x, x, w_up_bf16, conv_w, conv_b.reshape(1, -1), w_down_bf16, ln_g.reshape(1, -1), ln_b.reshape(1, -1))


def _rwkv_proj_kernel(x_ref, halo_ref, mu_ref, wrkv_ref, w0_ref, w1_ref, w2_ref, a0_ref, a1_ref, a2_ref,
                      g1_ref, g2_ref, r_ref, k_ref, v_ref, wl_ref, a_ref, g_ref, *, tm):
    i = pl.program_id(1)
    x = x_ref[0]
    prev_row = jnp.where(i > 0, halo_ref[0, 7:8, :], 0.0)
    row = lax.broadcasted_iota(jnp.int32, (tm, 1), 0)
    x_prev = jnp.where(row == 0, prev_row, pltpu.roll(x, 1, 0))
    xx = x_prev - x

    def mix(n):
        return (x + xx * mu_ref[n:n + 1, :]).astype(BF16)

    r_ref[0] = _dot(mix(0), wrkv_ref[0]).astype(r_ref.dtype)
    k_ref[0] = _dot(mix(1), wrkv_ref[1]).astype(k_ref.dtype)
    v_ref[0] = _dot(mix(2), wrkv_ref[2]).astype(v_ref.dtype)
    lw = jnp.tanh(_dot(mix(3), w1_ref[...])).astype(BF16)
    u = w0_ref[...] + _dot(lw, w2_ref[...])
    wl_ref[0] = -math.exp(-0.5) * jax.nn.sigmoid(u)
    la = _dot(mix(4), a1_ref[...]).astype(BF16)
    a_ref[0] = jax.nn.sigmoid(a0_ref[...] + _dot(la, a2_ref[...])).astype(a_ref.dtype)
    lg = jax.nn.sigmoid(_dot(mix(5), g1_ref[...])).astype(BF16)
    g_ref[0] = _dot(lg, g2_ref[...]).astype(g_ref.dtype)


def _rwkv_proj(x, mu, w_rkv, w0, w1, w2, a0, a1, a2, g1, g2, tm=512):
    b, s, d = x.shape
    kern = functools.partial(_rwkv_proj_kernel, tm=tm)
    fixed2 = lambda bi, i: (0, 0)
    tile = pl.BlockSpec((1, tm, d), lambda bi, i: (bi, i, 0))
    lora = w1.shape[1]
    out = jax.ShapeDtypeStruct((b, s, d), F32)
    half = jax.ShapeDtypeStruct((b, s, d), BF16)
    return pl.pallas_call(
        kern,
        grid=(b, s // tm),
        in_specs=[tile,
                  pl.BlockSpec((1, 8, d), lambda bi, i: (bi, jnp.maximum(i * (tm // 8) - 1, 0), 0)),
                  pl.BlockSpec((6, d), fixed2),
                  pl.BlockSpec((3, d, d), lambda bi, i: (0, 0, 0)),
                  pl.BlockSpec((1, d), fixed2), pl.BlockSpec((d, lora), fixed2), pl.BlockSpec((lora, d), fixed2),
                  pl.BlockSpec((1, d), fixed2), pl.BlockSpec((d, lora), fixed2), pl.BlockSpec((lora, d), fixed2),
                  pl.BlockSpec((d, GATE_LORA_PAD), fixed2), pl.BlockSpec((GATE_LORA_PAD, d), fixed2)],
        out_specs=[tile] * 6,
        out_shape=[half, half, half, out, half, half],
        compiler_params=_cparams(("parallel", "arbitrary")),
        name="rwkv_proj",
    )(x, x, mu, w_rkv, w0.reshape(1, -1), w1, w2, a0.reshape(1, -1), a1, a2, g1, g2)


def _each(f, *lists):
    return [f(*xs) for xs in zip(*lists)]


def _wkv_pair_consts():
    T = WKV_CHUNK
    lane = lax.broadcasted_iota(jnp.int32, (1, LANE), 1)
    row = lax.broadcasted_iota(jnp.int32, (T, LANE), 0)
    col = lax.broadcasted_iota(jnp.int32, (T, LANE), 1) % T
    r2 = lax.broadcasted_iota(jnp.int32, (LANE, LANE), 0)
    c2 = lax.broadcasted_iota(jnp.int32, (LANE, LANE), 1)
    tri_r = lax.broadcasted_iota(jnp.int32, (T, T), 0)
    tri_c = lax.broadcasted_iota(jnp.int32, (T, T), 1)
    same_head = (r2 // T) == (c2 // T)
    merges = []
    size = 2 * WKV_INV_BASE
    while size <= T:
        merges.append(((row // size) == (col // size)) & ((row // (size // 2)) != (col // (size // 2))))
        size *= 2
    return dict(
        even=lane < T,
        strict=col < row, incl=col <= row, eye=col == row,
        in_base=(row // WKV_INV_BASE) == (col // WKV_INV_BASE),
        merges=merges,
        same_head=same_head, eye128=r2 == c2,
        tri=jnp.where(tri_c <= tri_r, 1.0, 0.0).astype(BF16),
    )


def _head_sum(x, even):
    s_even = jnp.sum(jnp.where(even, x, 0.0), axis=-1, keepdims=True)
    s_odd = jnp.sum(jnp.where(even, 0.0, x), axis=-1, keepdims=True)
    return jnp.where(even, s_even, s_odd)


def _wkv_prepare(r, k, v, wl, lr, kk_scale, ka, rk, cst):
    T = WKV_CHUNK
    bf = lambda t: t.astype(BF16)
    even, strict, incl, eye = cst["even"], cst["strict"], cst["incl"], cst["eye"]
    tri_incl_bf16 = cst["tri"]

    def stack2(x):
        zero = jnp.zeros_like(x)
        return jnp.concatenate([jnp.where(even, x, zero), jnp.where(even, zero, x)], axis=0)

    def pair_dot(packed, x):
        return _dot(packed, stack2(x))

    kk = _each(lambda k_, s_: k_ * s_, k, kk_scale)
    ss = _each(lambda t: _head_sum(t * t, even), kk)
    kk = _each(lambda t, s_: t / jnp.maximum(jnp.sqrt(s_), 1e-12), kk, ss)
    k = _each(lambda k_, lr_, ka_: k_ * (1.0 + (lr_ - 1.0) * ka_), k, lr, ka)
    b_vec = _each(lambda kk_, lr_: kk_ * lr_, kk, lr)

    w_hi = _each(bf, wl)
    w_r1 = _each(lambda w_, h_: w_ - h_.astype(F32), wl, w_hi)
    w_mid = _each(bf, w_r1)
    w_lo = _each(lambda r1, m_: (r1 - m_.astype(F32)).astype(BF16), w_r1, w_mid)
    cl = _each(lambda h_, m_, l_: _dot(tri_incl_bf16, h_) + (_dot(tri_incl_bf16, m_) + _dot(tri_incl_bf16, l_)),
               w_hi, w_mid, w_lo)
    cl_last = _each(lambda c_: c_[T - 1:T, :], cl)
    g_inv = _each(lambda c_: jnp.exp(-c_), cl)
    a_hat = _each(lambda kk_, c_, w_: -kk_ * jnp.exp(c_ - w_), kk, cl, wl)
    b_hat = _each(lambda b_, g_: bf(b_ * g_), b_vec, g_inv)
    k_hat = _each(lambda k_, g_: bf(k_ * g_), k, g_inv)
    r_hat = _each(lambda r_, c_: r_ * jnp.exp(c_), r, cl)
    to_end = _each(lambda l_, c_: jnp.exp(l_ - c_), cl_last, cl)
    b_end = _each(lambda b_, e_: bf(b_ * e_), b_vec, to_end)
    k_end = _each(lambda k_, e_: bf(k_ * e_), k, to_end)

    ar = _each(lambda a_, r_: bf(jnp.concatenate([a_, r_], axis=0)), a_hat, r_hat)
    m_b = _each(lambda ar_, b_: _dot_nt(ar_, stack2(b_)), ar, b_hat)
    m_k = _each(lambda ar_, k_: _dot_nt(ar_, stack2(k_)), ar, k_hat)
    a_ab = _each(lambda m_: jnp.where(strict, m_[:T], 0.0), m_b)
    a_rb = _each(lambda m_: bf(jnp.where(incl, m_[T:], 0.0)), m_b)
    a_ak = _each(lambda m_: bf(jnp.where(strict, m_[:T], 0.0)), m_k)
    a_rk = _each(lambda m_: bf(jnp.where(incl, m_[T:], 0.0)), m_k)

    p = _each(lambda a_: jnp.where(cst["in_base"], a_, 0.0), a_ab)
    minv = _each(lambda p_: jnp.where(eye, 1.0, 0.0) + p_, p)
    for _ in range(int(math.log2(WKV_INV_BASE)) - 1):
        p = _each(lambda p_: pair_dot(bf(p_), bf(p_)), p)
        minv = _each(lambda m_, p_: m_ + pair_dot(bf(p_), bf(m_)), minv, p)
    for off in cst["merges"]:
        a_off = _each(lambda a_: bf(jnp.where(off, a_, 0.0)), a_ab)
        minv = _each(lambda m_, a_: m_ + pair_dot(bf(m_), bf(pair_dot(a_, bf(m_)))), minv, a_off)

    v_bf16 = _each(bf, v)
    minv_bf16 = _each(bf, minv)
    a_t = _each(lambda m_, a_: bf(pair_dot(m_, a_[:T])), minv_bf16, ar)
    akv = _each(lambda a_, v_: bf(pair_dot(a_, v_)), a_ak, v_bf16)
    w_bf16 = _each(lambda m_, x_: bf(pair_dot(m_, x_)), minv_bf16, akv)
    r_t = _each(lambda r_, a_, t_: bf(r_ + pair_dot(a_, t_)), r_hat, a_rb, a_t)
    y0 = _each(lambda arb_, w_, ark_, v_: pair_dot(arb_, w_) + pair_dot(ark_, v_), a_rb, w_bf16, a_rk, v_bf16)
    same_head, eye128 = cst["same_head"], cst["eye128"]
    trans = _each(lambda l_, t_, b_: bf(jnp.where(eye128, jnp.exp(l_), 0.0)
                                        + jnp.where(same_head, _dot_tn(t_, b_), 0.0)),
                  cl_last, a_t, b_end)
    s_add = _each(lambda w_, b_, v_, k_: jnp.where(same_head, _dot_tn(w_, b_) + _dot_tn(v_, k_), 0.0),
                  w_bf16, b_end, v_bf16, k_end)
    bonus = _each(lambda r_, k_, rk_, v_: _head_sum(r_ * k_ * rk_, even) * v_, r, k, rk, v)
    return trans, s_add, r_t, y0, bonus


def _wkv_kernel(r_ref, k_ref, v_ref, wl_ref, a_ref, gate_ref, kk_ref, ka_ref, rk_ref, lg_ref, lb_ref, o_ref,
                state_ref, trans_ref, sadd_ref, rt_ref, y0_ref, bonus_ref, *, pairs, unroll):
    @pl.when(pl.program_id(2) == 0)
    def _():
        state_ref[...] = jnp.zeros_like(state_ref)

    T = WKV_CHUNK
    n_chunks = WKV_TIME_BLOCK // T
    cst = _wkv_pair_consts()
    even = cst["even"]
    pair_cols = [slice(pp * LANE, (pp + 1) * LANE) for pp in range(pairs)]

    def prepare(group):
        chunk_ids = [group * unroll + cc for cc in range(unroll)]
        problems = [(cc, pp) for cc in range(unroll) for pp in range(pairs)]

        def tiles(ref):
            return [ref[0, pl.ds(pl.multiple_of(chunk_ids[cc] * T, T), T), pair_cols[pp]].astype(F32)
                    for cc, pp in problems]

        def vecs(ref):
            return [ref[:, pair_cols[pp]] for _, pp in problems]

        outs = _wkv_prepare(tiles(r_ref), tiles(k_ref), tiles(v_ref), tiles(wl_ref), tiles(a_ref),
                            vecs(kk_ref), vecs(ka_ref), vecs(rk_ref), cst)
        for ref, vals in zip((trans_ref, sadd_ref, rt_ref, y0_ref, bonus_ref), outs):
            for (cc, pp), val in zip(problems, vals):
                ref[chunk_ids[cc], pp] = val

    def scan(group):
        for cc in range(unroll):
            c = group * unroll + cc
            rows = pl.ds(pl.multiple_of(c * T, T), T)
            state_bf16 = [state_ref[pp].astype(BF16) for pp in range(pairs)]
            y = [_dot_nt(rt_ref[c, pp], state_bf16[pp]) + y0_ref[c, pp] for pp in range(pairs)]
            new_state = [_dot(state_bf16[pp], trans_ref[c, pp]) + sadd_ref[c, pp] for pp in range(pairs)]
            for pp in range(pairs):
                state_ref[pp] = new_state[pp]
            mu = _each(lambda y_: _head_sum(y_, even) * (1.0 / RWKV_HEAD_DIM), y)
            yc = _each(lambda y_, m_: y_ - m_, y, mu)
            var = _each(lambda c_: _head_sum(c_ * c_, even) * (1.0 / RWKV_HEAD_DIM), yc)
            for pp in range(pairs):
                out = (yc[pp] * lax.rsqrt(var[pp] + RWKV_GN_EPS) * lg_ref[:, pair_cols[pp]]
                       + lb_ref[:, pair_cols[pp]] + bonus_ref[c, pp])
                gated = out * gate_ref[0, rows, pair_cols[pp]].astype(F32)
                o_ref[0, rows, pair_cols[pp]] = gated.astype(o_ref.dtype)

    n_groups = n_chunks // unroll
    prepare(0)

    def pipelined(group, _):
        scan(group - 1)
        prepare(group)
        return 0

    lax.fori_loop(1, n_groups, pipelined, 0)
    scan(n_groups - 1)


def _wkv(r, k, v, wl, a, gate, k_k, k_a, r_k, lnx_g, lnx_b, pairs=2, unroll=8):
    b, s, d = r.shape
    tb = WKV_TIME_BLOCK
    width = pairs * LANE
    n_chunks = tb // WKV_CHUNK
    tile = pl.BlockSpec((1, tb, width), lambda bi, h, t: (bi, t, h))
    vec = pl.BlockSpec((1, width), lambda bi, h, t: (0, h))
    per_chunk = lambda rows, dt: pltpu.VMEM((n_chunks, pairs, rows, LANE), dt)
    kern = functools.partial(_wkv_kernel, pairs=pairs, unroll=unroll)
    return pl.pallas_call(
        kern,
        grid=(b, d // width, s // tb),
        in_specs=[tile] * 6 + [vec] * 5,
        out_specs=tile,
        out_shape=jax.ShapeDtypeStruct((b, s, d), BF16),
        scratch_shapes=[pltpu.VMEM((pairs, LANE, LANE), F32),
                        per_chunk(LANE, BF16), per_chunk(LANE, F32),
                        per_chunk(WKV_CHUNK, BF16), per_chunk(WKV_CHUNK, F32), per_chunk(WKV_CHUNK, F32)],
        compiler_params=_cparams(("parallel", "parallel", "arbitrary")),
        name="wkv7",
    )(r, k, v, wl, a, gate, *(t.reshape(1, -1) for t in (k_k, k_a, r_k, lnx_g, lnx_b)))


def kernel(x, ev_w_in, ev_lambda, ev_subln_g, ev_w_out, od_mu, od_w_rkv, od_w0, od_w1, od_w2, od_a0, od_a1, od_a2, od_g1, od_g2, od_k_k, od_k_a, od_r_k, od_lnx_g, od_lnx_b, od_w_out, ln_mix_g, ln_mix_b, ffn_w_up, ffn_conv_w, ffn_conv_b, ffn_w_down, ln_ffn_g, ln_ffn_b):
    b, s, d = x.shape
    m = b * s
    bf = lambda t: t.astype(BF16)

    qkv, gate = _in_proj(x.reshape(m, d), bf(ev_w_in[0]))
    qkv = qkv.reshape(b, s, EVEN_QKV_WIDTH)
    lambda_init = 0.8 - 0.6 * math.exp(-0.3 * 0)
    a_out = _diff_attention(qkv, ev_lambda[0], ev_subln_g[0], lambda_init)
    b_out = _retention(qkv, gate.reshape(b, s, RET_GATE_WIDTH))
    x1 = _even_out(a_out.reshape(m, -1), b_out.reshape(m, -1), x.reshape(m, d), bf(ev_w_out[0]),
                   ln_mix_g[0], ln_mix_b[0]).reshape(b, s, d)
    x2 = _ffn(x1, bf(ffn_w_up[0]), ffn_conv_w[0], ffn_conv_b[0], bf(ffn_w_down[0]), ln_ffn_g[0], ln_ffn_b[0])

    pad = GATE_LORA_PAD - GATE_LORA
    g1 = jnp.pad(bf(od_g1[0]), ((0, 0), (0, pad)))
    g2 = jnp.pad(bf(od_g2[0]), ((0, pad), (0, 0)))
    r, k, v, wl, a, g = _rwkv_proj(x2, od_mu[0], bf(od_w_rkv[0]), od_w0[0], bf(od_w1[0]), bf(od_w2[0]),
                                   od_a0[0], bf(od_a1[0]), bf(od_a2[0]), g1, g2)
    y = _wkv(r, k, v, wl, a, g, od_k_k[0], od_k_a[0], od_r_k[0], od_lnx_g[0], od_lnx_b[0])
    x3 = _odd_out(y.reshape(m, d), x2.reshape(m, d), bf(od_w_out[0]),
                  ln_mix_g[1], ln_mix_b[1]).reshape(b, s, d)
    x4 = _ffn(x3, bf(ffn_w_up[1]), ffn_conv_w[1], ffn_conv_b[1], bf(ffn_w_down[1]), ln_ffn_g[1], ln_ffn_b[1])
    return x4
```

```python
import functools
import math

import jax
import jax.numpy as jnp
from jax import lax
from jax.experimental import pallas as pl
from jax.experimental.pallas import tpu as pltpu

F32 = jnp.float32
BF16 = jnp.bfloat16

D_MODEL = 1024
DEPTH = 2
DN_ALPHA = (2.0 * DEPTH) ** 0.25
LN_EPS = 1e-5

DIFF_HEAD_DIM = 64
DIFF_HEADS = 4
RET_QK_DIM = 64
RET_V_DIM = 128
RET_HEADS = 4
RET_CHUNK = 128
EVEN_IN_WIDTH = 3072
EVEN_QKV_WIDTH = 2560
RET_GATE_WIDTH = 512

RWKV_HEAD_DIM = 64
RWKV_GN_EPS = 64e-5
GATE_LORA = 160
GATE_LORA_PAD = 256
WKV_CHUNK = 64
WKV_INV_BASE = 8
WKV_TIME_BLOCK = 1024

FFN_HIDDEN = 2816
FFN_CHUNK = 256
CONV_HALO = 16

LANE = 128
VMEM_LIMIT = 56 * 1024 * 1024


def _cparams(sem):
    return pltpu.CompilerParams(dimension_semantics=sem, vmem_limit_bytes=VMEM_LIMIT)


def _dot(a, b):
    return jnp.dot(a, b, preferred_element_type=F32)


def _dot_nt(a, b):
    return lax.dot_general(a, b, (((1,), (1,)), ((), ())), preferred_element_type=F32)


def _dot_tn(a, b):
    return lax.dot_general(a, b, (((0,), (0,)), ((), ())), preferred_element_type=F32)


def _layer_norm(z, g, b):
    mu = jnp.mean(z, axis=-1, keepdims=True)
    zc = z - mu
    var = jnp.mean(zc * zc, axis=-1, keepdims=True)
    return zc * lax.rsqrt(var + LN_EPS) * g + b


def _in_proj_kernel(x_ref, w_ref, qkv_ref, gate_ref):
    h = _dot(x_ref[...].astype(BF16), w_ref[...])
    qkv_ref[...] = h[:, :EVEN_QKV_WIDTH].astype(BF16)
    gate_ref[...] = h[:, EVEN_QKV_WIDTH:]


def _in_proj(x2d, w_bf16, tm=512):
    m = x2d.shape[0]
    return pl.pallas_call(
        _in_proj_kernel,
        grid=(m // tm,),
        in_specs=[pl.BlockSpec((tm, D_MODEL), lambda i: (i, 0)),
                  pl.BlockSpec((D_MODEL, EVEN_IN_WIDTH), lambda i: (0, 0))],
        out_specs=[pl.BlockSpec((tm, EVEN_QKV_WIDTH), lambda i: (i, 0)),
                   pl.BlockSpec((tm, RET_GATE_WIDTH), lambda i: (i, 0))],
        out_shape=[jax.ShapeDtypeStruct((m, EVEN_QKV_WIDTH), BF16),
                   jax.ShapeDtypeStruct((m, RET_GATE_WIDTH), F32)],
        compiler_params=_cparams(("parallel",)),
        name="in_proj",
    )(x2d, w_bf16)


def _diff_attn_kernel(lam_ref, g_ref, q_ref, k_ref, v_ref, o_ref, *, tq, lambda_init):
    h = pl.program_id(1)
    qi = pl.program_id(2)
    tk = tq
    slope = jnp.where(h == 0, 2.0 ** -2, jnp.where(h == 1, 2.0 ** -4,
                      jnp.where(h == 2, 2.0 ** -6, 2.0 ** -8))).astype(F32)
    lp = lam_ref[...]
    lam = (jnp.exp(jnp.sum(lp[0:1] * lp[1:2], axis=-1, keepdims=True))
           - jnp.exp(jnp.sum(lp[2:3] * lp[3:4], axis=-1, keepdims=True)) + lambda_init)

    lane = lax.broadcasted_iota(jnp.int32, (1, LANE), 1)
    q = q_ref[0] * jnp.asarray(DIFF_HEAD_DIM ** -0.5, BF16)
    zero = jnp.zeros_like(q)
    q_pos = jnp.broadcast_to(jnp.where(lane < 2, slope, 0.0).astype(BF16), (tq, LANE))
    q12 = jnp.concatenate([jnp.concatenate([jnp.where(lane < DIFF_HEAD_DIM, q, zero), q_pos], axis=1),
                           jnp.concatenate([jnp.where(lane >= DIFF_HEAD_DIM, q, zero), q_pos], axis=1)],
                          axis=0)
    key_off = lax.broadcasted_iota(jnp.int32, (tk, LANE), 0)
    k_pos = jnp.where(lane == 0, key_off & 255, jnp.where(lane == 1, key_off & ~255, 0)).astype(F32).astype(BF16)

    def scores_t(j):
        start = pl.multiple_of(j * tk, tk)
        k = jnp.concatenate([k_ref[0, pl.ds(start, tk), :], k_pos], axis=1)
        return _dot_nt(k, q12)

    def values(j):
        return v_ref[0, pl.ds(pl.multiple_of(j * tk, tk), tk), :]

    def update(carry, s_t, v, shift):
        m, l, acc_t = carry
        m_new = jnp.maximum(m, jnp.max(s_t, axis=0, keepdims=True) + shift)
        alpha = jnp.exp(m - m_new)
        p_t = jnp.exp(s_t - (m_new - shift))
        l = alpha * l + jnp.sum(p_t, axis=0, keepdims=True)
        acc_t = alpha * acc_t + _dot_tn(v, p_t.astype(BF16))
        return m_new, l, acc_t

    def body(j, carry):
        return update(carry, scores_t(j), values(j), ((j - qi) * tk).astype(F32) * slope)

    init = (jnp.full((1, 2 * tq), -1e30, F32), jnp.zeros((1, 2 * tq), F32), jnp.zeros((LANE, 2 * tq), F32))
    carry = lax.fori_loop(0, qi, body, init)
    key_i = lax.broadcasted_iota(jnp.int32, (tk, 2 * tq), 0)
    query_i = lax.broadcasted_iota(jnp.int32, (tk, 2 * tq), 1) % tq
    _, l, acc_t = update(carry, jnp.where(key_i <= query_i, scores_t(qi), -1e30), values(qi), 0.0)
    o_t = acc_t[:, :tq] / l[:, :tq] - lam * (acc_t[:, tq:] / l[:, tq:])
    o_t = o_t * lax.rsqrt(jnp.mean(o_t * o_t, axis=0, keepdims=True) + LN_EPS)
    o_ref[0] = (o_t.T * g_ref[...] * (1.0 - lambda_init)).astype(o_ref.dtype)


def _diff_attention(qkv, lam_p, subln_g, lambda_init, tq=512):
    b, s, _ = qkv.shape
    kern = functools.partial(_diff_attn_kernel, tq=tq, lambda_init=lambda_init)
    return pl.pallas_call(
        kern,
        grid=(b, DIFF_HEADS, s // tq),
        in_specs=[pl.BlockSpec((4, DIFF_HEAD_DIM), lambda bi, h, i: (0, 0)),
                  pl.BlockSpec((1, LANE), lambda bi, h, i: (0, 0)),
                  pl.BlockSpec((1, tq, LANE), lambda bi, h, i: (bi, i, h)),
                  pl.BlockSpec((1, s, LANE), lambda bi, h, i: (bi, 0, DIFF_HEADS + h)),
                  pl.BlockSpec((1, s, LANE), lambda bi, h, i: (bi, 0, 2 * DIFF_HEADS + h))],
        out_specs=pl.BlockSpec((1, tq, LANE), lambda bi, h, i: (bi, i, h)),
        out_shape=jax.ShapeDtypeStruct((b, s, DIFF_HEADS * LANE), BF16),
        compiler_params=_cparams(("parallel", "parallel", "arbitrary")),
        name="diff_attn",
    )(lam_p, subln_g.reshape(1, LANE), qkv, qkv, qkv)


def _retention_kernel(q_ref, k_ref, v_ref, g_ref, o_ref, state_ref, *, chunks_per_step):
    C = RET_CHUNK

    @pl.when(pl.program_id(1) == 0)
    def _():
        state_ref[...] = jnp.zeros_like(state_ref)

    lane = lax.broadcasted_iota(jnp.int32, (1, LANE), 1)
    row128 = lax.broadcasted_iota(jnp.int32, (LANE, 1), 0)
    ri = lax.broadcasted_iota(jnp.int32, (C, C), 0)
    ci = lax.broadcasted_iota(jnp.int32, (C, C), 1)
    rel = (ri - ci).astype(F32)
    idx = lax.broadcasted_iota(jnp.int32, (C, 1), 0).astype(F32)
    heads = range(RET_HEADS)
    log_gamma = [math.log1p(-(2.0 ** (-5.0 - h))) for h in heads]
    in_head = [(lane >= (h % 2) * RET_QK_DIM) & (lane < (h % 2 + 1) * RET_QK_DIM) for h in heads]
    in_rows = [(row128 >= (h % 2) * RET_QK_DIM) & (row128 < (h % 2 + 1) * RET_QK_DIM) for h in heads]
    decay = [jnp.where(rel >= 0, jnp.exp(lg * jnp.maximum(rel, 0.0)), 0.0) for lg in log_gamma]
    q_decay = [jnp.exp(lg * (idx + 1.0)) for lg in log_gamma]
    k_decay = [jnp.exp(lg * (C - 1.0 - idx)) for lg in log_gamma]
    v_cols = [slice(h * RET_V_DIM, (h + 1) * RET_V_DIM) for h in heads]

    def chunk_body(c, _):
        rows = pl.ds(pl.multiple_of(c * C, C), C)
        q_all = q_ref[0, rows, :].astype(F32)
        k_all = k_ref[0, rows, :].astype(F32) * (RET_QK_DIM ** -0.5)
        q = [jnp.where(in_head[h], q_all[:, (h // 2) * LANE:(h // 2 + 1) * LANE], 0.0) for h in heads]
        k = [k_all[:, (h // 2) * LANE:(h // 2 + 1) * LANE] for h in heads]
        v = [v_ref[0, rows, v_cols[h]] for h in heads]
        sc = [_dot_nt(q[h].astype(BF16), k[h].astype(BF16)) * decay[h] for h in heads]
        inner = [_dot(sc[h].astype(BF16), v[h]) for h in heads]
        state = [state_ref[h] for h in heads]
        cross = [_dot((q[h] * q_decay[h]).astype(BF16), state[h].astype(BF16)) for h in heads]
        upd = [_dot_tn((k[h] * k_decay[h]).astype(BF16), v[h]) for h in heads]
        for h in heads:
            state_ref[h] = state[h] * math.exp(log_gamma[h] * C) + jnp.where(in_rows[h], upd[h], 0.0)
        y = [inner[h] + cross[h] for h in heads]
        mu = [jnp.mean(y_, axis=-1, keepdims=True) for y_ in y]
        yc = [y_ - m_ for y_, m_ in zip(y, mu)]
        var = [jnp.mean(c_ * c_, axis=-1, keepdims=True) for c_ in yc]
        for h in heads:
            g = g_ref[0, rows, v_cols[h]]
            o_ref[0, rows, v_cols[h]] = (g * jax.nn.sigmoid(g)
                                         * (yc[h] * lax.rsqrt(var[h] + LN_EPS))).astype(o_ref.dtype)
        return 0

    lax.fori_loop(0, chunks_per_step, chunk_body, 0)


def _retention(qkv, gate, chunks_per_step=4):
    b, s, _ = qkv.shape
    C = RET_CHUNK * chunks_per_step
    qk_w = RET_HEADS * RET_QK_DIM
    v_w = RET_HEADS * RET_V_DIM
    return pl.pallas_call(
        functools.partial(_retention_kernel, chunks_per_step=chunks_per_step),
        grid=(b, s // C),
        in_specs=[pl.BlockSpec((1, C, qk_w), lambda bi, c: (bi, c, 1536 // qk_w)),
                  pl.BlockSpec((1, C, qk_w), lambda bi, c: (bi, c, 1792 // qk_w)),
                  pl.BlockSpec((1, C, v_w), lambda bi, c: (bi, c, 2048 // v_w)),
                  pl.BlockSpec((1, C, v_w), lambda bi, c: (bi, c, 0))],
        out_specs=pl.BlockSpec((1, C, v_w), lambda bi, c: (bi, c, 0)),
        out_shape=jax.ShapeDtypeStruct((b, s, v_w), BF16),
        scratch_shapes=[pltpu.VMEM((RET_HEADS, LANE, RET_V_DIM), F32)],
        compiler_params=_cparams(("parallel", "arbitrary")),
        name="retention",
    )(qkv, qkv, qkv, gate)


def _even_out_kernel(a_ref, b_ref, x_ref, wa_ref, wb_ref, g_ref, beta_ref, o_ref):
    mix = _dot(a_ref[...], wa_ref[...]) + _dot(b_ref[...], wb_ref[...])
    o_ref[...] = _layer_norm(DN_ALPHA * x_ref[...] + mix, g_ref[...], beta_ref[...])


def _even_out(a2d, b2d, x2d, w_out_bf16, ln_g, ln_b, tm=512):
    m = x2d.shape[0]
    half = D_MODEL // 2
    row = lambda i: (i, 0)
    fixed = lambda i: (0, 0)
    return pl.pallas_call(
        _even_out_kernel,
        grid=(m // tm,),
        in_specs=[pl.BlockSpec((tm, half), row), pl.BlockSpec((tm, half), row),
                  pl.BlockSpec((tm, D_MODEL), row),
                  pl.BlockSpec((half, D_MODEL), fixed), pl.BlockSpec((half, D_MODEL), lambda i: (1, 0)),
                  pl.BlockSpec((1, D_MODEL), fixed), pl.BlockSpec((1, D_MODEL), fixed)],
        out_specs=pl.BlockSpec((tm, D_MODEL), row),
        out_shape=jax.ShapeDtypeStruct((m, D_MODEL), F32),
        compiler_params=_cparams(("parallel",)),
        name="even_out_ln",
    )(a2d, b2d, x2d, w_out_bf16, w_out_bf16, ln_g.reshape(1, -1), ln_b.reshape(1, -1))


def _odd_out_kernel(y_ref, x_ref, w_ref, g_ref, beta_ref, o_ref):
    mix = _dot(y_ref[...], w_ref[...])
    o_ref[...] = _layer_norm(DN_ALPHA * x_ref[...] + mix, g_ref[...], beta_ref[...])


def _odd_out(y2d, x2d, w_out_bf16, ln_g, ln_b, tm=512):
    m = x2d.shape[0]
    row = lambda i: (i, 0)
    fixed = lambda i: (0, 0)
    return pl.pallas_call(
        _odd_out_kernel,
        grid=(m // tm,),
        in_specs=[pl.BlockSpec((tm, D_MODEL), row),
                  pl.BlockSpec((tm, D_MODEL), row),
                  pl.BlockSpec((D_MODEL, D_MODEL), fixed),
                  pl.BlockSpec((1, D_MODEL), fixed), pl.BlockSpec((1, D_MODEL), fixed)],
        out_specs=pl.BlockSpec((tm, D_MODEL), row),
        out_shape=jax.ShapeDtypeStruct((m, D_MODEL), F32),
        compiler_params=_cparams(("parallel",)),
        name="odd_out_ln",
    )(y2d, x2d, w_out_bf16, ln_g.reshape(1, -1), ln_b.reshape(1, -1))


def _ffn_kernel(x_ref, halo_ref, wu_ref, cw_ref, cb_ref, wd_ref, g_ref, beta_ref, o_ref, act_ref, *, tm):
    i = pl.program_id(1)
    x = x_ref[0]
    halo = jnp.where(i > 0, halo_ref[0], 0.0)
    xe = jnp.concatenate([halo, x], axis=0).astype(BF16)
    n_chunks = FFN_HIDDEN // FFN_CHUNK
    for c in range(n_chunks):
        lo = c * FFN_CHUNK
        ue = _dot(xe, wu_ref[:, lo:lo + FFN_CHUNK])
        gate = _dot(xe[CONV_HALO:], wu_ref[:, FFN_HIDDEN + lo:FFN_HIDDEN + lo + FFN_CHUNK])
        cw = cw_ref[:, lo:lo + FFN_CHUNK]
        conv = (cb_ref[:, lo:lo + FFN_CHUNK]
                + ue[CONV_HALO - 2:CONV_HALO - 2 + tm] * cw[0:1]
                + ue[CONV_HALO - 1:CONV_HALO - 1 + tm] * cw[1:2]
                + ue[CONV_HALO:] * cw[2:3])
        act = 0.5 * conv * (1.0 + lax.erf(conv * (2.0 ** -0.5)))
        act_ref[:, lo:lo + FFN_CHUNK] = (act * gate).astype(BF16)
    ffn = _dot(act_ref[...], wd_ref[...])
    o_ref[0] = _layer_norm(DN_ALPHA * x + ffn, g_ref[...], beta_ref[...])


def _ffn(x, w_up_bf16, conv_w, conv_b, w_down_bf16, ln_g, ln_b, tm=512):
    b, s, _ = x.shape
    kern = functools.partial(_ffn_kernel, tm=tm)
    fixed = lambda bi, i: (0, 0)
    halo_blocks = tm // CONV_HALO
    single = pl.Buffered(1)
    return pl.pallas_call(
        kern,
        grid=(b, s // tm),
        in_specs=[pl.BlockSpec((1, tm, D_MODEL), lambda bi, i: (bi, i, 0)),
                  pl.BlockSpec((1, CONV_HALO, D_MODEL),
                               lambda bi, i: (bi, jnp.maximum(i * halo_blocks - 1, 0), 0)),
                  pl.BlockSpec((D_MODEL, 2 * FFN_HIDDEN), fixed, pipeline_mode=single),
                  pl.BlockSpec((3, FFN_HIDDEN), fixed),
                  pl.BlockSpec((1, FFN_HIDDEN), fixed),
                  pl.BlockSpec((FFN_HIDDEN, D_MODEL), fixed, pipeline_mode=single),
                  pl.BlockSpec((1, D_MODEL), fixed), pl.BlockSpec((1, D_MODEL), fixed)],
        out_specs=pl.BlockSpec((1, tm, D_MODEL), lambda bi, i: (bi, i, 0)),
        out_shape=jax.ShapeDtypeStruct((b, s, D_MODEL), F32),
        scratch_shapes=[pltpu.VMEM((tm, FFN_HIDDEN), BF16)],
        compiler_params=_cparams(("parallel", "arbitrary")),
        name="conv_glu_ffn_ln",
    )(x, x, w_up_bf16, conv_w, conv_b.reshape(1, -1), w_down_bf16, ln_g.reshape(1, -1), ln_b.reshape(1, -1))


def _rwkv_proj_kernel(x_ref, halo_ref, mu_ref, wrkv_ref, w0_ref, w1_ref, w2_ref, a0_ref, a1_ref, a2_ref,
                      g1_ref, g2_ref, r_ref, k_ref, v_ref, wl_ref, a_ref, g_ref, *, tm):
    i = pl.program_id(1)
    x = x_ref[0]
    prev_row = jnp.where(i > 0, halo_ref[0, 7:8, :], 0.0)
    row = lax.broadcasted_iota(jnp.int32, (tm, 1), 0)
    x_prev = jnp.where(row == 0, prev_row, pltpu.roll(x, 1, 0))
    xx = x_prev - x

    def mix(n):
        return (x + xx * mu_ref[n:n + 1, :]).astype(BF16)

    r_ref[0] = _dot(mix(0), wrkv_ref[0]).astype(r_ref.dtype)
    k_ref[0] = _dot(mix(1), wrkv_ref[1]).astype(k_ref.dtype)
    v_ref[0] = _dot(mix(2), wrkv_ref[2]).astype(v_ref.dtype)
    lw = jnp.tanh(_dot(mix(3), w1_ref[...])).astype(BF16)
    u = w0_ref[...] + _dot(lw, w2_ref[...])
    wl_ref[0] = -math.exp(-0.5) * jax.nn.sigmoid(u)
    la = _dot(mix(4), a1_ref[...]).astype(BF16)
    a_ref[0] = jax.nn.sigmoid(a0_ref[...] + _dot(la, a2_ref[...])).astype(a_ref.dtype)
    lg = jax.nn.sigmoid(_dot(mix(5), g1_ref[...])).astype(BF16)
    g_ref[0] = _dot(lg, g2_ref[...]).astype(g_ref.dtype)


def _rwkv_proj(x, mu, w_rkv, w0, w1, w2, a0, a1, a2, g1, g2, tm=512):
    b, s, d = x.shape
    kern = functools.partial(_rwkv_proj_kernel, tm=tm)
    fixed2 = lambda bi, i: (0, 0)
    tile = pl.BlockSpec((1, tm, d), lambda bi, i: (bi, i, 0))
    lora = w1.shape[1]
    out = jax.ShapeDtypeStruct((b, s, d), F32)
    half = jax.ShapeDtypeStruct((b, s, d), BF16)
    return pl.pallas_call(
        kern,
        grid=(b, s // tm),
        in_specs=[tile,
                  pl.BlockSpec((1, 8, d), lambda bi, i: (bi, jnp.maximum(i * (tm // 8) - 1, 0), 0)),
                  pl.BlockSpec((6, d), fixed2),
                  pl.BlockSpec((3, d, d), lambda bi, i: (0, 0, 0)),
                  pl.BlockSpec((1, d), fixed2), pl.BlockSpec((d, lora), fixed2), pl.BlockSpec((lora, d), fixed2),
                  pl.BlockSpec((1, d), fixed2), pl.BlockSpec((d, lora), fixed2), pl.BlockSpec((lora, d), fixed2),
                  pl.BlockSpec((d, GATE_LORA_PAD), fixed2), pl.BlockSpec((GATE_LORA_PAD, d), fixed2)],
        out_specs=[tile] * 6,
        out_shape=[half, half, half, out, half, half],
        compiler_params=_cparams(("parallel", "arbitrary")),
        name="rwkv_proj",
    )(x, x, mu, w_rkv, w0.reshape(1, -1), w1, w2, a0.reshape(1, -1), a1, a2, g1, g2)


def _each(f, *lists):
    return [f(*xs) for xs in zip(*lists)]


def _wkv_pair_consts():
    T = WKV_CHUNK
    lane = lax.broadcasted_iota(jnp.int32, (1, LANE), 1)
    row = lax.broadcasted_iota(jnp.int32, (T, LANE), 0)
    col = lax.broadcasted_iota(jnp.int32, (T, LANE), 1) % T
    r2 = lax.broadcasted_iota(jnp.int32, (LANE, LANE), 0)
    c2 = lax.broadcasted_iota(jnp.int32, (LANE, LANE), 1)
    tri_r = lax.broadcasted_iota(jnp.int32, (T, T), 0)
    tri_c = lax.broadcasted_iota(jnp.int32, (T, T), 1)
    same_head = (r2 // T) == (c2 // T)
    merges = []
    size = 2 * WKV_INV_BASE
    while size <= T:
        merges.append(((row // size) == (col // size)) & ((row // (size // 2)) != (col // (size // 2))))
        size *= 2
    return dict(
        even=lane < T,
        strict=col < row, incl=col <= row, eye=col == row,
        in_base=(row // WKV_INV_BASE) == (col // WKV_INV_BASE),
        merges=merges,
        same_head=same_head, eye128=r2 == c2,
        tri=jnp.where(tri_c <= tri_r, 1.0, 0.0).astype(BF16),
    )


def _head_sum(x, even):
    s_even = jnp.sum(jnp.where(even, x, 0.0), axis=-1, keepdims=True)
    s_odd = jnp.sum(jnp.where(even, 0.0, x), axis=-1, keepdims=True)
    return jnp.where(even, s_even, s_odd)


def _wkv_prepare(r, k, v, wl, lr, kk_scale, ka, rk, cst):
    T = WKV_CHUNK
    bf = lambda t: t.astype(BF16)
    even, strict, incl, eye = cst["even"], cst["strict"], cst["incl"], cst["eye"]
    tri_incl_bf16 = cst["tri"]

    def stack2(x):
        zero = jnp.zeros_like(x)
        return jnp.concatenate([jnp.where(even, x, zero), jnp.where(even, zero, x)], axis=0)

    def pair_dot(packed, x):
        return _dot(packed, stack2(x))

    def pair_dot2(packed, x1, x2):
        return _dot(packed, jnp.concatenate([stack2(x1), stack2(x2)], axis=1))

    kk = _each(lambda k_, s_: k_ * s_, k, kk_scale)
    ss = _each(lambda t: _head_sum(t * t, even), kk)
    kk = _each(lambda t, s_: t / jnp.maximum(jnp.sqrt(s_), 1e-12), kk, ss)
    k = _each(lambda k_, lr_, ka_: k_ * (1.0 + (lr_ - 1.0) * ka_), k, lr, ka)
    b_vec = _each(lambda kk_, lr_: kk_ * lr_, kk, lr)

    w_hi = _each(bf, wl)
    w_r1 = _each(lambda w_, h_: w_ - h_.astype(F32), wl, w_hi)
    w_mid = _each(bf, w_r1)
    w_lo = _each(lambda r1, m_: (r1 - m_.astype(F32)).astype(BF16), w_r1, w_mid)
    cl = _each(lambda h_, m_, l_: _dot(tri_incl_bf16, h_) + (_dot(tri_incl_bf16, m_) + _dot(tri_incl_bf16, l_)),
               w_hi, w_mid, w_lo)
    cl_last = _each(lambda c_: c_[T - 1:T, :], cl)
    g_inv = _each(lambda c_: jnp.exp(-c_), cl)
    a_hat = _each(lambda kk_, c_, w_: -kk_ * jnp.exp(c_ - w_), kk, cl, wl)
    b_hat = _each(lambda b_, g_: bf(b_ * g_), b_vec, g_inv)
    k_hat = _each(lambda k_, g_: bf(k_ * g_), k, g_inv)
    r_hat = _each(lambda r_, c_: r_ * jnp.exp(c_), r, cl)
    to_end = _each(lambda l_, c_: jnp.exp(l_ - c_), cl_last, cl)
    b_end = _each(lambda b_, e_: bf(b_ * e_), b_vec, to_end)
    k_end = _each(lambda k_, e_: bf(k_ * e_), k, to_end)

    ar = _each(lambda a_, r_: bf(jnp.concatenate([a_, r_], axis=0)), a_hat, r_hat)
    m_b = _each(lambda ar_, b_: _dot_nt(ar_, stack2(b_)), ar, b_hat)
    m_k = _each(lambda ar_, k_: _dot_nt(ar_, stack2(k_)), ar, k_hat)
    a_ab = _each(lambda m_: jnp.where(strict, m_[:T], 0.0), m_b)
    a_rb = _each(lambda m_: bf(jnp.where(incl, m_[T:], 0.0)), m_b)
    a_ak = _each(lambda m_: bf(jnp.where(strict, m_[:T], 0.0)), m_k)
    a_rk = _each(lambda m_: bf(jnp.where(incl, m_[T:], 0.0)), m_k)

    p = _each(lambda a_: jnp.where(cst["in_base"], a_, 0.0), a_ab)
    minv = _each(lambda p_: jnp.where(eye, 1.0, 0.0) + p_, p)
    levels = int(math.log2(WKV_INV_BASE)) - 1
    p = _each(lambda p_: pair_dot(bf(p_), bf(p_)), p)
    for level in range(levels):
        if level + 1 < levels:
            both = _each(lambda p_, m_: pair_dot2(bf(p_), bf(m_), bf(p_)), p, minv)
            minv = _each(lambda m_, x_: m_ + x_[:, :LANE], minv, both)
            p = _each(lambda x_: x_[:, LANE:], both)
        else:
            minv = _each(lambda m_, p_: m_ + pair_dot(bf(p_), bf(m_)), minv, p)
    for off in cst["merges"]:
        a_off = _each(lambda a_: bf(jnp.where(off, a_, 0.0)), a_ab)
        minv = _each(lambda m_, a_: m_ + pair_dot(bf(m_), bf(pair_dot(a_, bf(m_)))), minv, a_off)

    v_bf16 = _each(bf, v)
    minv_bf16 = _each(bf, minv)
    ak_rk_v = _each(lambda ak_, rk_, v_: pair_dot(jnp.concatenate([ak_, rk_], axis=0), v_), a_ak, a_rk, v_bf16)
    akv = _each(lambda x_: bf(x_[:T]), ak_rk_v)
    at_w = _each(lambda m_, a_, x_: bf(pair_dot2(m_, a_[:T], x_)), minv_bf16, ar, akv)
    a_t = _each(lambda x_: x_[:, :LANE], at_w)
    w_bf16 = _each(lambda x_: x_[:, LANE:], at_w)
    rb_at_w = _each(lambda a_, t_, w_: pair_dot2(a_, t_, w_), a_rb, a_t, w_bf16)
    r_t = _each(lambda r_, x_: bf(r_ + x_[:, :LANE]), r_hat, rb_at_w)
    y0 = _each(lambda x_, y_: x_[:, LANE:] + y_[T:], rb_at_w, ak_rk_v)
    same_head, eye128 = cst["same_head"], cst["eye128"]
    trans = _each(lambda l_, t_, b_: bf(jnp.where(eye128, jnp.exp(l_), 0.0)
                                        + jnp.where(same_head, _dot_tn(t_, b_), 0.0)),
                  cl_last, a_t, b_end)
    s_add = _each(lambda w_, b_, v_, k_: jnp.where(same_head, _dot_tn(w_, b_) + _dot_tn(v_, k_), 0.0),
                  w_bf16, b_end, v_bf16, k_end)
    bonus = _each(lambda r_, k_, rk_, v_: _head_sum(r_ * k_ * rk_, even) * v_, r, k, rk, v)
    return trans, s_add, r_t, y0, bonus


def _wkv_kernel(r_ref, k_ref, v_ref, wl_ref, a_ref, gate_ref, kk_ref, ka_ref, rk_ref, lg_ref, lb_ref, o_ref,
                state_ref, trans_ref, sadd_ref, rt_ref, y0_ref, bonus_ref, *, pairs, unroll):
    @pl.when(pl.program_id(2) == 0)
    def _():
        state_ref[...] = jnp.zeros_like(state_ref)

    T = WKV_CHUNK
    n_chunks = WKV_TIME_BLOCK // T
    cst = _wkv_pair_consts()
    even = cst["even"]
    pair_cols = [slice(pp * LANE, (pp + 1) * LANE) for pp in range(pairs)]

    def prepare(group):
        chunk_ids = [group * unroll + cc for cc in range(unroll)]
        problems = [(cc, pp) for cc in range(unroll) for pp in range(pairs)]

        def tiles(ref):
            return [ref[0, pl.ds(pl.multiple_of(chunk_ids[cc] * T, T), T), pair_cols[pp]].astype(F32)
                    for cc, pp in problems]

        def vecs(ref):
            return [ref[:, pair_cols[pp]] for _, pp in problems]

        outs = _wkv_prepare(tiles(r_ref), tiles(k_ref), tiles(v_ref), tiles(wl_ref), tiles(a_ref),
                            vecs(kk_ref), vecs(ka_ref), vecs(rk_ref), cst)
        for ref, vals in zip((trans_ref, sadd_ref, rt_ref, y0_ref, bonus_ref), outs):
            for (cc, pp), val in zip(problems, vals):
                ref[chunk_ids[cc], pp] = val

    def scan(group):
        for cc in range(unroll):
            c = group * unroll + cc
            rows = pl.ds(pl.multiple_of(c * T, T), T)
            state_bf16 = [state_ref[pp].astype(BF16) for pp in range(pairs)]
            y = [_dot_nt(rt_ref[c, pp], state_bf16[pp]) + y0_ref[c, pp] for pp in range(pairs)]
            new_state = [_dot(state_bf16[pp], trans_ref[c, pp]) + sadd_ref[c, pp] for pp in range(pairs)]
            for pp in range(pairs):
                state_ref[pp] = new_state[pp]
            mu = _each(lambda y_: _head_sum(y_, even) * (1.0 / RWKV_HEAD_DIM), y)
            yc = _each(lambda y_, m_: y_ - m_, y, mu)
            var = _each(lambda c_: _head_sum(c_ * c_, even) * (1.0 / RWKV_HEAD_DIM), yc)
            for pp in range(pairs):
                out = (yc[pp] * lax.rsqrt(var[pp] + RWKV_GN_EPS) * lg_ref[:, pair_cols[pp]]
                       + lb_ref[:, pair_cols[pp]] + bonus_ref[c, pp])
                gated = out * gate_ref[0, rows, pair_cols[pp]].astype(F32)
                o_ref[0, rows, pair_cols[pp]] = gated.astype(o_ref.dtype)

    n_groups = n_chunks // unroll
    prepare(0)

    def pipelined(group, _):
        scan(group - 1)
        prepare(group)
        return 0

    lax.fori_loop(1, n_groups, pipelined, 0)
    scan(n_groups - 1)


def _wkv(r, k, v, wl, a, gate, k_k, k_a, r_k, lnx_g, lnx_b, pairs=2, unroll=8):
    b, s, d = r.shape
    tb = WKV_TIME_BLOCK
    width = pairs * LANE
    n_chunks = tb // WKV_CHUNK
    tile = pl.BlockSpec((1, tb, width), lambda bi, h, t: (bi, t, h))
    vec = pl.BlockSpec((1, width), lambda bi, h, t: (0, h))
    per_chunk = lambda rows, dt: pltpu.VMEM((n_chunks, pairs, rows, LANE), dt)
    kern = functools.partial(_wkv_kernel, pairs=pairs, unroll=unroll)
    return pl.pallas_call(
        kern,
        grid=(b, d // width, s // tb),
        in_specs=[tile] * 6 + [vec] * 5,
        out_specs=tile,
        out_shape=jax.ShapeDtypeStruct((b, s, d), BF16),
        scratch_shapes=[pltpu.VMEM((pairs, LANE, LANE), F32),
                        per_chunk(LANE, BF16), per_chunk(LANE, F32),
                        per_chunk(WKV_CHUNK, BF16), per_chunk(WKV_CHUNK, F32), per_chunk(WKV_CHUNK, F32)],
        compiler_params=_cparams(("parallel", "parallel", "arbitrary")),
        name="wkv7",
    )(r, k, v, wl, a, gate, *(t.reshape(1, -1) for t in (k_k, k_a, r_k, lnx_g, lnx_b)))


def kernel(x, ev_w_in, ev_lambda, ev_subln_g, ev_w_out, od_mu, od_w_rkv, od_w0, od_w1, od_w2, od_a0, od_a1, od_a2, od_g1, od_g2, od_k_k, od_k_a, od_r_k, od_lnx_g, od_lnx_b, od_w_out, ln_mix_g, ln_mix_b, ffn_w_up, ffn_conv_w, ffn_conv_b, ffn_w_down, ln_ffn_g, ln_ffn_b):
    b, s, d = x.shape
    m = b * s
    bf = lambda t: t.astype(BF16)

    qkv, gate = _in_proj(x.reshape(m, d), bf(ev_w_in[0]))
    qkv = qkv.reshape(b, s, EVEN_QKV_WIDTH)
    lambda_init = 0.8 - 0.6 * math.exp(-0.3 * 0)
    a_out = _diff_attention(qkv, ev_lambda[0], ev_subln_g[0], lambda_init)
    b_out = _retention(qkv, gate.reshape(b, s, RET_GATE_WIDTH))
    x1 = _even_out(a_out.reshape(m, -1), b_out.reshape(m, -1), x.reshape(m, d), bf(ev_w_out[0]),
                   ln_mix_g[0], ln_mix_b[0]).reshape(b, s, d)
    x2 = _ffn(x1, bf(ffn_w_up[0]), ffn_conv_w[0], ffn_conv_b[0], bf(ffn_w_down[0]), ln_ffn_g[0], ln_ffn_b[0])

    pad = GATE_LORA_PAD - GATE_LORA
    g1 = jnp.pad(bf(od_g1[0]), ((0, 0), (0, pad)))
    g2 = jnp.pad(bf(od_g2[0]), ((0, pad), (0, 0)))
    r, k, v, wl, a, g = _rwkv_proj(x2, od_mu[0], bf(od_w_rkv[0]), od_w0[0], bf(od_w1[0]), bf(od_w2[0]),
                                   od_a0[0], bf(od_a1[0]), bf(od_a2[0]), g1, g2)
    y = _wkv(r, k, v, wl, a, g, od_k_k[0], od_k_a[0], od_r_k[0], od_lnx_g[0], od_lnx_b[0])
    x3 = _odd_out(y.reshape(m, d), x2.reshape(m, d), bf(od_w_out[0]),
                  ln_mix_g[1], ln_mix_b[1]).reshape(b, s, d)
    x4 = _ffn(x3, bf(ffn_w_up[1]), ffn_conv_w[1], ffn_conv_b[1], bf(ffn_w_down[1]), ln_ffn_g[1], ln_ffn_b[1])
    return x4
```

```python
import functools
import math

import jax
import jax.numpy as jnp
from jax import lax
from jax.experimental import pallas as pl
from jax.experimental.pallas import tpu as pltpu

F32 = jnp.float32
BF16 = jnp.bfloat16

D_MODEL = 1024
DEPTH = 2
DN_ALPHA = (2.0 * DEPTH) ** 0.25
LN_EPS = 1e-5

DIFF_HEAD_DIM = 64
DIFF_HEADS = 4
RET_QK_DIM = 64
RET_V_DIM = 128
RET_HEADS = 4
RET_CHUNK = 128
EVEN_IN_WIDTH = 3072
EVEN_QKV_WIDTH = 2560
RET_GATE_WIDTH = 512

RWKV_HEAD_DIM = 64
RWKV_GN_EPS = 64e-5
GATE_LORA = 160
GATE_LORA_PAD = 256
WKV_CHUNK = 64
WKV_INV_BASE = 8
WKV_TIME_BLOCK = 1024

FFN_HIDDEN = 2816
FFN_CHUNK = 256
CONV_HALO = 16

LANE = 128
VMEM_LIMIT = 56 * 1024 * 1024


def _cparams(sem):
    return pltpu.CompilerParams(dimension_semantics=sem, vmem_limit_bytes=VMEM_LIMIT)


def _dot(a, b):
    return jnp.dot(a, b, preferred_element_type=F32)


def _dot_nt(a, b):
    return lax.dot_general(a, b, (((1,), (1,)), ((), ())), preferred_element_type=F32)


def _dot_tn(a, b):
    return lax.dot_general(a, b, (((0,), (0,)), ((), ())), preferred_element_type=F32)


def _layer_norm(z, g, b):
    mu = jnp.mean(z, axis=-1, keepdims=True)
    zc = z - mu
    var = jnp.mean(zc * zc, axis=-1, keepdims=True)
    return zc * lax.rsqrt(var + LN_EPS) * g + b


def _in_proj_kernel(x_ref, w_ref, qkv_ref, gate_ref):
    h = _dot(x_ref[...].astype(BF16), w_ref[...])
    qkv_ref[...] = h[:, :EVEN_QKV_WIDTH].astype(BF16)
    gate_ref[...] = h[:, EVEN_QKV_WIDTH:]


def _in_proj(x2d, w_bf16, tm=512):
    m = x2d.shape[0]
    return pl.pallas_call(
        _in_proj_kernel,
        grid=(m // tm,),
        in_specs=[pl.BlockSpec((tm, D_MODEL), lambda i: (i, 0)),
                  pl.BlockSpec((D_MODEL, EVEN_IN_WIDTH), lambda i: (0, 0))],
        out_specs=[pl.BlockSpec((tm, EVEN_QKV_WIDTH), lambda i: (i, 0)),
                   pl.BlockSpec((tm, RET_GATE_WIDTH), lambda i: (i, 0))],
        out_shape=[jax.ShapeDtypeStruct((m, EVEN_QKV_WIDTH), BF16),
                   jax.ShapeDtypeStruct((m, RET_GATE_WIDTH), F32)],
        compiler_params=_cparams(("parallel",)),
        name="in_proj",
    )(x2d, w_bf16)


def _diff_attn_kernel(lam_ref, g_ref, q_ref, k_ref, v_ref, o_ref, *, tq, lambda_init):
    h = pl.program_id(1)
    qi = pl.program_id(2)
    tk = tq
    slope = jnp.where(h == 0, 2.0 ** -2, jnp.where(h == 1, 2.0 ** -4,
                      jnp.where(h == 2, 2.0 ** -6, 2.0 ** -8))).astype(F32)
    lp = lam_ref[...]
    lam = (jnp.exp(jnp.sum(lp[0:1] * lp[1:2], axis=-1, keepdims=True))
           - jnp.exp(jnp.sum(lp[2:3] * lp[3:4], axis=-1, keepdims=True)) + lambda_init)

    lane = lax.broadcasted_iota(jnp.int32, (1, LANE), 1)
    q = q_ref[0] * jnp.asarray(DIFF_HEAD_DIM ** -0.5, BF16)
    zero = jnp.zeros_like(q)
    q_pos = jnp.broadcast_to(jnp.where(lane < 2, slope, 0.0).astype(BF16), (tq, LANE))
    q12 = jnp.concatenate([jnp.concatenate([jnp.where(lane < DIFF_HEAD_DIM, q, zero), q_pos], axis=1),
                           jnp.concatenate([jnp.where(lane >= DIFF_HEAD_DIM, q, zero), q_pos], axis=1)],
                          axis=0)
    key_off = lax.broadcasted_iota(jnp.int32, (tk, LANE), 0)
    k_pos = jnp.where(lane == 0, key_off & 255, jnp.where(lane == 1, key_off & ~255, 0)).astype(F32).astype(BF16)

    def scores_t(j):
        start = pl.multiple_of(j * tk, tk)
        k = jnp.concatenate([k_ref[0, pl.ds(start, tk), :], k_pos], axis=1)
        return _dot_nt(k, q12)

    def values(j):
        return v_ref[0, pl.ds(pl.multiple_of(j * tk, tk), tk), :]

    def update(carry, s_t, v, shift):
        m, l, acc_t = carry
        m_new = jnp.maximum(m, jnp.max(s_t, axis=0, keepdims=True) + shift)
        alpha = jnp.exp(m - m_new)
        p_t = jnp.exp(s_t - (m_new - shift))
        l = alpha * l + jnp.sum(p_t, axis=0, keepdims=True)
        acc_t = alpha * acc_t + _dot_tn(v, p_t.astype(BF16))
        return m_new, l, acc_t

    def body(j, carry):
        return update(carry, scores_t(j), values(j), ((j - qi) * tk).astype(F32) * slope)

    init = (jnp.full((1, 2 * tq), -1e30, F32), jnp.zeros((1, 2 * tq), F32), jnp.zeros((LANE, 2 * tq), F32))
    carry = lax.fori_loop(0, qi, body, init)
    key_i = lax.broadcasted_iota(jnp.int32, (tk, 2 * tq), 0)
    query_i = lax.broadcasted_iota(jnp.int32, (tk, 2 * tq), 1) % tq
    _, l, acc_t = update(carry, jnp.where(key_i <= query_i, scores_t(qi), -1e30), values(qi), 0.0)
    o_t = acc_t[:, :tq] / l[:, :tq] - lam * (acc_t[:, tq:] / l[:, tq:])
    o_t = o_t * lax.rsqrt(jnp.mean(o_t * o_t, axis=0, keepdims=True) + LN_EPS)
    o_ref[0] = (o_t.T * g_ref[...] * (1.0 - lambda_init)).astype(o_ref.dtype)


def _diff_attention(qkv, lam_p, subln_g, lambda_init, tq=512):
    b, s, _ = qkv.shape
    kern = functools.partial(_diff_attn_kernel, tq=tq, lambda_init=lambda_init)
    return pl.pallas_call(
        kern,
        grid=(b, DIFF_HEADS, s // tq),
        in_specs=[pl.BlockSpec((4, DIFF_HEAD_DIM), lambda bi, h, i: (0, 0)),
                  pl.BlockSpec((1, LANE), lambda bi, h, i: (0, 0)),
                  pl.BlockSpec((1, tq, LANE), lambda bi, h, i: (bi, i, h)),
                  pl.BlockSpec((1, s, LANE), lambda bi, h, i: (bi, 0, DIFF_HEADS + h)),
                  pl.BlockSpec((1, s, LANE), lambda bi, h, i: (bi, 0, 2 * DIFF_HEADS + h))],
        out_specs=pl.BlockSpec((1, tq, LANE), lambda bi, h, i: (bi, i, h)),
        out_shape=jax.ShapeDtypeStruct((b, s, DIFF_HEADS * LANE), BF16),
        compiler_params=_cparams(("parallel", "parallel", "arbitrary")),
        name="diff_attn",
    )(lam_p, subln_g.reshape(1, LANE), qkv, qkv, qkv)


def _retention_kernel(q_ref, k_ref, v_ref, g_ref, o_ref, state_ref, *, chunks_per_step):
    C = RET_CHUNK

    @pl.when(pl.program_id(1) == 0)
    def _():
        state_ref[...] = jnp.zeros_like(state_ref)

    lane = lax.broadcasted_iota(jnp.int32, (1, LANE), 1)
    row128 = lax.broadcasted_iota(jnp.int32, (LANE, 1), 0)
    ri = lax.broadcasted_iota(jnp.int32, (C, C), 0)
    ci = lax.broadcasted_iota(jnp.int32, (C, C), 1)
    rel = (ri - ci).astype(F32)
    idx = lax.broadcasted_iota(jnp.int32, (C, 1), 0).astype(F32)
    heads = range(RET_HEADS)
    log_gamma = [math.log1p(-(2.0 ** (-5.0 - h))) for h in heads]
    in_head = [(lane >= (h % 2) * RET_QK_DIM) & (lane < (h % 2 + 1) * RET_QK_DIM) for h in heads]
    in_rows = [(row128 >= (h % 2) * RET_QK_DIM) & (row128 < (h % 2 + 1) * RET_QK_DIM) for h in heads]
    decay = [jnp.where(rel >= 0, jnp.exp(lg * jnp.maximum(rel, 0.0)), 0.0) for lg in log_gamma]
    q_decay = [jnp.exp(lg * (idx + 1.0)) for lg in log_gamma]
    k_decay = [jnp.exp(lg * (C - 1.0 - idx)) for lg in log_gamma]
    v_cols = [slice(h * RET_V_DIM, (h + 1) * RET_V_DIM) for h in heads]

    def chunk_body(c, _):
        rows = pl.ds(pl.multiple_of(c * C, C), C)
        q_all = q_ref[0, rows, :].astype(F32)
        k_all = k_ref[0, rows, :].astype(F32) * (RET_QK_DIM ** -0.5)
        q = [jnp.where(in_head[h], q_all[:, (h // 2) * LANE:(h // 2 + 1) * LANE], 0.0) for h in heads]
        k = [k_all[:, (h // 2) * LANE:(h // 2 + 1) * LANE] for h in heads]
        v = [v_ref[0, rows, v_cols[h]] for h in heads]
        sc = [_dot_nt(q[h].astype(BF16), k[h].astype(BF16)) * decay[h] for h in heads]
        inner = [_dot(sc[h].astype(BF16), v[h]) for h in heads]
        state = [state_ref[h] for h in heads]
        cross = [_dot((q[h] * q_decay[h]).astype(BF16), state[h].astype(BF16)) for h in heads]
        upd = [_dot_tn((k[h] * k_decay[h]).astype(BF16), v[h]) for h in heads]
        for h in heads:
            state_ref[h] = state[h] * math.exp(log_gamma[h] * C) + jnp.where(in_rows[h], upd[h], 0.0)
        y = [inner[h] + cross[h] for h in heads]
        mu = [jnp.mean(y_, axis=-1, keepdims=True) for y_ in y]
        yc = [y_ - m_ for y_, m_ in zip(y, mu)]
        var = [jnp.mean(c_ * c_, axis=-1, keepdims=True) for c_ in yc]
        for h in heads:
            g = g_ref[0, rows, v_cols[h]]
            o_ref[0, rows, v_cols[h]] = (g * jax.nn.sigmoid(g)
                                         * (yc[h] * lax.rsqrt(var[h] + LN_EPS))).astype(o_ref.dtype)
        return 0

    lax.fori_loop(0, chunks_per_step, chunk_body, 0)


def _retention(qkv, gate, chunks_per_step=4):
    b, s, _ = qkv.shape
    C = RET_CHUNK * chunks_per_step
    qk_w = RET_HEADS * RET_QK_DIM
    v_w = RET_HEADS * RET_V_DIM
    return pl.pallas_call(
        functools.partial(_retention_kernel, chunks_per_step=chunks_per_step),
        grid=(b, s // C),
        in_specs=[pl.BlockSpec((1, C, qk_w), lambda bi, c: (bi, c, 1536 // qk_w)),
                  pl.BlockSpec((1, C, qk_w), lambda bi, c: (bi, c, 1792 // qk_w)),
                  pl.BlockSpec((1, C, v_w), lambda bi, c: (bi, c, 2048 // v_w)),
                  pl.BlockSpec((1, C, v_w), lambda bi, c: (bi, c, 0))],
        out_specs=pl.BlockSpec((1, C, v_w), lambda bi, c: (bi, c, 0)),
        out_shape=jax.ShapeDtypeStruct((b, s, v_w), BF16),
        scratch_shapes=[pltpu.VMEM((RET_HEADS, LANE, RET_V_DIM), F32)],
        compiler_params=_cparams(("parallel", "arbitrary")),
        name="retention",
    )(qkv, qkv, qkv, gate)


def _layer_tail_kernel(*refs, tm, n_pieces):
    piece_refs = [refs[3 * p:3 * p + 3] for p in range(n_pieces)]
    (x_ref, xhalo_ref, mg_ref, mb_ref, wu_ref, cw_ref, cb_ref, wd_ref, fg_ref, fb_ref,
     o_ref, act_ref) = refs[3 * n_pieces:]
    i = pl.program_id(1)
    mix = None
    for tile_ref, halo_ref, w_ref in piece_refs:
        part = _dot(jnp.concatenate([halo_ref[0], tile_ref[0]], axis=0), w_ref[...])
        mix = part if mix is None else mix + part
    xe = jnp.concatenate([xhalo_ref[0], x_ref[0]], axis=0)
    x1e = _layer_norm(DN_ALPHA * xe + mix, mg_ref[...], mb_ref[...])
    x1 = x1e[CONV_HALO:]
    xe = jnp.concatenate([jnp.where(i > 0, x1e[:CONV_HALO], 0.0), x1], axis=0).astype(BF16)
    n_chunks = FFN_HIDDEN // FFN_CHUNK
    for c in range(n_chunks):
        lo = c * FFN_CHUNK
        ue = _dot(xe, wu_ref[:, lo:lo + FFN_CHUNK])
        gate = _dot(xe[CONV_HALO:], wu_ref[:, FFN_HIDDEN + lo:FFN_HIDDEN + lo + FFN_CHUNK])
        cw = cw_ref[:, lo:lo + FFN_CHUNK]
        conv = (cb_ref[:, lo:lo + FFN_CHUNK]
                + ue[CONV_HALO - 2:CONV_HALO - 2 + tm] * cw[0:1]
                + ue[CONV_HALO - 1:CONV_HALO - 1 + tm] * cw[1:2]
                + ue[CONV_HALO:] * cw[2:3])
        act = 0.5 * conv * (1.0 + lax.erf(conv * (2.0 ** -0.5)))
        act_ref[:, lo:lo + FFN_CHUNK] = (act * gate).astype(BF16)
    ffn = _dot(act_ref[...], wd_ref[...])
    o_ref[0] = _layer_norm(DN_ALPHA * x1 + ffn, fg_ref[...], fb_ref[...])


def _layer_tail(pieces, x, w_out_bf16, ln_mix_g, ln_mix_b, w_up_bf16, conv_w, conv_b, w_down_bf16,
                ln_ffn_g, ln_ffn_b, tm=512):
    b, s, _ = x.shape
    assert len({p.shape[-1] for p in pieces}) == 1 and sum(p.shape[-1] for p in pieces) == D_MODEL
    kern = functools.partial(_layer_tail_kernel, tm=tm, n_pieces=len(pieces))
    fixed = lambda bi, i: (0, 0)
    tile = lambda bi, i: (bi, i, 0)
    halo = lambda bi, i: (bi, jnp.maximum(i * (tm // CONV_HALO) - 1, 0), 0)
    single = pl.Buffered(1)
    in_specs, args = [], []
    for row_block, piece in enumerate(pieces):
        width = piece.shape[-1]
        in_specs += [pl.BlockSpec((1, tm, width), tile), pl.BlockSpec((1, CONV_HALO, width), halo),
                     pl.BlockSpec((width, D_MODEL), functools.partial(lambda bi, i, r: (r, 0), r=row_block),
                                  pipeline_mode=single)]
        args += [piece, piece, w_out_bf16]
    in_specs += [pl.BlockSpec((1, tm, D_MODEL), tile), pl.BlockSpec((1, CONV_HALO, D_MODEL), halo),
                 pl.BlockSpec((1, D_MODEL), fixed), pl.BlockSpec((1, D_MODEL), fixed),
                 pl.BlockSpec((D_MODEL, 2 * FFN_HIDDEN), fixed, pipeline_mode=single),
                 pl.BlockSpec((3, FFN_HIDDEN), fixed),
                 pl.BlockSpec((1, FFN_HIDDEN), fixed),
                 pl.BlockSpec((FFN_HIDDEN, D_MODEL), fixed, pipeline_mode=single),
                 pl.BlockSpec((1, D_MODEL), fixed), pl.BlockSpec((1, D_MODEL), fixed)]
    args += [x, x, ln_mix_g.reshape(1, -1), ln_mix_b.reshape(1, -1), w_up_bf16, conv_w, conv_b.reshape(1, -1),
             w_down_bf16, ln_ffn_g.reshape(1, -1), ln_ffn_b.reshape(1, -1)]
    return pl.pallas_call(
        kern,
        grid=(b, s // tm),
        in_specs=in_specs,
        out_specs=pl.BlockSpec((1, tm, D_MODEL), tile),
        out_shape=jax.ShapeDtypeStruct((b, s, D_MODEL), F32),
        scratch_shapes=[pltpu.VMEM((tm, FFN_HIDDEN), BF16)],
        compiler_params=_cparams(("parallel", "arbitrary")),
        name="layer_tail",
    )(*args)


def _rwkv_proj_kernel(x_ref, halo_ref, mu_ref, wrkv_ref, w0_ref, w1_ref, w2_ref, a0_ref, a1_ref, a2_ref,
                      g1_ref, g2_ref, r_ref, k_ref, v_ref, wl_ref, a_ref, g_ref, *, tm):
    i = pl.program_id(1)
    x = x_ref[0]
    prev_row = jnp.where(i > 0, halo_ref[0, 7:8, :], 0.0)
    row = lax.broadcasted_iota(jnp.int32, (tm, 1), 0)
    x_prev = jnp.where(row == 0, prev_row, pltpu.roll(x, 1, 0))
    xx = x_prev - x

    def mix(n):
        return (x + xx * mu_ref[n:n + 1, :]).astype(BF16)

    r_ref[0] = _dot(mix(0), wrkv_ref[0]).astype(r_ref.dtype)
    k_ref[0] = _dot(mix(1), wrkv_ref[1]).astype(k_ref.dtype)
    v_ref[0] = _dot(mix(2), wrkv_ref[2]).astype(v_ref.dtype)
    lw = jnp.tanh(_dot(mix(3), w1_ref[...])).astype(BF16)
    u = w0_ref[...] + _dot(lw, w2_ref[...])
    wl_ref[0] = -math.exp(-0.5) * jax.nn.sigmoid(u)
    la = _dot(mix(4), a1_ref[...]).astype(BF16)
    a_ref[0] = jax.nn.sigmoid(a0_ref[...] + _dot(la, a2_ref[...])).astype(a_ref.dtype)
    lg = jax.nn.sigmoid(_dot(mix(5), g1_ref[...])).astype(BF16)
    g_ref[0] = _dot(lg, g2_ref[...]).astype(g_ref.dtype)


def _rwkv_proj(x, mu, w_rkv, w0, w1, w2, a0, a1, a2, g1, g2, tm=512):
    b, s, d = x.shape
    kern = functools.partial(_rwkv_proj_kernel, tm=tm)
    fixed2 = lambda bi, i: (0, 0)
    tile = pl.BlockSpec((1, tm, d), lambda bi, i: (bi, i, 0))
    lora = w1.shape[1]
    out = jax.ShapeDtypeStruct((b, s, d), F32)
    half = jax.ShapeDtypeStruct((b, s, d), BF16)
    return pl.pallas_call(
        kern,
        grid=(b, s // tm),
        in_specs=[tile,
                  pl.BlockSpec((1, 8, d), lambda bi, i: (bi, jnp.maximum(i * (tm // 8) - 1, 0), 0)),
                  pl.BlockSpec((6, d), fixed2),
                  pl.BlockSpec((3, d, d), lambda bi, i: (0, 0, 0)),
                  pl.BlockSpec((1, d), fixed2), pl.BlockSpec((d, lora), fixed2), pl.BlockSpec((lora, d), fixed2),
                  pl.BlockSpec((1, d), fixed2), pl.BlockSpec((d, lora), fixed2), pl.BlockSpec((lora, d), fixed2),
                  pl.BlockSpec((d, GATE_LORA_PAD), fixed2), pl.BlockSpec((GATE_LORA_PAD, d), fixed2)],
        out_specs=[tile] * 6,
        out_shape=[half, half, half, out, half, half],
        compiler_params=_cparams(("parallel", "arbitrary")),
        name="rwkv_proj",
    )(x, x, mu, w_rkv, w0.reshape(1, -1), w1, w2, a0.reshape(1, -1), a1, a2, g1, g2)


def _each(f, *lists):
    return [f(*xs) for xs in zip(*lists)]


def _wkv_pair_consts():
    T = WKV_CHUNK
    lane = lax.broadcasted_iota(jnp.int32, (1, LANE), 1)
    row = lax.broadcasted_iota(jnp.int32, (T, LANE), 0)
    col = lax.broadcasted_iota(jnp.int32, (T, LANE), 1) % T
    r2 = lax.broadcasted_iota(jnp.int32, (LANE, LANE), 0)
    c2 = lax.broadcasted_iota(jnp.int32, (LANE, LANE), 1)
    tri_r = lax.broadcasted_iota(jnp.int32, (T, T), 0)
    tri_c = lax.broadcasted_iota(jnp.int32, (T, T), 1)
    same_head = (r2 // T) == (c2 // T)
    merges = []
    size = 2 * WKV_INV_BASE
    while size <= T:
        merges.append(((row // size) == (col // size)) & ((row // (size // 2)) != (col // (size // 2))))
        size *= 2
    return dict(
        even=lane < T,
        strict=col < row, incl=col <= row, eye=col == row,
        in_base=(row // WKV_INV_BASE) == (col // WKV_INV_BASE),
        merges=merges,
        same_head=same_head, eye128=r2 == c2,
        tri=jnp.where(tri_c <= tri_r, 1.0, 0.0).astype(BF16),
    )


def _head_sum(x, even):
    s_even = jnp.sum(jnp.where(even, x, 0.0), axis=-1, keepdims=True)
    s_odd = jnp.sum(jnp.where(even, 0.0, x), axis=-1, keepdims=True)
    return jnp.where(even, s_even, s_odd)


def _wkv_prepare(r, k, v, wl, lr, kk_scale, ka, rk, cst):
    T = WKV_CHUNK
    bf = lambda t: t.astype(BF16)
    even, strict, incl, eye = cst["even"], cst["strict"], cst["incl"], cst["eye"]
    tri_incl_bf16 = cst["tri"]

    def stack2(x):
        zero = jnp.zeros_like(x)
        return jnp.concatenate([jnp.where(even, x, zero), jnp.where(even, zero, x)], axis=0)

    def pair_dot(packed, x):
        return _dot(packed, stack2(x))

    def pair_dot2(packed, x1, x2):
        return _dot(packed, jnp.concatenate([stack2(x1), stack2(x2)], axis=1))

    kk = _each(lambda k_, s_: k_ * s_, k, kk_scale)
    ss = _each(lambda t: _head_sum(t * t, even), kk)
    kk = _each(lambda t, s_: t / jnp.maximum(jnp.sqrt(s_), 1e-12), kk, ss)
    k = _each(lambda k_, lr_, ka_: k_ * (1.0 + (lr_ - 1.0) * ka_), k, lr, ka)
    b_vec = _each(lambda kk_, lr_: kk_ * lr_, kk, lr)

    w_hi = _each(bf, wl)
    w_r1 = _each(lambda w_, h_: w_ - h_.astype(F32), wl, w_hi)
    w_mid = _each(bf, w_r1)
    w_lo = _each(lambda r1, m_: (r1 - m_.astype(F32)).astype(BF16), w_r1, w_mid)
    cl = _each(lambda h_, m_, l_: _dot(tri_incl_bf16, h_) + (_dot(tri_incl_bf16, m_) + _dot(tri_incl_bf16, l_)),
               w_hi, w_mid, w_lo)
    cl_last = _each(lambda c_: c_[T - 1:T, :], cl)
    g_inv = _each(lambda c_: jnp.exp(-c_), cl)
    a_hat = _each(lambda kk_, c_, w_: -kk_ * jnp.exp(c_ - w_), kk, cl, wl)
    b_hat = _each(lambda b_, g_: bf(b_ * g_), b_vec, g_inv)
    k_hat = _each(lambda k_, g_: bf(k_ * g_), k, g_inv)
    r_hat = _each(lambda r_, c_: r_ * jnp.exp(c_), r, cl)
    to_end = _each(lambda l_, c_: jnp.exp(l_ - c_), cl_last, cl)
    b_end = _each(lambda b_, e_: bf(b_ * e_), b_vec, to_end)
    k_end = _each(lambda k_, e_: bf(k_ * e_), k, to_end)

    ar = _each(lambda a_, r_: bf(jnp.concatenate([a_, r_], axis=0)), a_hat, r_hat)
    m_bk = _each(lambda ar_, b_, k_: _dot_nt(ar_, jnp.concatenate([stack2(b_), stack2(k_)], axis=0)),
                 ar, b_hat, k_hat)
    m_b = _each(lambda m_: m_[:, :LANE], m_bk)
    m_k = _each(lambda m_: m_[:, LANE:], m_bk)
    a_ab = _each(lambda m_: jnp.where(strict, m_[:T], 0.0), m_b)
    a_rb = _each(lambda m_: bf(jnp.where(incl, m_[T:], 0.0)), m_b)
    a_ak = _each(lambda m_: bf(jnp.where(strict, m_[:T], 0.0)), m_k)
    a_rk = _each(lambda m_: bf(jnp.where(incl, m_[T:], 0.0)), m_k)

    p = _each(lambda a_: jnp.where(cst["in_base"], a_, 0.0), a_ab)
    minv = _each(lambda p_: jnp.where(eye, 1.0, 0.0) + p_, p)
    levels = int(math.log2(WKV_INV_BASE)) - 1
    p = _each(lambda p_: pair_dot(bf(p_), bf(p_)), p)
    for level in range(levels):
        if level + 1 < levels:
            both = _each(lambda p_, m_: pair_dot2(bf(p_), bf(m_), bf(p_)), p, minv)
            minv = _each(lambda m_, x_: m_ + x_[:, :LANE], minv, both)
            p = _each(lambda x_: x_[:, LANE:], both)
        else:
            minv = _each(lambda m_, p_: m_ + pair_dot(bf(p_), bf(m_)), minv, p)
    for off in cst["merges"]:
        a_off = _each(lambda a_: bf(jnp.where(off, a_, 0.0)), a_ab)
        minv = _each(lambda m_, a_: m_ + pair_dot(bf(m_), bf(pair_dot(a_, bf(m_)))), minv, a_off)

    v_bf16 = _each(bf, v)
    minv_bf16 = _each(bf, minv)
    ak_rk_v = _each(lambda ak_, rk_, v_: pair_dot(jnp.concatenate([ak_, rk_], axis=0), v_), a_ak, a_rk, v_bf16)
    akv = _each(lambda x_: bf(x_[:T]), ak_rk_v)
    at_w = _each(lambda m_, a_, x_: bf(pair_dot2(m_, a_[:T], x_)), minv_bf16, ar, akv)
    a_t = _each(lambda x_: x_[:, :LANE], at_w)
    w_bf16 = _each(lambda x_: x_[:, LANE:], at_w)
    rb_at_w = _each(lambda a_, t_, w_: pair_dot2(a_, t_, w_), a_rb, a_t, w_bf16)
    r_t = _each(lambda r_, x_: bf(r_ + x_[:, :LANE]), r_hat, rb_at_w)
    y0 = _each(lambda x_, y_: x_[:, LANE:] + y_[T:], rb_at_w, ak_rk_v)
    same_head, eye128 = cst["same_head"], cst["eye128"]
    atw_b = _each(lambda x_, b_: _dot_tn(x_, b_), at_w, b_end)
    trans = _each(lambda l_, x_: bf(jnp.where(eye128, jnp.exp(l_), 0.0) + jnp.where(same_head, x_[:LANE], 0.0)),
                  cl_last, atw_b)
    s_add = _each(lambda x_, v_, k_: jnp.where(same_head, x_[LANE:] + _dot_tn(v_, k_), 0.0),
                  atw_b, v_bf16, k_end)
    bonus = _each(lambda r_, k_, rk_, v_: _head_sum(r_ * k_ * rk_, even) * v_, r, k, rk, v)
    return trans, s_add, r_t, y0, bonus


def _wkv_kernel(r_ref, k_ref, v_ref, wl_ref, a_ref, gate_ref, kk_ref, ka_ref, rk_ref, lg_ref, lb_ref, o_ref,
                state_ref, trans_ref, sadd_ref, rt_ref, y0_ref, bonus_ref, *, pairs, unroll):
    @pl.when(pl.program_id(2) == 0)
    def _():
        state_ref[...] = jnp.zeros_like(state_ref)

    T = WKV_CHUNK
    n_chunks = WKV_TIME_BLOCK // T
    cst = _wkv_pair_consts()
    even = cst["even"]
    pair_cols = [slice(pp * LANE, (pp + 1) * LANE) for pp in range(pairs)]

    def prepare(group):
        chunk_ids = [group * unroll + cc for cc in range(unroll)]
        problems = [(cc, pp) for cc in range(unroll) for pp in range(pairs)]

        def tiles(ref):
            return [ref[0, pl.ds(pl.multiple_of(chunk_ids[cc] * T, T), T), pair_cols[pp]].astype(F32)
                    for cc, pp in problems]

        def vecs(ref):
            return [ref[:, pair_cols[pp]] for _, pp in problems]

        outs = _wkv_prepare(tiles(r_ref), tiles(k_ref), tiles(v_ref), tiles(wl_ref), tiles(a_ref),
                            vecs(kk_ref), vecs(ka_ref), vecs(rk_ref), cst)
        for ref, vals in zip((trans_ref, sadd_ref, rt_ref, y0_ref, bonus_ref), outs):
            for (cc, pp), val in zip(problems, vals):
                ref[chunk_ids[cc], pp] = val

    def scan(group):
        for cc in range(unroll):
            c = group * unroll + cc
            rows = pl.ds(pl.multiple_of(c * T, T), T)
            state_bf16 = [state_ref[pp].astype(BF16) for pp in range(pairs)]
            y = [_dot_nt(rt_ref[c, pp], state_bf16[pp]) + y0_ref[c, pp] for pp in range(pairs)]
            new_state = [_dot(state_bf16[pp], trans_ref[c, pp]) + sadd_ref[c, pp] for pp in range(pairs)]
            for pp in range(pairs):
                state_ref[pp] = new_state[pp]
            mu = _each(lambda y_: _head_sum(y_, even) * (1.0 / RWKV_HEAD_DIM), y)
            yc = _each(lambda y_, m_: y_ - m_, y, mu)
            var = _each(lambda c_: _head_sum(c_ * c_, even) * (1.0 / RWKV_HEAD_DIM), yc)
            for pp in range(pairs):
                out = (yc[pp] * lax.rsqrt(var[pp] + RWKV_GN_EPS) * lg_ref[:, pair_cols[pp]]
                       + lb_ref[:, pair_cols[pp]] + bonus_ref[c, pp])
                gated = out * gate_ref[0, rows, pair_cols[pp]].astype(F32)
                o_ref[0, rows, pair_cols[pp]] = gated.astype(o_ref.dtype)

    n_groups = n_chunks // unroll
    prepare(0)

    def pipelined(group, _):
        scan(group - 1)
        prepare(group)
        return 0

    lax.fori_loop(1, n_groups, pipelined, 0)
    scan(n_groups - 1)


def _wkv(r, k, v, wl, a, gate, k_k, k_a, r_k, lnx_g, lnx_b, pairs=2, unroll=8):
    b, s, d = r.shape
    tb = WKV_TIME_BLOCK
    width = pairs * LANE
    n_chunks = tb // WKV_CHUNK
    tile = pl.BlockSpec((1, tb, width), lambda bi, h, t: (bi, t, h))
    vec = pl.BlockSpec((1, width), lambda bi, h, t: (0, h))
    per_chunk = lambda rows, dt: pltpu.VMEM((n_chunks, pairs, rows, LANE), dt)
    kern = functools.partial(_wkv_kernel, pairs=pairs, unroll=unroll)
    return pl.pallas_call(
        kern,
        grid=(b, d // width, s // tb),
        in_specs=[tile] * 6 + [vec] * 5,
        out_specs=tile,
        out_shape=jax.ShapeDtypeStruct((b, s, d), BF16),
        scratch_shapes=[pltpu.VMEM((pairs, LANE, LANE), F32),
                        per_chunk(LANE, BF16), per_chunk(LANE, F32),
                        per_chunk(WKV_CHUNK, BF16), per_chunk(WKV_CHUNK, F32), per_chunk(WKV_CHUNK, F32)],
        compiler_params=_cparams(("parallel", "parallel", "arbitrary")),
        name="wkv7",
    )(r, k, v, wl, a, gate, *(t.reshape(1, -1) for t in (k_k, k_a, r_k, lnx_g, lnx_b)))


def kernel(x, ev_w_in, ev_lambda, ev_subln_g, ev_w_out, od_mu, od_w_rkv, od_w0, od_w1, od_w2, od_a0, od_a1, od_a2, od_g1, od_g2, od_k_k, od_k_a, od_r_k, od_lnx_g, od_lnx_b, od_w_out, ln_mix_g, ln_mix_b, ffn_w_up, ffn_conv_w, ffn_conv_b, ffn_w_down, ln_ffn_g, ln_ffn_b):
    b, s, d = x.shape
    m = b * s
    bf = lambda t: t.astype(BF16)

    qkv, gate = _in_proj(x.reshape(m, d), bf(ev_w_in[0]))
    qkv = qkv.reshape(b, s, EVEN_QKV_WIDTH)
    lambda_init = 0.8 - 0.6 * math.exp(-0.3 * 0)
    a_out = _diff_attention(qkv, ev_lambda[0], ev_subln_g[0], lambda_init)
    b_out = _retention(qkv, gate.reshape(b, s, RET_GATE_WIDTH))
    x2 = _layer_tail([a_out, b_out], x, bf(ev_w_out[0]), ln_mix_g[0], ln_mix_b[0], bf(ffn_w_up[0]),
                     ffn_conv_w[0], ffn_conv_b[0], bf(ffn_w_down[0]), ln_ffn_g[0], ln_ffn_b[0])

    pad = GATE_LORA_PAD - GATE_LORA
    g1 = jnp.pad(bf(od_g1[0]), ((0, 0), (0, pad)))
    g2 = jnp.pad(bf(od_g2[0]), ((0, pad), (0, 0)))
    r, k, v, wl, a, g = _rwkv_proj(x2, od_mu[0], bf(od_w_rkv[0]), od_w0[0], bf(od_w1[0]), bf(od_w2[0]),
                                   od_a0[0], bf(od_a1[0]), bf(od_a2[0]), g1, g2)
    y = _wkv(r, k, v, wl, a, g, od_k_k[0], od_k_a[0], od_r_k[0], od_lnx_g[0], od_lnx_b[0])
    x4 = _layer_tail([y], x2, bf(od_w_out[0]), ln_mix_g[1], ln_mix_b[1], bf(ffn_w_up[1]),
                     ffn_conv_w[1], ffn_conv_b[1], bf(ffn_w_down[1]), ln_ffn_g[1], ln_ffn_b[1])
    return x4
```

```python
import functools
import math

import jax
import jax.numpy as jnp
from jax import lax
from jax.experimental import pallas as pl
from jax.experimental.pallas import tpu as pltpu

F32 = jnp.float32
BF16 = jnp.bfloat16

D_MODEL = 1024
DEPTH = 2
DN_ALPHA = (2.0 * DEPTH) ** 0.25
LN_EPS = 1e-5

DIFF_HEAD_DIM = 64
DIFF_HEADS = 4
RET_QK_DIM = 64
RET_V_DIM = 128
RET_HEADS = 4
RET_CHUNK = 128
EVEN_IN_WIDTH = 3072
EVEN_QKV_WIDTH = 2560
RET_GATE_WIDTH = 512

RWKV_HEAD_DIM = 64
RWKV_GN_EPS = 64e-5
GATE_LORA = 160
GATE_LORA_PAD = 256
WKV_CHUNK = 64
WKV_INV_BASE = 8
WKV_TIME_BLOCK = 1024

FFN_HIDDEN = 2816
FFN_CHUNK = 256
CONV_HALO = 16

LANE = 128
VMEM_LIMIT = 56 * 1024 * 1024


def _cparams(sem):
    return pltpu.CompilerParams(dimension_semantics=sem, vmem_limit_bytes=VMEM_LIMIT)


def _dot(a, b):
    return jnp.dot(a, b, preferred_element_type=F32)


def _dot_nt(a, b):
    return lax.dot_general(a, b, (((1,), (1,)), ((), ())), preferred_element_type=F32)


def _dot_tn(a, b):
    return lax.dot_general(a, b, (((0,), (0,)), ((), ())), preferred_element_type=F32)


def _layer_norm(z, g, b):
    mu = jnp.mean(z, axis=-1, keepdims=True)
    zc = z - mu
    var = jnp.mean(zc * zc, axis=-1, keepdims=True)
    return zc * lax.rsqrt(var + LN_EPS) * g + b


def _in_proj_kernel(x_ref, w_ref, qkv_ref, gate_ref):
    h = _dot(x_ref[...].astype(BF16), w_ref[...])
    qkv_ref[...] = h[:, :EVEN_QKV_WIDTH].astype(BF16)
    gate_ref[...] = h[:, EVEN_QKV_WIDTH:]


def _in_proj(x2d, w_bf16, tm=512):
    m = x2d.shape[0]
    return pl.pallas_call(
        _in_proj_kernel,
        grid=(m // tm,),
        in_specs=[pl.BlockSpec((tm, D_MODEL), lambda i: (i, 0)),
                  pl.BlockSpec((D_MODEL, EVEN_IN_WIDTH), lambda i: (0, 0))],
        out_specs=[pl.BlockSpec((tm, EVEN_QKV_WIDTH), lambda i: (i, 0)),
                   pl.BlockSpec((tm, RET_GATE_WIDTH), lambda i: (i, 0))],
        out_shape=[jax.ShapeDtypeStruct((m, EVEN_QKV_WIDTH), BF16),
                   jax.ShapeDtypeStruct((m, RET_GATE_WIDTH), F32)],
        compiler_params=_cparams(("parallel",)),
        name="in_proj",
    )(x2d, w_bf16)


def _diff_attn_kernel(lam_ref, g_ref, q_ref, k_ref, v_ref, o_ref, sa_ref, sb_ref, m_ref, l_ref, acc_ref,
                      *, tq, lambda_init):
    h = pl.program_id(1)
    qi = pl.program_id(2)
    tk = tq
    slope = jnp.where(h == 0, 2.0 ** -2, jnp.where(h == 1, 2.0 ** -4,
                      jnp.where(h == 2, 2.0 ** -6, 2.0 ** -8))).astype(F32)
    lp = lam_ref[...]
    lam = (jnp.exp(jnp.sum(lp[0:1] * lp[1:2], axis=-1, keepdims=True))
           - jnp.exp(jnp.sum(lp[2:3] * lp[3:4], axis=-1, keepdims=True)) + lambda_init)

    lane = lax.broadcasted_iota(jnp.int32, (1, LANE), 1)
    q = q_ref[0] * jnp.asarray(DIFF_HEAD_DIM ** -0.5, BF16)
    zero = jnp.zeros_like(q)
    q_pos = jnp.broadcast_to(jnp.where(lane < 2, slope, 0.0).astype(BF16), (tq, LANE))
    q12 = jnp.concatenate([jnp.concatenate([jnp.where(lane < DIFF_HEAD_DIM, q, zero), q_pos], axis=1),
                           jnp.concatenate([jnp.where(lane >= DIFF_HEAD_DIM, q, zero), q_pos], axis=1)],
                          axis=0)
    key_off = lax.broadcasted_iota(jnp.int32, (tk, LANE), 0)
    k_pos = jnp.where(lane == 0, key_off & 255, jnp.where(lane == 1, key_off & ~255, 0)).astype(F32).astype(BF16)

    def scores_t(j, dst_ref):
        start = pl.multiple_of(j * tk, tk)
        k = jnp.concatenate([k_ref[0, pl.ds(start, tk), :], k_pos], axis=1)
        dst_ref[...] = _dot_nt(k, q12)

    def update(src_ref, j, diagonal):
        s_t = src_ref[...]
        if diagonal:
            key_i = lax.broadcasted_iota(jnp.int32, (tk, 1), 0)
            query_i = lax.broadcasted_iota(jnp.int32, (1, 2 * tq), 1) % tq
            s_t = jnp.where(key_i <= query_i, s_t, -1e30)
        shift = ((j - qi) * tk).astype(F32) * slope
        v = v_ref[0, pl.ds(pl.multiple_of(j * tk, tk), tk), :]
        m = m_ref[...]
        m_new = jnp.maximum(m, jnp.max(s_t, axis=0, keepdims=True) + shift)
        alpha = jnp.exp(m - m_new)
        p_t = jnp.exp(s_t - (m_new - shift))
        m_ref[...] = m_new
        l_ref[...] = alpha * l_ref[...] + jnp.sum(p_t, axis=0, keepdims=True)
        acc_ref[...] = alpha * acc_ref[...] + _dot_tn(v, p_t.astype(BF16))

    m_ref[...] = jnp.full(m_ref.shape, -1e30, F32)
    l_ref[...] = jnp.zeros(l_ref.shape, F32)
    acc_ref[...] = jnp.zeros(acc_ref.shape, F32)

    scores_t(0, sa_ref)

    def pair_body(jj, _):
        j0 = 2 * jj
        scores_t(j0 + 1, sb_ref)
        update(sa_ref, j0, False)
        scores_t(j0 + 2, sa_ref)
        update(sb_ref, j0 + 1, False)
        return 0

    lax.fori_loop(0, qi // 2, pair_body, 0)

    @pl.when(qi % 2 == 1)
    def _():
        scores_t(qi, sb_ref)
        update(sa_ref, qi - 1, False)
        update(sb_ref, qi, True)

    @pl.when(qi % 2 == 0)
    def _():
        update(sa_ref, qi, True)

    l = l_ref[...]
    acc_t = acc_ref[...]
    o_t = acc_t[:, :tq] / l[:, :tq] - lam * (acc_t[:, tq:] / l[:, tq:])
    o_t = o_t * lax.rsqrt(jnp.mean(o_t * o_t, axis=0, keepdims=True) + LN_EPS)
    o_ref[0] = (o_t.T * g_ref[...] * (1.0 - lambda_init)).astype(o_ref.dtype)


def _diff_attention(qkv, lam_p, subln_g, lambda_init, tq=512):
    b, s, _ = qkv.shape
    kern = functools.partial(_diff_attn_kernel, tq=tq, lambda_init=lambda_init)
    return pl.pallas_call(
        kern,
        grid=(b, DIFF_HEADS, s // tq),
        in_specs=[pl.BlockSpec((4, DIFF_HEAD_DIM), lambda bi, h, i: (0, 0)),
                  pl.BlockSpec((1, LANE), lambda bi, h, i: (0, 0)),
                  pl.BlockSpec((1, tq, LANE), lambda bi, h, i: (bi, i, h)),
                  pl.BlockSpec((1, s, LANE), lambda bi, h, i: (bi, 0, DIFF_HEADS + h)),
                  pl.BlockSpec((1, s, LANE), lambda bi, h, i: (bi, 0, 2 * DIFF_HEADS + h))],
        out_specs=pl.BlockSpec((1, tq, LANE), lambda bi, h, i: (bi, i, h)),
        out_shape=jax.ShapeDtypeStruct((b, s, DIFF_HEADS * LANE), BF16),
        scratch_shapes=[pltpu.VMEM((tq, 2 * tq), F32), pltpu.VMEM((tq, 2 * tq), F32),
                        pltpu.VMEM((1, 2 * tq), F32), pltpu.VMEM((1, 2 * tq), F32),
                        pltpu.VMEM((LANE, 2 * tq), F32)],
        compiler_params=_cparams(("parallel", "parallel", "arbitrary")),
        name="diff_attn",
    )(lam_p, subln_g.reshape(1, LANE), qkv, qkv, qkv)


def _retention_kernel(q_ref, k_ref, v_ref, g_ref, o_ref, state_ref, *, chunks_per_step):
    C = RET_CHUNK

    @pl.when(pl.program_id(1) == 0)
    def _():
        state_ref[...] = jnp.zeros_like(state_ref)

    lane = lax.broadcasted_iota(jnp.int32, (1, LANE), 1)
    row128 = lax.broadcasted_iota(jnp.int32, (LANE, 1), 0)
    ri = lax.broadcasted_iota(jnp.int32, (C, C), 0)
    ci = lax.broadcasted_iota(jnp.int32, (C, C), 1)
    rel = (ri - ci).astype(F32)
    idx = lax.broadcasted_iota(jnp.int32, (C, 1), 0).astype(F32)
    heads = range(RET_HEADS)
    log_gamma = [math.log1p(-(2.0 ** (-5.0 - h))) for h in heads]
    in_head = [(lane >= (h % 2) * RET_QK_DIM) & (lane < (h % 2 + 1) * RET_QK_DIM) for h in heads]
    in_rows = [(row128 >= (h % 2) * RET_QK_DIM) & (row128 < (h % 2 + 1) * RET_QK_DIM) for h in heads]
    decay = [jnp.where(rel >= 0, jnp.exp(lg * jnp.maximum(rel, 0.0)), 0.0) for lg in log_gamma]
    q_decay = [jnp.exp(lg * (idx + 1.0)) for lg in log_gamma]
    k_decay = [jnp.exp(lg * (C - 1.0 - idx)) for lg in log_gamma]
    v_cols = [slice(h * RET_V_DIM, (h + 1) * RET_V_DIM) for h in heads]

    def chunk_body(c, _):
        rows = pl.ds(pl.multiple_of(c * C, C), C)
        q_all = q_ref[0, rows, :].astype(F32)
        k_all = k_ref[0, rows, :].astype(F32) * (RET_QK_DIM ** -0.5)
        q = [jnp.where(in_head[h], q_all[:, (h // 2) * LANE:(h // 2 + 1) * LANE], 0.0) for h in heads]
        k = [k_all[:, (h // 2) * LANE:(h // 2 + 1) * LANE] for h in heads]
        v = [v_ref[0, rows, v_cols[h]] for h in heads]
        sc = [_dot_nt(q[h].astype(BF16), k[h].astype(BF16)) * decay[h] for h in heads]
        inner = [_dot(sc[h].astype(BF16), v[h]) for h in heads]
        state = [state_ref[h] for h in heads]
        cross = [_dot((q[h] * q_decay[h]).astype(BF16), state[h].astype(BF16)) for h in heads]
        upd = [_dot_tn((k[h] * k_decay[h]).astype(BF16), v[h]) for h in heads]
        for h in heads:
            state_ref[h] = state[h] * math.exp(log_gamma[h] * C) + jnp.where(in_rows[h], upd[h], 0.0)
        y = [inner[h] + cross[h] for h in heads]
        mu = [jnp.mean(y_, axis=-1, keepdims=True) for y_ in y]
        yc = [y_ - m_ for y_, m_ in zip(y, mu)]
        var = [jnp.mean(c_ * c_, axis=-1, keepdims=True) for c_ in yc]
        for h in heads:
            g = g_ref[0, rows, v_cols[h]]
            o_ref[0, rows, v_cols[h]] = (g * jax.nn.sigmoid(g)
                                         * (yc[h] * lax.rsqrt(var[h] + LN_EPS))).astype(o_ref.dtype)
        return 0

    lax.fori_loop(0, chunks_per_step, chunk_body, 0)


def _retention(qkv, gate, chunks_per_step=4):
    b, s, _ = qkv.shape
    C = RET_CHUNK * chunks_per_step
    qk_w = RET_HEADS * RET_QK_DIM
    v_w = RET_HEADS * RET_V_DIM
    return pl.pallas_call(
        functools.partial(_retention_kernel, chunks_per_step=chunks_per_step),
        grid=(b, s // C),
        in_specs=[pl.BlockSpec((1, C, qk_w), lambda bi, c: (bi, c, 1536 // qk_w)),
                  pl.BlockSpec((1, C, qk_w), lambda bi, c: (bi, c, 1792 // qk_w)),
                  pl.BlockSpec((1, C, v_w), lambda bi, c: (bi, c, 2048 // v_w)),
                  pl.BlockSpec((1, C, v_w), lambda bi, c: (bi, c, 0))],
        out_specs=pl.BlockSpec((1, C, v_w), lambda bi, c: (bi, c, 0)),
        out_shape=jax.ShapeDtypeStruct((b, s, v_w), BF16),
        scratch_shapes=[pltpu.VMEM((RET_HEADS, LANE, RET_V_DIM), F32)],
        compiler_params=_cparams(("parallel", "arbitrary")),
        name="retention",
    )(qkv, qkv, qkv, gate)


def _layer_tail_kernel(*refs, tm, n_pieces):
    piece_refs = [refs[3 * p:3 * p + 3] for p in range(n_pieces)]
    (x_ref, xhalo_ref, mg_ref, mb_ref, wu_ref, cw_ref, cb_ref, wd_ref, fg_ref, fb_ref,
     o_ref, act_ref) = refs[3 * n_pieces:]
    i = pl.program_id(1)
    mix = None
    for tile_ref, halo_ref, w_ref in piece_refs:
        part = _dot(jnp.concatenate([halo_ref[0], tile_ref[0]], axis=0), w_ref[...])
        mix = part if mix is None else mix + part
    xe = jnp.concatenate([xhalo_ref[0], x_ref[0]], axis=0)
    x1e = _layer_norm(DN_ALPHA * xe + mix, mg_ref[...], mb_ref[...])
    x1 = x1e[CONV_HALO:]
    xe = jnp.concatenate([jnp.where(i > 0, x1e[:CONV_HALO], 0.0), x1], axis=0).astype(BF16)
    n_chunks = FFN_HIDDEN // FFN_CHUNK
    for c in range(n_chunks):
        lo = c * FFN_CHUNK
        ue = _dot(xe, wu_ref[:, lo:lo + FFN_CHUNK])
        gate = _dot(xe[CONV_HALO:], wu_ref[:, FFN_HIDDEN + lo:FFN_HIDDEN + lo + FFN_CHUNK])
        cw = cw_ref[:, lo:lo + FFN_CHUNK]
        conv = (cb_ref[:, lo:lo + FFN_CHUNK]
                + ue[CONV_HALO - 2:CONV_HALO - 2 + tm] * cw[0:1]
                + ue[CONV_HALO - 1:CONV_HALO - 1 + tm] * cw[1:2]
                + ue[CONV_HALO:] * cw[2:3])
        act = 0.5 * conv * (1.0 + lax.erf(conv * (2.0 ** -0.5)))
        act_ref[:, lo:lo + FFN_CHUNK] = (act * gate).astype(BF16)
    ffn = _dot(act_ref[...], wd_ref[...])
    o_ref[0] = _layer_norm(DN_ALPHA * x1 + ffn, fg_ref[...], fb_ref[...])


def _layer_tail(pieces, x, w_out_bf16, ln_mix_g, ln_mix_b, w_up_bf16, conv_w, conv_b, w_down_bf16,
                ln_ffn_g, ln_ffn_b, tm=512):
    b, s, _ = x.shape
    assert len({p.shape[-1] for p in pieces}) == 1 and sum(p.shape[-1] for p in pieces) == D_MODEL
    kern = functools.partial(_layer_tail_kernel, tm=tm, n_pieces=len(pieces))
    fixed = lambda bi, i: (0, 0)
    tile = lambda bi, i: (bi, i, 0)
    halo = lambda bi, i: (bi, jnp.maximum(i * (tm // CONV_HALO) - 1, 0), 0)
    single = pl.Buffered(1)
    in_specs, args = [], []
    for row_block, piece in enumerate(pieces):
        width = piece.shape[-1]
        in_specs += [pl.BlockSpec((1, tm, width), tile), pl.BlockSpec((1, CONV_HALO, width), halo),
                     pl.BlockSpec((width, D_MODEL), functools.partial(lambda bi, i, r: (r, 0), r=row_block),
                                  pipeline_mode=single)]
        args += [piece, piece, w_out_bf16]
    in_specs += [pl.BlockSpec((1, tm, D_MODEL), tile), pl.BlockSpec((1, CONV_HALO, D_MODEL), halo),
                 pl.BlockSpec((1, D_MODEL), fixed), pl.BlockSpec((1, D_MODEL), fixed),
                 pl.BlockSpec((D_MODEL, 2 * FFN_HIDDEN), fixed, pipeline_mode=single),
                 pl.BlockSpec((3, FFN_HIDDEN), fixed),
                 pl.BlockSpec((1, FFN_HIDDEN), fixed),
                 pl.BlockSpec((FFN_HIDDEN, D_MODEL), fixed, pipeline_mode=single),
                 pl.BlockSpec((1, D_MODEL), fixed), pl.BlockSpec((1, D_MODEL), fixed)]
    args += [x, x, ln_mix_g.reshape(1, -1), ln_mix_b.reshape(1, -1), w_up_bf16, conv_w, conv_b.reshape(1, -1),
             w_down_bf16, ln_ffn_g.reshape(1, -1), ln_ffn_b.reshape(1, -1)]
    return pl.pallas_call(
        kern,
        grid=(b, s // tm),
        in_specs=in_specs,
        out_specs=pl.BlockSpec((1, tm, D_MODEL), tile),
        out_shape=jax.ShapeDtypeStruct((b, s, D_MODEL), F32),
        scratch_shapes=[pltpu.VMEM((tm, FFN_HIDDEN), BF16)],
        compiler_params=_cparams(("parallel", "arbitrary")),
        name="layer_tail",
    )(*args)


def _rwkv_proj_kernel(x_ref, halo_ref, mu_ref, wrkv_ref, w0_ref, w1_ref, w2_ref, a0_ref, a1_ref, a2_ref,
                      g1_ref, g2_ref, r_ref, k_ref, v_ref, wl_ref, a_ref, g_ref, *, tm):
    i = pl.program_id(1)
    x = x_ref[0]
    prev_row = jnp.where(i > 0, halo_ref[0, 7:8, :], 0.0)
    row = lax.broadcasted_iota(jnp.int32, (tm, 1), 0)
    x_prev = jnp.where(row == 0, prev_row, pltpu.roll(x, 1, 0))
    xx = x_prev - x

    def mix(n):
        return (x + xx * mu_ref[n:n + 1, :]).astype(BF16)

    r_ref[0] = _dot(mix(0), wrkv_ref[0]).astype(r_ref.dtype)
    k_ref[0] = _dot(mix(1), wrkv_ref[1]).astype(k_ref.dtype)
    v_ref[0] = _dot(mix(2), wrkv_ref[2]).astype(v_ref.dtype)
    lw = jnp.tanh(_dot(mix(3), w1_ref[...])).astype(BF16)
    u = w0_ref[...] + _dot(lw, w2_ref[...])
    wl_ref[0] = -math.exp(-0.5) * jax.nn.sigmoid(u)
    la = _dot(mix(4), a1_ref[...]).astype(BF16)
    a_ref[0] = jax.nn.sigmoid(a0_ref[...] + _dot(la, a2_ref[...])).astype(a_ref.dtype)
    lg = jax.nn.sigmoid(_dot(mix(5), g1_ref[...])).astype(BF16)
    g_ref[0] = _dot(lg, g2_ref[...]).astype(g_ref.dtype)


def _rwkv_proj(x, mu, w_rkv, w0, w1, w2, a0, a1, a2, g1, g2, tm=512):
    b, s, d = x.shape
    kern = functools.partial(_rwkv_proj_kernel, tm=tm)
    fixed2 = lambda bi, i: (0, 0)
    tile = pl.BlockSpec((1, tm, d), lambda bi, i: (bi, i, 0))
    lora = w1.shape[1]
    out = jax.ShapeDtypeStruct((b, s, d), F32)
    half = jax.ShapeDtypeStruct((b, s, d), BF16)
    return pl.pallas_call(
        kern,
        grid=(b, s // tm),
        in_specs=[tile,
                  pl.BlockSpec((1, 8, d), lambda bi, i: (bi, jnp.maximum(i * (tm // 8) - 1, 0), 0)),
                  pl.BlockSpec((6, d), fixed2),
                  pl.BlockSpec((3, d, d), lambda bi, i: (0, 0, 0)),
                  pl.BlockSpec((1, d), fixed2), pl.BlockSpec((d, lora), fixed2), pl.BlockSpec((lora, d), fixed2),
                  pl.BlockSpec((1, d), fixed2), pl.BlockSpec((d, lora), fixed2), pl.BlockSpec((lora, d), fixed2),
                  pl.BlockSpec((d, GATE_LORA_PAD), fixed2), pl.BlockSpec((GATE_LORA_PAD, d), fixed2)],
        out_specs=[tile] * 6,
        out_shape=[half, half, half, out, half, half],
        compiler_params=_cparams(("parallel", "arbitrary")),
        name="rwkv_proj",
    )(x, x, mu, w_rkv, w0.reshape(1, -1), w1, w2, a0.reshape(1, -1), a1, a2, g1, g2)


def _each(f, *lists):
    return [f(*xs) for xs in zip(*lists)]


def _wkv_pair_consts():
    T = WKV_CHUNK
    lane = lax.broadcasted_iota(jnp.int32, (1, LANE), 1)
    row = lax.broadcasted_iota(jnp.int32, (T, LANE), 0)
    col = lax.broadcasted_iota(jnp.int32, (T, LANE), 1) % T
    r2 = lax.broadcasted_iota(jnp.int32, (LANE, LANE), 0)
    c2 = lax.broadcasted_iota(jnp.int32, (LANE, LANE), 1)
    tri_r = lax.broadcasted_iota(jnp.int32, (T, T), 0)
    tri_c = lax.broadcasted_iota(jnp.int32, (T, T), 1)
    same_head = (r2 // T) == (c2 // T)
    merges = []
    size = 2 * WKV_INV_BASE
    while size <= T:
        merges.append(((row // size) == (col // size)) & ((row // (size // 2)) != (col // (size // 2))))
        size *= 2
    return dict(
        even=lane < T,
        strict=col < row, incl=col <= row, eye=col == row,
        in_base=(row // WKV_INV_BASE) == (col // WKV_INV_BASE),
        merges=merges,
        same_head=same_head, eye128=r2 == c2,
        tri=jnp.where(tri_c <= tri_r, 1.0, 0.0).astype(BF16),
    )


def _head_sum(x, even):
    s_even = jnp.sum(jnp.where(even, x, 0.0), axis=-1, keepdims=True)
    s_odd = jnp.sum(jnp.where(even, 0.0, x), axis=-1, keepdims=True)
    return jnp.where(even, s_even, s_odd)


def _wkv_prepare(r, k, v, wl, lr, kk_scale, ka, rk, cst):
    T = WKV_CHUNK
    bf = lambda t: t.astype(BF16)
    even, strict, incl, eye = cst["even"], cst["strict"], cst["incl"], cst["eye"]
    tri_incl_bf16 = cst["tri"]

    def stack2(x):
        zero = jnp.zeros_like(x)
        return jnp.concatenate([jnp.where(even, x, zero), jnp.where(even, zero, x)], axis=0)

    def pair_dot(packed, x):
        return _dot(packed, stack2(x))

    def pair_dot2(packed, x1, x2):
        return _dot(packed, jnp.concatenate([stack2(x1), stack2(x2)], axis=1))

    kk = _each(lambda k_, s_: k_ * s_, k, kk_scale)
    ss = _each(lambda t: _head_sum(t * t, even), kk)
    kk = _each(lambda t, s_: t / jnp.maximum(jnp.sqrt(s_), 1e-12), kk, ss)
    k = _each(lambda k_, lr_, ka_: k_ * (1.0 + (lr_ - 1.0) * ka_), k, lr, ka)
    b_vec = _each(lambda kk_, lr_: kk_ * lr_, kk, lr)

    w_hi = _each(bf, wl)
    w_r1 = _each(lambda w_, h_: w_ - h_.astype(F32), wl, w_hi)
    w_mid = _each(bf, w_r1)
    w_lo = _each(lambda r1, m_: (r1 - m_.astype(F32)).astype(BF16), w_r1, w_mid)
    cl = _each(lambda h_, m_, l_: _dot(tri_incl_bf16, h_) + (_dot(tri_incl_bf16, m_) + _dot(tri_incl_bf16, l_)),
               w_hi, w_mid, w_lo)
    cl_last = _each(lambda c_: c_[T - 1:T, :], cl)
    g_inv = _each(lambda c_: jnp.exp(-c_), cl)
    a_hat = _each(lambda kk_, c_, w_: -kk_ * jnp.exp(c_ - w_), kk, cl, wl)
    b_hat = _each(lambda b_, g_: bf(b_ * g_), b_vec, g_inv)
    k_hat = _each(lambda k_, g_: bf(k_ * g_), k, g_inv)
    r_hat = _each(lambda r_, c_: r_ * jnp.exp(c_), r, cl)
    to_end = _each(lambda l_, c_: jnp.exp(l_ - c_), cl_last, cl)
    b_end = _each(lambda b_, e_: bf(b_ * e_), b_vec, to_end)
    k_end = _each(lambda k_, e_: bf(k_ * e_), k, to_end)

    ar = _each(lambda a_, r_: bf(jnp.concatenate([a_, r_], axis=0)), a_hat, r_hat)
    m_bk = _each(lambda ar_, b_, k_: _dot_nt(ar_, jnp.concatenate([stack2(b_), stack2(k_)], axis=0)),
                 ar, b_hat, k_hat)
    m_b = _each(lambda m_: m_[:, :LANE], m_bk)
    m_k = _each(lambda m_: m_[:, LANE:], m_bk)
    a_ab = _each(lambda m_: jnp.where(strict, m_[:T], 0.0), m_b)
    a_rb = _each(lambda m_: bf(jnp.where(incl, m_[T:], 0.0)), m_b)
    a_ak = _each(lambda m_: bf(jnp.where(strict, m_[:T], 0.0)), m_k)
    a_rk = _each(lambda m_: bf(jnp.where(incl, m_[T:], 0.0)), m_k)

    p = _each(lambda a_: jnp.where(cst["in_base"], a_, 0.0), a_ab)
    minv = _each(lambda p_: jnp.where(eye, 1.0, 0.0) + p_, p)
    levels = int(math.log2(WKV_INV_BASE)) - 1
    p = _each(lambda p_: pair_dot(bf(p_), bf(p_)), p)
    for level in range(levels):
        if level + 1 < levels:
            both = _each(lambda p_, m_: pair_dot2(bf(p_), bf(m_), bf(p_)), p, minv)
            minv = _each(lambda m_, x_: m_ + x_[:, :LANE], minv, both)
            p = _each(lambda x_: x_[:, LANE:], both)
        else:
            minv = _each(lambda m_, p_: m_ + pair_dot(bf(p_), bf(m_)), minv, p)
    for off in cst["merges"]:
        a_off = _each(lambda a_: bf(jnp.where(off, a_, 0.0)), a_ab)
        minv = _each(lambda m_, a_: m_ + pair_dot(bf(m_), bf(pair_dot(a_, bf(m_)))), minv, a_off)

    v_bf16 = _each(bf, v)
    minv_bf16 = _each(bf, minv)
    ak_rk_v = _each(lambda ak_, rk_, v_: pair_dot(jnp.concatenate([ak_, rk_], axis=0), v_), a_ak, a_rk, v_bf16)
    akv = _each(lambda x_: bf(x_[:T]), ak_rk_v)
    at_w = _each(lambda m_, a_, x_: bf(pair_dot2(m_, a_[:T], x_)), minv_bf16, ar, akv)
    a_t = _each(lambda x_: x_[:, :LANE], at_w)
    w_bf16 = _each(lambda x_: x_[:, LANE:], at_w)
    rb_at_w = _each(lambda a_, t_, w_: pair_dot2(a_, t_, w_), a_rb, a_t, w_bf16)
    r_t = _each(lambda r_, x_: bf(r_ + x_[:, :LANE]), r_hat, rb_at_w)
    y0 = _each(lambda x_, y_: x_[:, LANE:] + y_[T:], rb_at_w, ak_rk_v)
    same_head, eye128 = cst["same_head"], cst["eye128"]
    atw_b = _each(lambda x_, b_: _dot_tn(x_, b_), at_w, b_end)
    trans = _each(lambda l_, x_: bf(jnp.where(eye128, jnp.exp(l_), 0.0) + jnp.where(same_head, x_[:LANE], 0.0)),
                  cl_last, atw_b)
    s_add = _each(lambda x_, v_, k_: jnp.where(same_head, x_[LANE:] + _dot_tn(v_, k_), 0.0),
                  atw_b, v_bf16, k_end)
    bonus = _each(lambda r_, k_, rk_, v_: _head_sum(r_ * k_ * rk_, even) * v_, r, k, rk, v)
    return trans, s_add, r_t, y0, bonus


def _wkv_kernel(r_ref, k_ref, v_ref, wl_ref, a_ref, gate_ref, kk_ref, ka_ref, rk_ref, lg_ref, lb_ref, o_ref,
                state_ref, trans_ref, sadd_ref, rt_ref, y0_ref, bonus_ref, *, pairs, unroll):
    @pl.when(pl.program_id(2) == 0)
    def _():
        state_ref[...] = jnp.zeros_like(state_ref)

    T = WKV_CHUNK
    n_chunks = WKV_TIME_BLOCK // T
    cst = _wkv_pair_consts()
    even = cst["even"]
    pair_cols = [slice(pp * LANE, (pp + 1) * LANE) for pp in range(pairs)]

    def prepare(group):
        chunk_ids = [group * unroll + cc for cc in range(unroll)]
        problems = [(cc, pp) for cc in range(unroll) for pp in range(pairs)]

        def tiles(ref):
            return [ref[0, pl.ds(pl.multiple_of(chunk_ids[cc] * T, T), T), pair_cols[pp]].astype(F32)
                    for cc, pp in problems]

        def vecs(ref):
            return [ref[:, pair_cols[pp]] for _, pp in problems]

        outs = _wkv_prepare(tiles(r_ref), tiles(k_ref), tiles(v_ref), tiles(wl_ref), tiles(a_ref),
                            vecs(kk_ref), vecs(ka_ref), vecs(rk_ref), cst)
        for ref, vals in zip((trans_ref, sadd_ref, rt_ref, y0_ref, bonus_ref), outs):
            for (cc, pp), val in zip(problems, vals):
                ref[chunk_ids[cc], pp] = val

    def scan(group):
        for cc in range(unroll):
            c = group * unroll + cc
            rows = pl.ds(pl.multiple_of(c * T, T), T)
            state_bf16 = [state_ref[pp].astype(BF16) for pp in range(pairs)]
            y = [_dot_nt(rt_ref[c, pp], state_bf16[pp]) + y0_ref[c, pp] for pp in range(pairs)]
            new_state = [_dot(state_bf16[pp], trans_ref[c, pp]) + sadd_ref[c, pp] for pp in range(pairs)]
            for pp in range(pairs):
                state_ref[pp] = new_state[pp]
            mu = _each(lambda y_: _head_sum(y_, even) * (1.0 / RWKV_HEAD_DIM), y)
            yc = _each(lambda y_, m_: y_ - m_, y, mu)
            var = _each(lambda c_: _head_sum(c_ * c_, even) * (1.0 / RWKV_HEAD_DIM), yc)
            for pp in range(pairs):
                out = (yc[pp] * lax.rsqrt(var[pp] + RWKV_GN_EPS) * lg_ref[:, pair_cols[pp]]
                       + lb_ref[:, pair_cols[pp]] + bonus_ref[c, pp])
                gated = out * gate_ref[0, rows, pair_cols[pp]].astype(F32)
                o_ref[0, rows, pair_cols[pp]] = gated.astype(o_ref.dtype)

    n_groups = n_chunks // unroll
    prepare(0)

    def pipelined(group, _):
        scan(group - 1)
        prepare(group)
        return 0

    lax.fori_loop(1, n_groups, pipelined, 0)
    scan(n_groups - 1)


def _wkv(r, k, v, wl, a, gate, k_k, k_a, r_k, lnx_g, lnx_b, pairs=2, unroll=8):
    b, s, d = r.shape
    tb = WKV_TIME_BLOCK
    width = pairs * LANE
    n_chunks = tb // WKV_CHUNK
    tile = pl.BlockSpec((1, tb, width), lambda bi, h, t: (bi, t, h))
    vec = pl.BlockSpec((1, width), lambda bi, h, t: (0, h))
    per_chunk = lambda rows, dt: pltpu.VMEM((n_chunks, pairs, rows, LANE), dt)
    kern = functools.partial(_wkv_kernel, pairs=pairs, unroll=unroll)
    return pl.pallas_call(
        kern,
        grid=(b, d // width, s // tb),
        in_specs=[tile] * 6 + [vec] * 5,
        out_specs=tile,
        out_shape=jax.ShapeDtypeStruct((b, s, d), BF16),
        scratch_shapes=[pltpu.VMEM((pairs, LANE, LANE), F32),
                        per_chunk(LANE, BF16), per_chunk(LANE, F32),
                        per_chunk(WKV_CHUNK, BF16), per_chunk(WKV_CHUNK, F32), per_chunk(WKV_CHUNK, F32)],
        compiler_params=_cparams(("parallel", "parallel", "arbitrary")),
        name="wkv7",
    )(r, k, v, wl, a, gate, *(t.reshape(1, -1) for t in (k_k, k_a, r_k, lnx_g, lnx_b)))


def kernel(x, ev_w_in, ev_lambda, ev_subln_g, ev_w_out, od_mu, od_w_rkv, od_w0, od_w1, od_w2, od_a0, od_a1, od_a2, od_g1, od_g2, od_k_k, od_k_a, od_r_k, od_lnx_g, od_lnx_b, od_w_out, ln_mix_g, ln_mix_b, ffn_w_up, ffn_conv_w, ffn_conv_b, ffn_w_down, ln_ffn_g, ln_ffn_b):
    b, s, d = x.shape
    m = b * s
    bf = lambda t: t.astype(BF16)

    qkv, gate = _in_proj(x.reshape(m, d), bf(ev_w_in[0]))
    qkv = qkv.reshape(b, s, EVEN_QKV_WIDTH)
    lambda_init = 0.8 - 0.6 * math.exp(-0.3 * 0)
    a_out = _diff_attention(qkv, ev_lambda[0], ev_subln_g[0], lambda_init)
    b_out = _retention(qkv, gate.reshape(b, s, RET_GATE_WIDTH))
    x2 = _layer_tail([a_out, b_out], x, bf(ev_w_out[0]), ln_mix_g[0], ln_mix_b[0], bf(ffn_w_up[0]),
                     ffn_conv_w[0], ffn_conv_b[0], bf(ffn_w_down[0]), ln_ffn_g[0], ln_ffn_b[0])

    pad = GATE_LORA_PAD - GATE_LORA
    g1 = jnp.pad(bf(od_g1[0]), ((0, 0), (0, pad)))
    g2 = jnp.pad(bf(od_g2[0]), ((0, pad), (0, 0)))
    r, k, v, wl, a, g = _rwkv_proj(x2, od_mu[0], bf(od_w_rkv[0]), od_w0[0], bf(od_w1[0]), bf(od_w2[0]),
                                   od_a0[0], bf(od_a1[0]), bf(od_a2[0]), g1, g2)
    y = _wkv(r, k, v, wl, a, g, od_k_k[0], od_k_a[0], od_r_k[0], od_lnx_g[0], od_lnx_b[0])
    x4 = _layer_tail([y], x2, bf(od_w_out[0]), ln_mix_g[1], ln_mix_b[1], bf(ffn_w_up[1]),
                     ffn_conv_w[1], ffn_conv_b[1], bf(ffn_w_down[1]), ln_ffn_g[1], ln_ffn_b[1])
    return x4
```

```python
import functools
import math

import jax
import jax.numpy as jnp
from jax import lax
from jax.experimental import pallas as pl
from jax.experimental.pallas import tpu as pltpu

F32 = jnp.float32
BF16 = jnp.bfloat16

D_MODEL = 1024
DEPTH = 2
DN_ALPHA = (2.0 * DEPTH) ** 0.25
LN_EPS = 1e-5

DIFF_HEAD_DIM = 64
LOG2_E = math.log2(math.e)
DIFF_HEADS = 4
RET_QK_DIM = 64
RET_V_DIM = 128
RET_HEADS = 4
RET_CHUNK = 128
EVEN_IN_WIDTH = 3072
EVEN_QKV_WIDTH = 2560
RET_GATE_WIDTH = 512

RWKV_HEAD_DIM = 64
RWKV_GN_EPS = 64e-5
GATE_LORA = 160
GATE_LORA_PAD = 256
WKV_CHUNK = 64
WKV_INV_BASE = 8
WKV_TIME_BLOCK = 1024

FFN_HIDDEN = 2816
FFN_CHUNK = 256
CONV_HALO = 16

LANE = 128
VMEM_LIMIT = 56 * 1024 * 1024


def _cparams(sem):
    return pltpu.CompilerParams(dimension_semantics=sem, vmem_limit_bytes=VMEM_LIMIT)


def _dot(a, b):
    return jnp.dot(a, b, preferred_element_type=F32)


def _dot_nt(a, b):
    return lax.dot_general(a, b, (((1,), (1,)), ((), ())), preferred_element_type=F32)


def _dot_tn(a, b):
    return lax.dot_general(a, b, (((0,), (0,)), ((), ())), preferred_element_type=F32)


def _layer_norm(z, g, b):
    mu = jnp.mean(z, axis=-1, keepdims=True)
    zc = z - mu
    var = jnp.mean(zc * zc, axis=-1, keepdims=True)
    return zc * lax.rsqrt(var + LN_EPS) * g + b


def _in_proj_kernel(x_ref, w_ref, qkv_ref, gate_ref):
    h = _dot(x_ref[...].astype(BF16), w_ref[...])
    qkv_ref[...] = h[:, :EVEN_QKV_WIDTH].astype(BF16)
    gate_ref[...] = h[:, EVEN_QKV_WIDTH:]


def _in_proj(x2d, w_bf16, tm=512):
    m = x2d.shape[0]
    return pl.pallas_call(
        _in_proj_kernel,
        grid=(m // tm,),
        in_specs=[pl.BlockSpec((tm, D_MODEL), lambda i: (i, 0)),
                  pl.BlockSpec((D_MODEL, EVEN_IN_WIDTH), lambda i: (0, 0))],
        out_specs=[pl.BlockSpec((tm, EVEN_QKV_WIDTH), lambda i: (i, 0)),
                   pl.BlockSpec((tm, RET_GATE_WIDTH), lambda i: (i, 0))],
        out_shape=[jax.ShapeDtypeStruct((m, EVEN_QKV_WIDTH), BF16),
                   jax.ShapeDtypeStruct((m, RET_GATE_WIDTH), F32)],
        compiler_params=_cparams(("parallel",)),
        name="in_proj",
    )(x2d, w_bf16)


def _diff_attn_kernel(lam_ref, g_ref, q_ref, k_ref, v_ref, o_ref, sa_ref, sb_ref, m_ref, l_ref, acc_ref,
                      *, tq, lambda_init):
    h = pl.program_id(1)
    qi = pl.program_id(2)
    tk = tq
    slope = jnp.where(h == 0, 2.0 ** -2, jnp.where(h == 1, 2.0 ** -4,
                      jnp.where(h == 2, 2.0 ** -6, 2.0 ** -8))).astype(F32)
    lp = lam_ref[...]
    lam = (jnp.exp(jnp.sum(lp[0:1] * lp[1:2], axis=-1, keepdims=True))
           - jnp.exp(jnp.sum(lp[2:3] * lp[3:4], axis=-1, keepdims=True)) + lambda_init)

    lane = lax.broadcasted_iota(jnp.int32, (1, LANE), 1)
    q = q_ref[0]
    zero = jnp.zeros_like(q)
    slope = slope * LOG2_E
    c = jnp.full((1, LANE), slope, F32)
    c_hi = c.astype(BF16).astype(F32)
    q_pos = jnp.broadcast_to(jnp.where(lane < 2, c_hi, jnp.where(lane < 4, c - c_hi, 0.0)).astype(BF16),
                             (tq, LANE))
    q12 = jnp.concatenate([jnp.concatenate([jnp.where(lane < DIFF_HEAD_DIM, q, zero), q_pos], axis=1),
                           jnp.concatenate([jnp.where(lane >= DIFF_HEAD_DIM, q, zero), q_pos], axis=1)],
                          axis=0)
    key_off = lax.broadcasted_iota(jnp.int32, (tk, LANE), 0)
    k_pos = jnp.where((lane == 0) | (lane == 2), key_off & 255,
                      jnp.where((lane == 1) | (lane == 3), key_off & ~255, 0)).astype(F32).astype(BF16)

    def scores_t(j, dst_ref):
        start = pl.multiple_of(j * tk, tk)
        k = jnp.concatenate([k_ref[0, pl.ds(start, tk), :], k_pos], axis=1)
        dst_ref[...] = _dot_nt(k, q12)

    def update(src_ref, j, diagonal):
        s_t = src_ref[...]
        if diagonal:
            key_i = lax.broadcasted_iota(jnp.int32, (tk, 1), 0)
            query_i = lax.broadcasted_iota(jnp.int32, (1, 2 * tq), 1) % tq
            s_t = jnp.where(key_i <= query_i, s_t, -1e30)
        shift = ((j - qi) * tk).astype(F32) * slope
        v = v_ref[0, pl.ds(pl.multiple_of(j * tk, tk), tk), :]
        m = m_ref[...]
        m_new = jnp.maximum(m, jnp.max(s_t, axis=0, keepdims=True) + shift)
        alpha = jnp.exp2(m - m_new)
        p_t = jnp.exp2(s_t - (m_new - shift))
        m_ref[...] = m_new
        l_ref[...] = alpha * l_ref[...] + jnp.sum(p_t, axis=0, keepdims=True)
        acc_ref[...] = alpha * acc_ref[...] + _dot_tn(v, p_t.astype(BF16))

    m_ref[...] = jnp.full(m_ref.shape, -1e30, F32)
    l_ref[...] = jnp.zeros(l_ref.shape, F32)
    acc_ref[...] = jnp.zeros(acc_ref.shape, F32)

    scores_t(0, sa_ref)

    def pair_body(jj, _):
        j0 = 2 * jj
        scores_t(j0 + 1, sb_ref)
        update(sa_ref, j0, False)
        scores_t(j0 + 2, sa_ref)
        update(sb_ref, j0 + 1, False)
        return 0

    lax.fori_loop(0, qi // 2, pair_body, 0)

    @pl.when(qi % 2 == 1)
    def _():
        scores_t(qi, sb_ref)
        update(sa_ref, qi - 1, False)
        update(sb_ref, qi, True)

    @pl.when(qi % 2 == 0)
    def _():
        update(sa_ref, qi, True)

    l = l_ref[...]
    acc_t = acc_ref[...]
    o_t = acc_t[:, :tq] / l[:, :tq] - lam * (acc_t[:, tq:] / l[:, tq:])
    o_t = o_t * lax.rsqrt(jnp.mean(o_t * o_t, axis=0, keepdims=True) + LN_EPS)
    o_ref[0] = (o_t.T * g_ref[...] * (1.0 - lambda_init)).astype(o_ref.dtype)


def _diff_attention(qkv, lam_p, subln_g, lambda_init, tq=512):
    b, s, _ = qkv.shape
    kern = functools.partial(_diff_attn_kernel, tq=tq, lambda_init=lambda_init)
    return pl.pallas_call(
        kern,
        grid=(b, DIFF_HEADS, s // tq),
        in_specs=[pl.BlockSpec((4, DIFF_HEAD_DIM), lambda bi, h, i: (0, 0)),
                  pl.BlockSpec((1, LANE), lambda bi, h, i: (0, 0)),
                  pl.BlockSpec((1, tq, LANE), lambda bi, h, i: (bi, i, h)),
                  pl.BlockSpec((1, s, LANE), lambda bi, h, i: (bi, 0, DIFF_HEADS + h)),
                  pl.BlockSpec((1, s, LANE), lambda bi, h, i: (bi, 0, 2 * DIFF_HEADS + h))],
        out_specs=pl.BlockSpec((1, tq, LANE), lambda bi, h, i: (bi, i, h)),
        out_shape=jax.ShapeDtypeStruct((b, s, DIFF_HEADS * LANE), BF16),
        scratch_shapes=[pltpu.VMEM((tq, 2 * tq), F32), pltpu.VMEM((tq, 2 * tq), F32),
                        pltpu.VMEM((1, 2 * tq), F32), pltpu.VMEM((1, 2 * tq), F32),
                        pltpu.VMEM((LANE, 2 * tq), F32)],
        compiler_params=_cparams(("parallel", "parallel", "arbitrary")),
        name="diff_attn",
    )(lam_p, subln_g.reshape(1, LANE), qkv, qkv, qkv)


def _retention_kernel(q_ref, k_ref, v_ref, g_ref, o_ref, state_ref, *, chunks_per_step):
    C = RET_CHUNK

    @pl.when(pl.program_id(1) == 0)
    def _():
        state_ref[...] = jnp.zeros_like(state_ref)

    lane = lax.broadcasted_iota(jnp.int32, (1, LANE), 1)
    row128 = lax.broadcasted_iota(jnp.int32, (LANE, 1), 0)
    ri = lax.broadcasted_iota(jnp.int32, (C, C), 0)
    ci = lax.broadcasted_iota(jnp.int32, (C, C), 1)
    rel = (ri - ci).astype(F32)
    idx = lax.broadcasted_iota(jnp.int32, (C, 1), 0).astype(F32)
    heads = range(RET_HEADS)
    log_gamma = [math.log1p(-(2.0 ** (-5.0 - h))) for h in heads]
    in_head = [(lane >= (h % 2) * RET_QK_DIM) & (lane < (h % 2 + 1) * RET_QK_DIM) for h in heads]
    in_rows = [(row128 >= (h % 2) * RET_QK_DIM) & (row128 < (h % 2 + 1) * RET_QK_DIM) for h in heads]
    decay = [jnp.where(rel >= 0, jnp.exp(lg * jnp.maximum(rel, 0.0)), 0.0) for lg in log_gamma]
    q_decay = [jnp.exp(lg * (idx + 1.0)) for lg in log_gamma]
    k_decay = [jnp.exp(lg * (C - 1.0 - idx)) for lg in log_gamma]
    v_cols = [slice(h * RET_V_DIM, (h + 1) * RET_V_DIM) for h in heads]

    group = 2
    problems = [(cc, h) for cc in range(group) for h in heads]

    def group_body(gi, _):
        rows = [pl.ds(pl.multiple_of((gi * group + cc) * C, C), C) for cc in range(group)]
        q_all = [q_ref[0, r, :].astype(F32) for r in rows]
        k_all = [k_ref[0, r, :].astype(F32) * (RET_QK_DIM ** -0.5) for r in rows]
        q = [jnp.where(in_head[h], q_all[cc][:, (h // 2) * LANE:(h // 2 + 1) * LANE], 0.0) for cc, h in problems]
        k = [k_all[cc][:, (h // 2) * LANE:(h // 2 + 1) * LANE] for cc, h in problems]
        v = [v_ref[0, rows[cc], v_cols[h]] for cc, h in problems]
        sc = [_dot_nt(q_.astype(BF16), k_.astype(BF16)) * decay[h] for q_, k_, (_, h) in zip(q, k, problems)]
        inner = [_dot(s_.astype(BF16), v_) for s_, v_ in zip(sc, v)]
        upd = [_dot_tn((k_ * k_decay[h]).astype(BF16), v_) for k_, v_, (_, h) in zip(k, v, problems)]
        q_dec = [(q_ * q_decay[h]).astype(BF16) for q_, (_, h) in zip(q, problems)]
        state = [state_ref[h] for h in heads]
        cross = []
        for i, (cc, h) in enumerate(problems):
            cross.append(_dot(q_dec[i], state[h].astype(BF16)))
            state[h] = state[h] * math.exp(log_gamma[h] * C) + jnp.where(in_rows[h], upd[i], 0.0)
        for h in heads:
            state_ref[h] = state[h]
        y = [i_ + c_ for i_, c_ in zip(inner, cross)]
        mu = [jnp.mean(y_, axis=-1, keepdims=True) for y_ in y]
        yc = [y_ - m_ for y_, m_ in zip(y, mu)]
        var = [jnp.mean(c_ * c_, axis=-1, keepdims=True) for c_ in yc]
        for i, (cc, h) in enumerate(problems):
            g = g_ref[0, rows[cc], v_cols[h]]
            o_ref[0, rows[cc], v_cols[h]] = (g * jax.nn.sigmoid(g)
                                             * (yc[i] * lax.rsqrt(var[i] + LN_EPS))).astype(o_ref.dtype)
        return 0

    lax.fori_loop(0, chunks_per_step // group, group_body, 0)


def _retention(qkv, gate, chunks_per_step=4):
    b, s, _ = qkv.shape
    C = RET_CHUNK * chunks_per_step
    qk_w = RET_HEADS * RET_QK_DIM
    v_w = RET_HEADS * RET_V_DIM
    return pl.pallas_call(
        functools.partial(_retention_kernel, chunks_per_step=chunks_per_step),
        grid=(b, s // C),
        in_specs=[pl.BlockSpec((1, C, qk_w), lambda bi, c: (bi, c, 1536 // qk_w)),
                  pl.BlockSpec((1, C, qk_w), lambda bi, c: (bi, c, 1792 // qk_w)),
                  pl.BlockSpec((1, C, v_w), lambda bi, c: (bi, c, 2048 // v_w)),
                  pl.BlockSpec((1, C, v_w), lambda bi, c: (bi, c, 0))],
        out_specs=pl.BlockSpec((1, C, v_w), lambda bi, c: (bi, c, 0)),
        out_shape=jax.ShapeDtypeStruct((b, s, v_w), BF16),
        scratch_shapes=[pltpu.VMEM((RET_HEADS, LANE, RET_V_DIM), F32)],
        compiler_params=_cparams(("parallel", "arbitrary")),
        name="retention",
    )(qkv, qkv, qkv, gate)


def _layer_tail_kernel(*refs, tm, n_pieces):
    piece_refs = [refs[3 * p:3 * p + 3] for p in range(n_pieces)]
    (x_ref, xhalo_ref, mg_ref, mb_ref, wu_ref, cw_ref, cb_ref, wd_ref, fg_ref, fb_ref,
     o_ref, act_ref) = refs[3 * n_pieces:]
    i = pl.program_id(1)
    mix = None
    for tile_ref, halo_ref, w_ref in piece_refs:
        part = _dot(jnp.concatenate([halo_ref[0], tile_ref[0]], axis=0), w_ref[...])
        mix = part if mix is None else mix + part
    xe = jnp.concatenate([xhalo_ref[0], x_ref[0]], axis=0)
    x1e = _layer_norm(DN_ALPHA * xe + mix, mg_ref[...], mb_ref[...])
    x1 = x1e[CONV_HALO:]
    xe = jnp.concatenate([jnp.where(i > 0, x1e[:CONV_HALO], 0.0), x1], axis=0).astype(BF16)
    n_chunks = FFN_HIDDEN // FFN_CHUNK
    for c in range(n_chunks):
        lo = c * FFN_CHUNK
        ue = _dot(xe, wu_ref[:, lo:lo + FFN_CHUNK])
        gate = _dot(xe[CONV_HALO:], wu_ref[:, FFN_HIDDEN + lo:FFN_HIDDEN + lo + FFN_CHUNK])
        cw = cw_ref[:, lo:lo + FFN_CHUNK]
        conv = (cb_ref[:, lo:lo + FFN_CHUNK]
                + ue[CONV_HALO - 2:CONV_HALO - 2 + tm] * cw[0:1]
                + ue[CONV_HALO - 1:CONV_HALO - 1 + tm] * cw[1:2]
                + ue[CONV_HALO:] * cw[2:3])
        act = 0.5 * conv * (1.0 + lax.erf(conv * (2.0 ** -0.5)))
        act_ref[:, lo:lo + FFN_CHUNK] = (act * gate).astype(BF16)
    ffn = _dot(act_ref[...], wd_ref[...])
    o_ref[0] = _layer_norm(DN_ALPHA * x1 + ffn, fg_ref[...], fb_ref[...])


def _layer_tail(pieces, x, w_out_bf16, ln_mix_g, ln_mix_b, w_up_bf16, conv_w, conv_b, w_down_bf16,
                ln_ffn_g, ln_ffn_b, tm=512):
    b, s, _ = x.shape
    assert len({p.shape[-1] for p in pieces}) == 1 and sum(p.shape[-1] for p in pieces) == D_MODEL
    kern = functools.partial(_layer_tail_kernel, tm=tm, n_pieces=len(pieces))
    fixed = lambda bi, i: (0, 0)
    tile = lambda bi, i: (bi, i, 0)
    halo = lambda bi, i: (bi, jnp.maximum(i * (tm // CONV_HALO) - 1, 0), 0)
    single = pl.Buffered(1)
    in_specs, args = [], []
    for row_block, piece in enumerate(pieces):
        width = piece.shape[-1]
        in_specs += [pl.BlockSpec((1, tm, width), tile), pl.BlockSpec((1, CONV_HALO, width), halo),
                     pl.BlockSpec((width, D_MODEL), functools.partial(lambda bi, i, r: (r, 0), r=row_block),
                                  pipeline_mode=single)]
        args += [piece, piece, w_out_bf16]
    in_specs += [pl.BlockSpec((1, tm, D_MODEL), tile), pl.BlockSpec((1, CONV_HALO, D_MODEL), halo),
                 pl.BlockSpec((1, D_MODEL), fixed), pl.BlockSpec((1, D_MODEL), fixed),
                 pl.BlockSpec((D_MODEL, 2 * FFN_HIDDEN), fixed, pipeline_mode=single),
                 pl.BlockSpec((3, FFN_HIDDEN), fixed),
                 pl.BlockSpec((1, FFN_HIDDEN), fixed),
                 pl.BlockSpec((FFN_HIDDEN, D_MODEL), fixed, pipeline_mode=single),
                 pl.BlockSpec((1, D_MODEL), fixed), pl.BlockSpec((1, D_MODEL), fixed)]
    args += [x, x, ln_mix_g.reshape(1, -1), ln_mix_b.reshape(1, -1), w_up_bf16, conv_w, conv_b.reshape(1, -1),
             w_down_bf16, ln_ffn_g.reshape(1, -1), ln_ffn_b.reshape(1, -1)]
    return pl.pallas_call(
        kern,
        grid=(b, s // tm),
        in_specs=in_specs,
        out_specs=pl.BlockSpec((1, tm, D_MODEL), tile),
        out_shape=jax.ShapeDtypeStruct((b, s, D_MODEL), F32),
        scratch_shapes=[pltpu.VMEM((tm, FFN_HIDDEN), BF16)],
        compiler_params=_cparams(("parallel", "arbitrary")),
        name="layer_tail",
    )(*args)


def _rwkv_proj_kernel(x_ref, halo_ref, mu_ref, wrkv_ref, w0_ref, w1_ref, w2_ref, a0_ref, a1_ref, a2_ref,
                      g1_ref, g2_ref, r_ref, k_ref, v_ref, wl_ref, a_ref, g_ref, *, tm):
    i = pl.program_id(1)
    x = x_ref[0]
    prev_row = jnp.where(i > 0, halo_ref[0, 7:8, :], 0.0)
    row = lax.broadcasted_iota(jnp.int32, (tm, 1), 0)
    x_prev = jnp.where(row == 0, prev_row, pltpu.roll(x, 1, 0))
    xx = x_prev - x

    def mix(n):
        return (x + xx * mu_ref[n:n + 1, :]).astype(BF16)

    r_ref[0] = _dot(mix(0), wrkv_ref[0]).astype(r_ref.dtype)
    k_ref[0] = _dot(mix(1), wrkv_ref[1]).astype(k_ref.dtype)
    v_ref[0] = _dot(mix(2), wrkv_ref[2]).astype(v_ref.dtype)
    lw = jnp.tanh(_dot(mix(3), w1_ref[...])).astype(BF16)
    u = w0_ref[...] + _dot(lw, w2_ref[...])
    wl_ref[0] = -math.exp(-0.5) * jax.nn.sigmoid(u)
    la = _dot(mix(4), a1_ref[...]).astype(BF16)
    a_ref[0] = jax.nn.sigmoid(a0_ref[...] + _dot(la, a2_ref[...])).astype(a_ref.dtype)
    lg = jax.nn.sigmoid(_dot(mix(5), g1_ref[...])).astype(BF16)
    g_ref[0] = _dot(lg, g2_ref[...]).astype(g_ref.dtype)


def _rwkv_proj(x, mu, w_rkv, w0, w1, w2, a0, a1, a2, g1, g2, tm=512):
    b, s, d = x.shape
    kern = functools.partial(_rwkv_proj_kernel, tm=tm)
    fixed2 = lambda bi, i: (0, 0)
    tile = pl.BlockSpec((1, tm, d), lambda bi, i: (bi, i, 0))
    lora = w1.shape[1]
    out = jax.ShapeDtypeStruct((b, s, d), F32)
    half = jax.ShapeDtypeStruct((b, s, d), BF16)
    return pl.pallas_call(
        kern,
        grid=(b, s // tm),
        in_specs=[tile,
                  pl.BlockSpec((1, 8, d), lambda bi, i: (bi, jnp.maximum(i * (tm // 8) - 1, 0), 0)),
                  pl.BlockSpec((6, d), fixed2),
                  pl.BlockSpec((3, d, d), lambda bi, i: (0, 0, 0)),
                  pl.BlockSpec((1, d), fixed2), pl.BlockSpec((d, lora), fixed2), pl.BlockSpec((lora, d), fixed2),
                  pl.BlockSpec((1, d), fixed2), pl.BlockSpec((d, lora), fixed2), pl.BlockSpec((lora, d), fixed2),
                  pl.BlockSpec((d, GATE_LORA_PAD), fixed2), pl.BlockSpec((GATE_LORA_PAD, d), fixed2)],
        out_specs=[tile] * 6,
        out_shape=[half, half, half, out, half, half],
        compiler_params=_cparams(("parallel", "arbitrary")),
        name="rwkv_proj",
    )(x, x, mu, w_rkv, w0.reshape(1, -1), w1, w2, a0.reshape(1, -1), a1, a2, g1, g2)


def _each(f, *lists):
    return [f(*xs) for xs in zip(*lists)]


def _wkv_pair_consts():
    T = WKV_CHUNK
    lane = lax.broadcasted_iota(jnp.int32, (1, LANE), 1)
    row = lax.broadcasted_iota(jnp.int32, (T, LANE), 0)
    col = lax.broadcasted_iota(jnp.int32, (T, LANE), 1) % T
    r2 = lax.broadcasted_iota(jnp.int32, (LANE, LANE), 0)
    c2 = lax.broadcasted_iota(jnp.int32, (LANE, LANE), 1)
    tri_r = lax.broadcasted_iota(jnp.int32, (T, T), 0)
    tri_c = lax.broadcasted_iota(jnp.int32, (T, T), 1)
    same_head = (r2 // T) == (c2 // T)
    merges = []
    size = 2 * WKV_INV_BASE
    while size <= T:
        merges.append(((row // size) == (col // size)) & ((row // (size // 2)) != (col // (size // 2))))
        size *= 2
    return dict(
        even=lane < T,
        strict=col < row, incl=col <= row, eye=col == row,
        in_base=(row // WKV_INV_BASE) == (col // WKV_INV_BASE),
        merges=merges,
        same_head=same_head, eye128=r2 == c2,
        tri=jnp.where(tri_c <= tri_r, 1.0, 0.0).astype(BF16),
    )


def _head_sum(x, even):
    s_even = jnp.sum(jnp.where(even, x, 0.0), axis=-1, keepdims=True)
    s_odd = jnp.sum(jnp.where(even, 0.0, x), axis=-1, keepdims=True)
    return jnp.where(even, s_even, s_odd)


def _wkv_prepare(r, k, v, wl, lr, kk_scale, ka, rk, cst):
    T = WKV_CHUNK
    bf = lambda t: t.astype(BF16)
    even, strict, incl, eye = cst["even"], cst["strict"], cst["incl"], cst["eye"]
    tri_incl_bf16 = cst["tri"]

    def stack2(x):
        zero = jnp.zeros_like(x)
        return jnp.concatenate([jnp.where(even, x, zero), jnp.where(even, zero, x)], axis=0)

    def pair_dot(packed, x):
        return _dot(packed, stack2(x))

    def pair_dot2(packed, x1, x2):
        return _dot(packed, jnp.concatenate([stack2(x1), stack2(x2)], axis=1))

    kk = _each(lambda k_, s_: k_ * s_, k, kk_scale)
    ss = _each(lambda t: _head_sum(t * t, even), kk)
    kk = _each(lambda t, s_: t / jnp.maximum(jnp.sqrt(s_), 1e-12), kk, ss)
    k = _each(lambda k_, lr_, ka_: k_ * (1.0 + (lr_ - 1.0) * ka_), k, lr, ka)
    b_vec = _each(lambda kk_, lr_: kk_ * lr_, kk, lr)

    w_hi = _each(bf, wl)
    w_r1 = _each(lambda w_, h_: w_ - h_.astype(F32), wl, w_hi)
    w_mid = _each(bf, w_r1)
    w_lo = _each(lambda r1, m_: (r1 - m_.astype(F32)).astype(BF16), w_r1, w_mid)
    cl = _each(lambda h_, m_, l_: _dot(tri_incl_bf16, h_) + (_dot(tri_incl_bf16, m_) + _dot(tri_incl_bf16, l_)),
               w_hi, w_mid, w_lo)
    cl_last = _each(lambda c_: c_[T - 1:T, :], cl)
    g_inv = _each(lambda c_: jnp.exp(-c_), cl)
    a_hat = _each(lambda kk_, c_, w_: -kk_ * jnp.exp(c_ - w_), kk, cl, wl)
    b_hat = _each(lambda b_, g_: bf(b_ * g_), b_vec, g_inv)
    k_hat = _each(lambda k_, g_: bf(k_ * g_), k, g_inv)
    r_hat = _each(lambda r_, c_: r_ * jnp.exp(c_), r, cl)
    to_end = _each(lambda l_, c_: jnp.exp(l_ - c_), cl_last, cl)
    b_end = _each(lambda b_, e_: bf(b_ * e_), b_vec, to_end)
    k_end = _each(lambda k_, e_: bf(k_ * e_), k, to_end)

    ar = _each(lambda a_, r_: bf(jnp.concatenate([a_, r_], axis=0)), a_hat, r_hat)
    m_bk = _each(lambda ar_, b_, k_: _dot_nt(ar_, jnp.concatenate([stack2(b_), stack2(k_)], axis=0)),
                 ar, b_hat, k_hat)
    m_b = _each(lambda m_: m_[:, :LANE], m_bk)
    m_k = _each(lambda m_: m_[:, LANE:], m_bk)
    a_ab = _each(lambda m_: jnp.where(strict, m_[:T], 0.0), m_b)
    a_rb = _each(lambda m_: bf(jnp.where(incl, m_[T:], 0.0)), m_b)
    a_ak = _each(lambda m_: bf(jnp.where(strict, m_[:T], 0.0)), m_k)
    a_rk = _each(lambda m_: bf(jnp.where(incl, m_[T:], 0.0)), m_k)

    p = _each(lambda a_: jnp.where(cst["in_base"], a_, 0.0), a_ab)
    minv = _each(lambda p_: jnp.where(eye, 1.0, 0.0) + p_, p)
    levels = int(math.log2(WKV_INV_BASE)) - 1
    p = _each(lambda p_: pair_dot(bf(p_), bf(p_)), p)
    for level in range(levels):
        if level + 1 < levels:
            both = _each(lambda p_, m_: pair_dot2(bf(p_), bf(m_), bf(p_)), p, minv)
            minv = _each(lambda m_, x_: m_ + x_[:, :LANE], minv, both)
            p = _each(lambda x_: x_[:, LANE:], both)
        else:
            minv = _each(lambda m_, p_: m_ + pair_dot(bf(p_), bf(m_)), minv, p)
    for off in cst["merges"]:
        a_off = _each(lambda a_: bf(jnp.where(off, a_, 0.0)), a_ab)
        minv = _each(lambda m_, a_: m_ + pair_dot(bf(m_), bf(pair_dot(a_, bf(m_)))), minv, a_off)

    v_bf16 = _each(bf, v)
    minv_bf16 = _each(bf, minv)
    ak_rk_v = _each(lambda ak_, rk_, v_: pair_dot(jnp.concatenate([ak_, rk_], axis=0), v_), a_ak, a_rk, v_bf16)
    akv = _each(lambda x_: bf(x_[:T]), ak_rk_v)
    at_w = _each(lambda m_, a_, x_: bf(pair_dot2(m_, a_[:T], x_)), minv_bf16, ar, akv)
    a_t = _each(lambda x_: x_[:, :LANE], at_w)
    w_bf16 = _each(lambda x_: x_[:, LANE:], at_w)
    rb_at_w = _each(lambda a_, t_, w_: pair_dot2(a_, t_, w_), a_rb, a_t, w_bf16)
    r_t = _each(lambda r_, x_: bf(r_ + x_[:, :LANE]), r_hat, rb_at_w)
    y0 = _each(lambda x_, y_: x_[:, LANE:] + y_[T:], rb_at_w, ak_rk_v)
    same_head, eye128 = cst["same_head"], cst["eye128"]
    atw_b = _each(lambda x_, b_: _dot_tn(x_, b_), at_w, b_end)
    trans = _each(lambda l_, x_: bf(jnp.where(eye128, jnp.exp(l_), 0.0) + jnp.where(same_head, x_[:LANE], 0.0)),
                  cl_last, atw_b)
    s_add = _each(lambda x_, v_, k_: jnp.where(same_head, x_[LANE:] + _dot_tn(v_, k_), 0.0),
                  atw_b, v_bf16, k_end)
    bonus = _each(lambda r_, k_, rk_, v_: _head_sum(r_ * k_ * rk_, even) * v_, r, k, rk, v)
    return trans, s_add, r_t, y0, bonus


def _wkv_kernel(r_ref, k_ref, v_ref, wl_ref, a_ref, gate_ref, kk_ref, ka_ref, rk_ref, lg_ref, lb_ref, o_ref,
                state_ref, trans_ref, sadd_ref, rt_ref, y0_ref, bonus_ref, *, pairs, unroll):
    @pl.when(pl.program_id(2) == 0)
    def _():
        state_ref[...] = jnp.zeros_like(state_ref)

    T = WKV_CHUNK
    n_chunks = WKV_TIME_BLOCK // T
    cst = _wkv_pair_consts()
    even = cst["even"]
    pair_cols = [slice(pp * LANE, (pp + 1) * LANE) for pp in range(pairs)]

    def prepare(group):
        chunk_ids = [group * unroll + cc for cc in range(unroll)]
        problems = [(cc, pp) for cc in range(unroll) for pp in range(pairs)]

        def tiles(ref):
            return [ref[0, pl.ds(pl.multiple_of(chunk_ids[cc] * T, T), T), pair_cols[pp]].astype(F32)
                    for cc, pp in problems]

        def vecs(ref):
            return [ref[:, pair_cols[pp]] for _, pp in problems]

        outs = _wkv_prepare(tiles(r_ref), tiles(k_ref), tiles(v_ref), tiles(wl_ref), tiles(a_ref),
                            vecs(kk_ref), vecs(ka_ref), vecs(rk_ref), cst)
        for ref, vals in zip((trans_ref, sadd_ref, rt_ref, y0_ref, bonus_ref), outs):
            for (cc, pp), val in zip(problems, vals):
                ref[chunk_ids[cc], pp] = val

    def scan(group):
        for cc in range(unroll):
            c = group * unroll + cc
            rows = pl.ds(pl.multiple_of(c * T, T), T)
            state_bf16 = [state_ref[pp].astype(BF16) for pp in range(pairs)]
            y = [_dot_nt(rt_ref[c, pp], state_bf16[pp]) + y0_ref[c, pp] for pp in range(pairs)]
            new_state = [_dot(state_bf16[pp], trans_ref[c, pp]) + sadd_ref[c, pp] for pp in range(pairs)]
            for pp in range(pairs):
                state_ref[pp] = new_state[pp]
            mu = _each(lambda y_: _head_sum(y_, even) * (1.0 / RWKV_HEAD_DIM), y)
            yc = _each(lambda y_, m_: y_ - m_, y, mu)
            var = _each(lambda c_: _head_sum(c_ * c_, even) * (1.0 / RWKV_HEAD_DIM), yc)
            for pp in range(pairs):
                out = (yc[pp] * lax.rsqrt(var[pp] + RWKV_GN_EPS) * lg_ref[:, pair_cols[pp]]
                       + lb_ref[:, pair_cols[pp]] + bonus_ref[c, pp])
                gated = out * gate_ref[0, rows, pair_cols[pp]].astype(F32)
                o_ref[0, rows, pair_cols[pp]] = gated.astype(o_ref.dtype)

    n_groups = n_chunks // unroll
    prepare(0)

    def pipelined(group, _):
        scan(group - 1)
        prepare(group)
        return 0

    lax.fori_loop(1, n_groups, pipelined, 0)
    scan(n_groups - 1)


def _wkv(r, k, v, wl, a, gate, k_k, k_a, r_k, lnx_g, lnx_b, pairs=2, unroll=8):
    b, s, d = r.shape
    tb = WKV_TIME_BLOCK
    width = pairs * LANE
    n_chunks = tb // WKV_CHUNK
    tile = pl.BlockSpec((1, tb, width), lambda bi, h, t: (bi, t, h))
    vec = pl.BlockSpec((1, width), lambda bi, h, t: (0, h))
    per_chunk = lambda rows, dt: pltpu.VMEM((n_chunks, pairs, rows, LANE), dt)
    kern = functools.partial(_wkv_kernel, pairs=pairs, unroll=unroll)
    return pl.pallas_call(
        kern,
        grid=(b, d // width, s // tb),
        in_specs=[tile] * 6 + [vec] * 5,
        out_specs=tile,
        out_shape=jax.ShapeDtypeStruct((b, s, d), BF16),
        scratch_shapes=[pltpu.VMEM((pairs, LANE, LANE), F32),
                        per_chunk(LANE, BF16), per_chunk(LANE, F32),
                        per_chunk(WKV_CHUNK, BF16), per_chunk(WKV_CHUNK, F32), per_chunk(WKV_CHUNK, F32)],
        compiler_params=_cparams(("parallel", "parallel", "arbitrary")),
        name="wkv7",
    )(r, k, v, wl, a, gate, *(t.reshape(1, -1) for t in (k_k, k_a, r_k, lnx_g, lnx_b)))


def kernel(x, ev_w_in, ev_lambda, ev_subln_g, ev_w_out, od_mu, od_w_rkv, od_w0, od_w1, od_w2, od_a0, od_a1, od_a2, od_g1, od_g2, od_k_k, od_k_a, od_r_k, od_lnx_g, od_lnx_b, od_w_out, ln_mix_g, ln_mix_b, ffn_w_up, ffn_conv_w, ffn_conv_b, ffn_w_down, ln_ffn_g, ln_ffn_b):
    b, s, d = x.shape
    m = b * s
    bf = lambda t: t.astype(BF16)

    q_cols = DIFF_HEADS * 2 * DIFF_HEAD_DIM
    col_scale = jnp.where(jnp.arange(EVEN_IN_WIDTH) < q_cols, DIFF_HEAD_DIM ** -0.5 * LOG2_E, 1.0)
    qkv, gate = _in_proj(x.reshape(m, d), bf(ev_w_in[0] * col_scale))
    qkv = qkv.reshape(b, s, EVEN_QKV_WIDTH)
    lambda_init = 0.8 - 0.6 * math.exp(-0.3 * 0)
    a_out = _diff_attention(qkv, ev_lambda[0], ev_subln_g[0], lambda_init)
    b_out = _retention(qkv, gate.reshape(b, s, RET_GATE_WIDTH))
    x2 = _layer_tail([a_out, b_out], x, bf(ev_w_out[0]), ln_mix_g[0], ln_mix_b[0], bf(ffn_w_up[0]),
                     ffn_conv_w[0], ffn_conv_b[0], bf(ffn_w_down[0]), ln_ffn_g[0], ln_ffn_b[0])

    pad = GATE_LORA_PAD - GATE_LORA
    g1 = jnp.pad(bf(od_g1[0]), ((0, 0), (0, pad)))
    g2 = jnp.pad(bf(od_g2[0]), ((0, pad), (0, 0)))
    r, k, v, wl, a, g = _rwkv_proj(x2, od_mu[0], bf(od_w_rkv[0]), od_w0[0], bf(od_w1[0]), bf(od_w2[0]),
                                   od_a0[0], bf(od_a1[0]), bf(od_a2[0]), g1, g2)
    y = _wkv(r, k, v, wl, a, g, od_k_k[0], od_k_a[0], od_r_k[0], od_lnx_g[0], od_lnx_b[0])
    x4 = _layer_tail([y], x2, bf(od_w_out[0]), ln_mix_g[1], ln_mix_b[1], bf(ffn_w_up[1]),
                     ffn_conv_w[1], ffn_conv_b[1], bf(ffn_w_down[1]), ln_ffn_g[1], ln_ffn_b[1])
    return x4
```

```python
import functools
import math

import jax
import jax.numpy as jnp
from jax import lax
from jax.experimental import pallas as pl
from jax.experimental.pallas import tpu as pltpu

F32 = jnp.float32
BF16 = jnp.bfloat16

D_MODEL = 1024
DEPTH = 2
DN_ALPHA = (2.0 * DEPTH) ** 0.25
LN_EPS = 1e-5

DIFF_HEAD_DIM = 64
LOG2_E = math.log2(math.e)
DIFF_HEADS = 4
RET_QK_DIM = 64
RET_V_DIM = 128
RET_HEADS = 4
RET_CHUNK = 128
EVEN_IN_WIDTH = 3072
EVEN_QKV_WIDTH = 2560
RET_GATE_WIDTH = 512

RWKV_HEAD_DIM = 64
RWKV_GN_EPS = 64e-5
GATE_LORA = 160
GATE_LORA_PAD = 256
WKV_CHUNK = 64
WKV_INV_BASE = 8
WKV_TIME_BLOCK = 1024

FFN_HIDDEN = 2816
FFN_CHUNK = 256
CONV_HALO = 16

LANE = 128
VMEM_LIMIT = 56 * 1024 * 1024


def _cparams(sem):
    return pltpu.CompilerParams(dimension_semantics=sem, vmem_limit_bytes=VMEM_LIMIT)


def _dot(a, b):
    return jnp.dot(a, b, preferred_element_type=F32)


def _dot_nt(a, b):
    return lax.dot_general(a, b, (((1,), (1,)), ((), ())), preferred_element_type=F32)


def _dot_tn(a, b):
    return lax.dot_general(a, b, (((0,), (0,)), ((), ())), preferred_element_type=F32)


def _layer_norm(z, g, b):
    mu = jnp.mean(z, axis=-1, keepdims=True)
    zc = z - mu
    var = jnp.mean(zc * zc, axis=-1, keepdims=True)
    return zc * lax.rsqrt(var + LN_EPS) * g + b


def _in_proj_kernel(x_ref, w_ref, qkv_ref, gate_ref):
    h = _dot(x_ref[...].astype(BF16), w_ref[...])
    qkv_ref[...] = h[:, :EVEN_QKV_WIDTH].astype(BF16)
    gate_ref[...] = h[:, EVEN_QKV_WIDTH:]


def _in_proj(x2d, w_bf16, tm=512):
    m = x2d.shape[0]
    return pl.pallas_call(
        _in_proj_kernel,
        grid=(m // tm,),
        in_specs=[pl.BlockSpec((tm, D_MODEL), lambda i: (i, 0)),
                  pl.BlockSpec((D_MODEL, EVEN_IN_WIDTH), lambda i: (0, 0))],
        out_specs=[pl.BlockSpec((tm, EVEN_QKV_WIDTH), lambda i: (i, 0)),
                   pl.BlockSpec((tm, RET_GATE_WIDTH), lambda i: (i, 0))],
        out_shape=[jax.ShapeDtypeStruct((m, EVEN_QKV_WIDTH), BF16),
                   jax.ShapeDtypeStruct((m, RET_GATE_WIDTH), F32)],
        compiler_params=_cparams(("parallel",)),
        name="in_proj",
    )(x2d, w_bf16)


def _diff_attn_kernel(lam_ref, g_ref, q_ref, k_ref, v_ref, o_ref, sa_ref, sb_ref, m_ref, l_ref, acc_ref,
                      *, tq, lambda_init):
    h = pl.program_id(1)
    qi = pl.program_id(2)
    tk = tq
    slope = jnp.where(h == 0, 2.0 ** -2, jnp.where(h == 1, 2.0 ** -4,
                      jnp.where(h == 2, 2.0 ** -6, 2.0 ** -8))).astype(F32)
    lp = lam_ref[...]
    lam = (jnp.exp(jnp.sum(lp[0:1] * lp[1:2], axis=-1, keepdims=True))
           - jnp.exp(jnp.sum(lp[2:3] * lp[3:4], axis=-1, keepdims=True)) + lambda_init)

    lane = lax.broadcasted_iota(jnp.int32, (1, LANE), 1)
    q = q_ref[0]
    zero = jnp.zeros_like(q)
    slope = slope * LOG2_E
    c = jnp.full((1, LANE), slope, F32)
    c_hi = c.astype(BF16).astype(F32)
    q_pos = jnp.broadcast_to(jnp.where(lane < 2, c_hi, jnp.where(lane < 4, c - c_hi, 0.0)).astype(BF16),
                             (tq, LANE))
    q12 = jnp.concatenate([jnp.concatenate([jnp.where(lane < DIFF_HEAD_DIM, q, zero), q_pos], axis=1),
                           jnp.concatenate([jnp.where(lane >= DIFF_HEAD_DIM, q, zero), q_pos], axis=1)],
                          axis=0)
    q12_t = q12.astype(F32).T.astype(BF16)
    key_off = lax.broadcasted_iota(jnp.int32, (tk, LANE), 0)
    k_pos = jnp.where((lane == 0) | (lane == 2), key_off & 255,
                      jnp.where((lane == 1) | (lane == 3), key_off & ~255, 0)).astype(F32).astype(BF16)

    def scores_t(j, dst_ref):
        start = pl.multiple_of(j * tk, tk)
        k = jnp.concatenate([k_ref[0, pl.ds(start, tk), :], k_pos], axis=1)
        dst_ref[...] = _dot(k, q12_t)

    def update(src_ref, j, diagonal):
        s_t = src_ref[...]
        if diagonal:
            key_i = lax.broadcasted_iota(jnp.int32, (tk, 1), 0)
            query_i = lax.broadcasted_iota(jnp.int32, (1, 2 * tq), 1) % tq
            s_t = jnp.where(key_i <= query_i, s_t, -1e30)
        shift = ((j - qi) * tk).astype(F32) * slope
        v = v_ref[0, pl.ds(pl.multiple_of(j * tk, tk), tk), :]
        m = m_ref[...]
        m_new = jnp.maximum(m, jnp.max(s_t, axis=0, keepdims=True) + shift)
        alpha = jnp.exp2(m - m_new)
        p_t = jnp.exp2(s_t - (m_new - shift))
        m_ref[...] = m_new
        l_ref[...] = alpha * l_ref[...] + jnp.sum(p_t, axis=0, keepdims=True)
        acc_ref[...] = alpha * acc_ref[...] + _dot_tn(v, p_t.astype(BF16))

    m_ref[...] = jnp.full(m_ref.shape, -1e30, F32)
    l_ref[...] = jnp.zeros(l_ref.shape, F32)
    acc_ref[...] = jnp.zeros(acc_ref.shape, F32)

    scores_t(0, sa_ref)

    def pair_body(jj, _):
        j0 = 2 * jj
        scores_t(j0 + 1, sb_ref)
        update(sa_ref, j0, False)
        scores_t(j0 + 2, sa_ref)
        update(sb_ref, j0 + 1, False)
        return 0

    lax.fori_loop(0, qi // 2, pair_body, 0)

    @pl.when(qi % 2 == 1)
    def _():
        scores_t(qi, sb_ref)
        update(sa_ref, qi - 1, False)
        update(sb_ref, qi, True)

    @pl.when(qi % 2 == 0)
    def _():
        update(sa_ref, qi, True)

    l = l_ref[...]
    acc_t = acc_ref[...]
    o_t = acc_t[:, :tq] / l[:, :tq] - lam * (acc_t[:, tq:] / l[:, tq:])
    o_t = o_t * lax.rsqrt(jnp.mean(o_t * o_t, axis=0, keepdims=True) + LN_EPS)
    o_ref[0] = (o_t.T * g_ref[...] * (1.0 - lambda_init)).astype(o_ref.dtype)


def _diff_attention(qkv, lam_p, subln_g, lambda_init, tq=512):
    b, s, _ = qkv.shape
    kern = functools.partial(_diff_attn_kernel, tq=tq, lambda_init=lambda_init)
    return pl.pallas_call(
        kern,
        grid=(b, DIFF_HEADS, s // tq),
        in_specs=[pl.BlockSpec((4, DIFF_HEAD_DIM), lambda bi, h, i: (0, 0)),
                  pl.BlockSpec((1, LANE), lambda bi, h, i: (0, 0)),
                  pl.BlockSpec((1, tq, LANE), lambda bi, h, i: (bi, i, h)),
                  pl.BlockSpec((1, s, LANE), lambda bi, h, i: (bi, 0, DIFF_HEADS + h)),
                  pl.BlockSpec((1, s, LANE), lambda bi, h, i: (bi, 0, 2 * DIFF_HEADS + h))],
        out_specs=pl.BlockSpec((1, tq, LANE), lambda bi, h, i: (bi, i, h)),
        out_shape=jax.ShapeDtypeStruct((b, s, DIFF_HEADS * LANE), BF16),
        scratch_shapes=[pltpu.VMEM((tq, 2 * tq), F32), pltpu.VMEM((tq, 2 * tq), F32),
                        pltpu.VMEM((1, 2 * tq), F32), pltpu.VMEM((1, 2 * tq), F32),
                        pltpu.VMEM((LANE, 2 * tq), F32)],
        compiler_params=_cparams(("parallel", "parallel", "arbitrary")),
        name="diff_attn",
    )(lam_p, subln_g.reshape(1, LANE), qkv, qkv, qkv)


def _retention_kernel(q_ref, k_ref, v_ref, g_ref, o_ref, state_ref, *, chunks_per_step):
    C = RET_CHUNK

    @pl.when(pl.program_id(1) == 0)
    def _():
        state_ref[...] = jnp.zeros_like(state_ref)

    lane = lax.broadcasted_iota(jnp.int32, (1, LANE), 1)
    row128 = lax.broadcasted_iota(jnp.int32, (LANE, 1), 0)
    ri = lax.broadcasted_iota(jnp.int32, (C, C), 0)
    ci = lax.broadcasted_iota(jnp.int32, (C, C), 1)
    rel = (ri - ci).astype(F32)
    idx = lax.broadcasted_iota(jnp.int32, (C, 1), 0).astype(F32)
    heads = range(RET_HEADS)
    log_gamma = [math.log1p(-(2.0 ** (-5.0 - h))) for h in heads]
    in_head = [(lane >= (h % 2) * RET_QK_DIM) & (lane < (h % 2 + 1) * RET_QK_DIM) for h in heads]
    in_rows = [(row128 >= (h % 2) * RET_QK_DIM) & (row128 < (h % 2 + 1) * RET_QK_DIM) for h in heads]
    decay = [jnp.where(rel >= 0, jnp.exp(lg * jnp.maximum(rel, 0.0)), 0.0) for lg in log_gamma]
    q_decay = [jnp.exp(lg * (idx + 1.0)) for lg in log_gamma]
    k_decay = [jnp.exp(lg * (C - 1.0 - idx)) for lg in log_gamma]
    v_cols = [slice(h * RET_V_DIM, (h + 1) * RET_V_DIM) for h in heads]

    group = 2
    problems = [(cc, h) for cc in range(group) for h in heads]

    def group_body(gi, _):
        rows = [pl.ds(pl.multiple_of((gi * group + cc) * C, C), C) for cc in range(group)]
        q_all = [q_ref[0, r, :].astype(F32) for r in rows]
        k_all = [k_ref[0, r, :].astype(F32) * (RET_QK_DIM ** -0.5) for r in rows]
        q = [jnp.where(in_head[h], q_all[cc][:, (h // 2) * LANE:(h // 2 + 1) * LANE], 0.0) for cc, h in problems]
        k = [k_all[cc][:, (h // 2) * LANE:(h // 2 + 1) * LANE] for cc, h in problems]
        v = [v_ref[0, rows[cc], v_cols[h]] for cc, h in problems]
        sc = [_dot_nt(q_.astype(BF16), k_.astype(BF16)) * decay[h] for q_, k_, (_, h) in zip(q, k, problems)]
        inner = [_dot(s_.astype(BF16), v_) for s_, v_ in zip(sc, v)]
        upd = [_dot_tn((k_ * k_decay[h]).astype(BF16), v_) for k_, v_, (_, h) in zip(k, v, problems)]
        q_dec = [(q_ * q_decay[h]).astype(BF16) for q_, (_, h) in zip(q, problems)]
        state = [state_ref[h] for h in heads]
        cross = []
        for i, (cc, h) in enumerate(problems):
            cross.append(_dot(q_dec[i], state[h].astype(BF16)))
            state[h] = state[h] * math.exp(log_gamma[h] * C) + jnp.where(in_rows[h], upd[i], 0.0)
        for h in heads:
            state_ref[h] = state[h]
        y = [i_ + c_ for i_, c_ in zip(inner, cross)]
        mu = [jnp.mean(y_, axis=-1, keepdims=True) for y_ in y]
        yc = [y_ - m_ for y_, m_ in zip(y, mu)]
        var = [jnp.mean(c_ * c_, axis=-1, keepdims=True) for c_ in yc]
        for i, (cc, h) in enumerate(problems):
            g = g_ref[0, rows[cc], v_cols[h]]
            o_ref[0, rows[cc], v_cols[h]] = (g * jax.nn.sigmoid(g)
                                             * (yc[i] * lax.rsqrt(var[i] + LN_EPS))).astype(o_ref.dtype)
        return 0

    lax.fori_loop(0, chunks_per_step // group, group_body, 0)


def _retention(qkv, gate, chunks_per_step=4):
    b, s, _ = qkv.shape
    C = RET_CHUNK * chunks_per_step
    qk_w = RET_HEADS * RET_QK_DIM
    v_w = RET_HEADS * RET_V_DIM
    return pl.pallas_call(
        functools.partial(_retention_kernel, chunks_per_step=chunks_per_step),
        grid=(b, s // C),
        in_specs=[pl.BlockSpec((1, C, qk_w), lambda bi, c: (bi, c, 1536 // qk_w)),
                  pl.BlockSpec((1, C, qk_w), lambda bi, c: (bi, c, 1792 // qk_w)),
                  pl.BlockSpec((1, C, v_w), lambda bi, c: (bi, c, 2048 // v_w)),
                  pl.BlockSpec((1, C, v_w), lambda bi, c: (bi, c, 0))],
        out_specs=pl.BlockSpec((1, C, v_w), lambda bi, c: (bi, c, 0)),
        out_shape=jax.ShapeDtypeStruct((b, s, v_w), BF16),
        scratch_shapes=[pltpu.VMEM((RET_HEADS, LANE, RET_V_DIM), F32)],
        compiler_params=_cparams(("parallel", "arbitrary")),
        name="retention",
    )(qkv, qkv, qkv, gate)


def _layer_tail_kernel(*refs, tm, n_pieces):
    piece_refs = [refs[3 * p:3 * p + 3] for p in range(n_pieces)]
    (x_ref, xhalo_ref, mg_ref, mb_ref, wu_ref, cw_ref, cb_ref, wd_ref, fg_ref, fb_ref,
     o_ref, act_ref) = refs[3 * n_pieces:]
    i = pl.program_id(1)
    mix = None
    for tile_ref, halo_ref, w_ref in piece_refs:
        part = _dot(jnp.concatenate([halo_ref[0], tile_ref[0]], axis=0), w_ref[...])
        mix = part if mix is None else mix + part
    xe = jnp.concatenate([xhalo_ref[0], x_ref[0]], axis=0)
    x1e = _layer_norm(DN_ALPHA * xe + mix, mg_ref[...], mb_ref[...])
    x1 = x1e[CONV_HALO:]
    xe = jnp.concatenate([jnp.where(i > 0, x1e[:CONV_HALO], 0.0), x1], axis=0).astype(BF16)
    n_chunks = FFN_HIDDEN // FFN_CHUNK
    for c in range(n_chunks):
        lo = c * FFN_CHUNK
        ue = _dot(xe, wu_ref[:, lo:lo + FFN_CHUNK])
        gate = _dot(xe[CONV_HALO:], wu_ref[:, FFN_HIDDEN + lo:FFN_HIDDEN + lo + FFN_CHUNK])
        cw = cw_ref[:, lo:lo + FFN_CHUNK]
        conv = (cb_ref[:, lo:lo + FFN_CHUNK]
                + ue[CONV_HALO - 2:CONV_HALO - 2 + tm] * cw[0:1]
                + ue[CONV_HALO - 1:CONV_HALO - 1 + tm] * cw[1:2]
                + ue[CONV_HALO:] * cw[2:3])
        act = 0.5 * conv * (1.0 + lax.erf(conv * (2.0 ** -0.5)))
        act_ref[:, lo:lo + FFN_CHUNK] = (act * gate).astype(BF16)
    ffn = _dot(act_ref[...], wd_ref[...])
    o_ref[0] = _layer_norm(DN_ALPHA * x1 + ffn, fg_ref[...], fb_ref[...])


def _layer_tail(pieces, x, w_out_bf16, ln_mix_g, ln_mix_b, w_up_bf16, conv_w, conv_b, w_down_bf16,
                ln_ffn_g, ln_ffn_b, tm=512):
    b, s, _ = x.shape
    assert len({p.shape[-1] for p in pieces}) == 1 and sum(p.shape[-1] for p in pieces) == D_MODEL
    kern = functools.partial(_layer_tail_kernel, tm=tm, n_pieces=len(pieces))
    fixed = lambda bi, i: (0, 0)
    tile = lambda bi, i: (bi, i, 0)
    halo = lambda bi, i: (bi, jnp.maximum(i * (tm // CONV_HALO) - 1, 0), 0)
    single = pl.Buffered(1)
    in_specs, args = [], []
    for row_block, piece in enumerate(pieces):
        width = piece.shape[-1]
        in_specs += [pl.BlockSpec((1, tm, width), tile), pl.BlockSpec((1, CONV_HALO, width), halo),
                     pl.BlockSpec((width, D_MODEL), functools.partial(lambda bi, i, r: (r, 0), r=row_block),
                                  pipeline_mode=single)]
        args += [piece, piece, w_out_bf16]
    in_specs += [pl.BlockSpec((1, tm, D_MODEL), tile), pl.BlockSpec((1, CONV_HALO, D_MODEL), halo),
                 pl.BlockSpec((1, D_MODEL), fixed), pl.BlockSpec((1, D_MODEL), fixed),
                 pl.BlockSpec((D_MODEL, 2 * FFN_HIDDEN), fixed, pipeline_mode=single),
                 pl.BlockSpec((3, FFN_HIDDEN), fixed),
                 pl.BlockSpec((1, FFN_HIDDEN), fixed),
                 pl.BlockSpec((FFN_HIDDEN, D_MODEL), fixed, pipeline_mode=single),
                 pl.BlockSpec((1, D_MODEL), fixed), pl.BlockSpec((1, D_MODEL), fixed)]
    args += [x, x, ln_mix_g.reshape(1, -1), ln_mix_b.reshape(1, -1), w_up_bf16, conv_w, conv_b.reshape(1, -1),
             w_down_bf16, ln_ffn_g.reshape(1, -1), ln_ffn_b.reshape(1, -1)]
    return pl.pallas_call(
        kern,
        grid=(b, s // tm),
        in_specs=in_specs,
        out_specs=pl.BlockSpec((1, tm, D_MODEL), tile),
        out_shape=jax.ShapeDtypeStruct((b, s, D_MODEL), F32),
        scratch_shapes=[pltpu.VMEM((tm, FFN_HIDDEN), BF16)],
        compiler_params=_cparams(("parallel", "arbitrary")),
        name="layer_tail",
    )(*args)


def _rwkv_proj_kernel(x_ref, halo_ref, mu_ref, wrkv_ref, w0_ref, w1_ref, w2_ref, a0_ref, a1_ref, a2_ref,
                      g1_ref, g2_ref, r_ref, k_ref, v_ref, wl_ref, a_ref, g_ref, *, tm):
    i = pl.program_id(1)
    x = x_ref[0]
    prev_row = jnp.where(i > 0, halo_ref[0, 7:8, :], 0.0)
    row = lax.broadcasted_iota(jnp.int32, (tm, 1), 0)
    x_prev = jnp.where(row == 0, prev_row, pltpu.roll(x, 1, 0))
    xx = x_prev - x

    def mix(n):
        return (x + xx * mu_ref[n:n + 1, :]).astype(BF16)

    r_ref[0] = _dot(mix(0), wrkv_ref[0]).astype(r_ref.dtype)
    k_ref[0] = _dot(mix(1), wrkv_ref[1]).astype(k_ref.dtype)
    v_ref[0] = _dot(mix(2), wrkv_ref[2]).astype(v_ref.dtype)
    lw = jnp.tanh(_dot(mix(3), w1_ref[...])).astype(BF16)
    u = w0_ref[...] + _dot(lw, w2_ref[...])
    wl_ref[0] = -math.exp(-0.5) * jax.nn.sigmoid(u)
    la = _dot(mix(4), a1_ref[...]).astype(BF16)
    a_ref[0] = jax.nn.sigmoid(a0_ref[...] + _dot(la, a2_ref[...])).astype(a_ref.dtype)
    lg = jax.nn.sigmoid(_dot(mix(5), g1_ref[...])).astype(BF16)
    g_ref[0] = _dot(lg, g2_ref[...]).astype(g_ref.dtype)


def _rwkv_proj(x, mu, w_rkv, w0, w1, w2, a0, a1, a2, g1, g2, tm=512):
    b, s, d = x.shape
    kern = functools.partial(_rwkv_proj_kernel, tm=tm)
    fixed2 = lambda bi, i: (0, 0)
    tile = pl.BlockSpec((1, tm, d), lambda bi, i: (bi, i, 0))
    lora = w1.shape[1]
    out = jax.ShapeDtypeStruct((b, s, d), F32)
    half = jax.ShapeDtypeStruct((b, s, d), BF16)
    return pl.pallas_call(
        kern,
        grid=(b, s // tm),
        in_specs=[tile,
                  pl.BlockSpec((1, 8, d), lambda bi, i: (bi, jnp.maximum(i * (tm // 8) - 1, 0), 0)),
                  pl.BlockSpec((6, d), fixed2),
                  pl.BlockSpec((3, d, d), lambda bi, i: (0, 0, 0)),
                  pl.BlockSpec((1, d), fixed2), pl.BlockSpec((d, lora), fixed2), pl.BlockSpec((lora, d), fixed2),
                  pl.BlockSpec((1, d), fixed2), pl.BlockSpec((d, lora), fixed2), pl.BlockSpec((lora, d), fixed2),
                  pl.BlockSpec((d, GATE_LORA_PAD), fixed2), pl.BlockSpec((GATE_LORA_PAD, d), fixed2)],
        out_specs=[tile] * 6,
        out_shape=[half, half, half, out, half, half],
        compiler_params=_cparams(("parallel", "arbitrary")),
        name="rwkv_proj",
    )(x, x, mu, w_rkv, w0.reshape(1, -1), w1, w2, a0.reshape(1, -1), a1, a2, g1, g2)


def _each(f, *lists):
    return [f(*xs) for xs in zip(*lists)]


def _wkv_pair_consts():
    T = WKV_CHUNK
    lane = lax.broadcasted_iota(jnp.int32, (1, LANE), 1)
    row = lax.broadcasted_iota(jnp.int32, (T, LANE), 0)
    col = lax.broadcasted_iota(jnp.int32, (T, LANE), 1) % T
    r2 = lax.broadcasted_iota(jnp.int32, (LANE, LANE), 0)
    c2 = lax.broadcasted_iota(jnp.int32, (LANE, LANE), 1)
    tri_r = lax.broadcasted_iota(jnp.int32, (T, T), 0)
    tri_c = lax.broadcasted_iota(jnp.int32, (T, T), 1)
    same_head = (r2 // T) == (c2 // T)
    merges = []
    size = 2 * WKV_INV_BASE
    while size <= T:
        merges.append(((row // size) == (col // size)) & ((row // (size // 2)) != (col // (size // 2))))
        size *= 2
    return dict(
        even=lane < T,
        strict=col < row, incl=col <= row, eye=col == row,
        in_base=(row // WKV_INV_BASE) == (col // WKV_INV_BASE),
        merges=merges,
        same_head=same_head, eye128=r2 == c2,
        tri=jnp.where(tri_c <= tri_r, 1.0, 0.0).astype(BF16),
    )


def _head_sum(x, even):
    s_even = jnp.sum(jnp.where(even, x, 0.0), axis=-1, keepdims=True)
    s_odd = jnp.sum(jnp.where(even, 0.0, x), axis=-1, keepdims=True)
    return jnp.where(even, s_even, s_odd)


def _wkv_prepare(r, k, v, wl, lr, kk_scale, ka, rk, cst):
    T = WKV_CHUNK
    bf = lambda t: t.astype(BF16)
    even, strict, incl, eye = cst["even"], cst["strict"], cst["incl"], cst["eye"]
    tri_incl_bf16 = cst["tri"]

    def stack2(x):
        zero = jnp.zeros_like(x)
        return jnp.concatenate([jnp.where(even, x, zero), jnp.where(even, zero, x)], axis=0)

    def pair_dot(packed, x):
        return _dot(packed, stack2(x))

    def pair_dot2(packed, x1, x2):
        return _dot(packed, jnp.concatenate([stack2(x1), stack2(x2)], axis=1))

    kk = _each(lambda k_, s_: k_ * s_, k, kk_scale)
    ss = _each(lambda t: _head_sum(t * t, even), kk)
    kk = _each(lambda t, s_: t / jnp.maximum(jnp.sqrt(s_), 1e-12), kk, ss)
    k = _each(lambda k_, lr_, ka_: k_ * (1.0 + (lr_ - 1.0) * ka_), k, lr, ka)
    b_vec = _each(lambda kk_, lr_: kk_ * lr_, kk, lr)

    w_hi = _each(bf, wl)
    w_r1 = _each(lambda w_, h_: w_ - h_.astype(F32), wl, w_hi)
    w_mid = _each(bf, w_r1)
    w_lo = _each(lambda r1, m_: (r1 - m_.astype(F32)).astype(BF16), w_r1, w_mid)
    cl = _each(lambda h_, m_, l_: _dot(tri_incl_bf16, h_) + (_dot(tri_incl_bf16, m_) + _dot(tri_incl_bf16, l_)),
               w_hi, w_mid, w_lo)
    cl_last = _each(lambda c_: c_[T - 1:T, :], cl)
    g_inv = _each(lambda c_: jnp.exp(-c_), cl)
    a_hat = _each(lambda kk_, c_, w_: -kk_ * jnp.exp(c_ - w_), kk, cl, wl)
    b_hat = _each(lambda b_, g_: bf(b_ * g_), b_vec, g_inv)
    k_hat = _each(lambda k_, g_: bf(k_ * g_), k, g_inv)
    r_hat = _each(lambda r_, c_: r_ * jnp.exp(c_), r, cl)
    to_end = _each(lambda l_, c_: jnp.exp(l_ - c_), cl_last, cl)
    b_end = _each(lambda b_, e_: bf(b_ * e_), b_vec, to_end)
    k_end = _each(lambda k_, e_: bf(k_ * e_), k, to_end)

    ar = _each(lambda a_, r_: bf(jnp.concatenate([a_, r_], axis=0)), a_hat, r_hat)
    m_bk = _each(lambda ar_, b_, k_: _dot_nt(ar_, jnp.concatenate([stack2(b_), stack2(k_)], axis=0)),
                 ar, b_hat, k_hat)
    m_b = _each(lambda m_: m_[:, :LANE], m_bk)
    m_k = _each(lambda m_: m_[:, LANE:], m_bk)
    a_ab = _each(lambda m_: jnp.where(strict, m_[:T], 0.0), m_b)
    a_rb = _each(lambda m_: bf(jnp.where(incl, m_[T:], 0.0)), m_b)
    a_ak = _each(lambda m_: bf(jnp.where(strict, m_[:T], 0.0)), m_k)
    a_rk = _each(lambda m_: bf(jnp.where(incl, m_[T:], 0.0)), m_k)

    p = _each(lambda a_: jnp.where(cst["in_base"], a_, 0.0), a_ab)
    minv = _each(lambda p_: jnp.where(eye, 1.0, 0.0) + p_, p)
    levels = int(math.log2(WKV_INV_BASE)) - 1
    p = _each(lambda p_: pair_dot(bf(p_), bf(p_)), p)
    for level in range(levels):
        if level + 1 < levels:
            both = _each(lambda p_, m_: pair_dot2(bf(p_), bf(m_), bf(p_)), p, minv)
            minv = _each(lambda m_, x_: m_ + x_[:, :LANE], minv, both)
            p = _each(lambda x_: x_[:, LANE:], both)
        else:
            minv = _each(lambda m_, p_: m_ + pair_dot(bf(p_), bf(m_)), minv, p)
    for off in cst["merges"]:
        a_off = _each(lambda a_: bf(jnp.where(off, a_, 0.0)), a_ab)
        minv = _each(lambda m_, a_: m_ + pair_dot(bf(m_), bf(pair_dot(a_, bf(m_)))), minv, a_off)

    v_bf16 = _each(bf, v)
    minv_bf16 = _each(bf, minv)
    ak_rk_v = _each(lambda ak_, rk_, v_: pair_dot(jnp.concatenate([ak_, rk_], axis=0), v_), a_ak, a_rk, v_bf16)
    akv = _each(lambda x_: bf(x_[:T]), ak_rk_v)
    at_w = _each(lambda m_, a_, x_: bf(pair_dot2(m_, a_[:T], x_)), minv_bf16, ar, akv)
    a_t = _each(lambda x_: x_[:, :LANE], at_w)
    w_bf16 = _each(lambda x_: x_[:, LANE:], at_w)
    rb_at_w = _each(lambda a_, t_, w_: pair_dot2(a_, t_, w_), a_rb, a_t, w_bf16)
    r_t = _each(lambda r_, x_: bf(r_ + x_[:, :LANE]), r_hat, rb_at_w)
    y0 = _each(lambda x_, y_: x_[:, LANE:] + y_[T:], rb_at_w, ak_rk_v)
    same_head, eye128 = cst["same_head"], cst["eye128"]
    atw_b = _each(lambda x_, b_: _dot_tn(x_, b_), at_w, b_end)
    trans = _each(lambda l_, x_: bf(jnp.where(eye128, jnp.exp(l_), 0.0) + jnp.where(same_head, x_[:LANE], 0.0)),
                  cl_last, atw_b)
    s_add = _each(lambda x_, v_, k_: jnp.where(same_head, x_[LANE:] + _dot_tn(v_, k_), 0.0),
                  atw_b, v_bf16, k_end)
    bonus = _each(lambda r_, k_, rk_, v_: _head_sum(r_ * k_ * rk_, even) * v_, r, k, rk, v)
    return trans, s_add, r_t, y0, bonus


def _wkv_kernel(r_ref, k_ref, v_ref, wl_ref, a_ref, gate_ref, kk_ref, ka_ref, rk_ref, lg_ref, lb_ref, o_ref,
                state_ref, trans_ref, sadd_ref, rt_ref, y0_ref, bonus_ref, *, pairs, unroll):
    @pl.when(pl.program_id(2) == 0)
    def _():
        state_ref[...] = jnp.zeros_like(state_ref)

    T = WKV_CHUNK
    n_chunks = WKV_TIME_BLOCK // T
    cst = _wkv_pair_consts()
    even = cst["even"]
    pair_cols = [slice(pp * LANE, (pp + 1) * LANE) for pp in range(pairs)]

    def prepare(group):
        chunk_ids = [group * unroll + cc for cc in range(unroll)]
        problems = [(cc, pp) for cc in range(unroll) for pp in range(pairs)]

        def tiles(ref):
            return [ref[0, pl.ds(pl.multiple_of(chunk_ids[cc] * T, T), T), pair_cols[pp]].astype(F32)
                    for cc, pp in problems]

        def vecs(ref):
            return [ref[:, pair_cols[pp]] for _, pp in problems]

        outs = _wkv_prepare(tiles(r_ref), tiles(k_ref), tiles(v_ref), tiles(wl_ref), tiles(a_ref),
                            vecs(kk_ref), vecs(ka_ref), vecs(rk_ref), cst)
        for ref, vals in zip((trans_ref, sadd_ref, rt_ref, y0_ref, bonus_ref), outs):
            for (cc, pp), val in zip(problems, vals):
                ref[chunk_ids[cc], pp] = val

    def scan(group):
        for cc in range(unroll):
            c = group * unroll + cc
            rows = pl.ds(pl.multiple_of(c * T, T), T)
            state_bf16 = [state_ref[pp].astype(BF16) for pp in range(pairs)]
            y = [_dot_nt(rt_ref[c, pp], state_bf16[pp]) + y0_ref[c, pp] for pp in range(pairs)]
            new_state = [_dot(state_bf16[pp], trans_ref[c, pp]) + sadd_ref[c, pp] for pp in range(pairs)]
            for pp in range(pairs):
                state_ref[pp] = new_state[pp]
            mu = _each(lambda y_: _head_sum(y_, even) * (1.0 / RWKV_HEAD_DIM), y)
            yc = _each(lambda y_, m_: y_ - m_, y, mu)
            var = _each(lambda c_: _head_sum(c_ * c_, even) * (1.0 / RWKV_HEAD_DIM), yc)
            for pp in range(pairs):
                out = (yc[pp] * lax.rsqrt(var[pp] + RWKV_GN_EPS) * lg_ref[:, pair_cols[pp]]
                       + lb_ref[:, pair_cols[pp]] + bonus_ref[c, pp])
                gated = out * gate_ref[0, rows, pair_cols[pp]].astype(F32)
                o_ref[0, rows, pair_cols[pp]] = gated.astype(o_ref.dtype)

    n_groups = n_chunks // unroll
    prepare(0)

    def pipelined(group, _):
        scan(group - 1)
        prepare(group)
        return 0

    lax.fori_loop(1, n_groups, pipelined, 0)
    scan(n_groups - 1)


def _wkv(r, k, v, wl, a, gate, k_k, k_a, r_k, lnx_g, lnx_b, pairs=2, unroll=8):
    b, s, d = r.shape
    tb = WKV_TIME_BLOCK
    width = pairs * LANE
    n_chunks = tb // WKV_CHUNK
    tile = pl.BlockSpec((1, tb, width), lambda bi, h, t: (bi, t, h))
    vec = pl.BlockSpec((1, width), lambda bi, h, t: (0, h))
    per_chunk = lambda rows, dt: pltpu.VMEM((n_chunks, pairs, rows, LANE), dt)
    kern = functools.partial(_wkv_kernel, pairs=pairs, unroll=unroll)
    return pl.pallas_call(
        kern,
        grid=(b, d // width, s // tb),
        in_specs=[tile] * 6 + [vec] * 5,
        out_specs=tile,
        out_shape=jax.ShapeDtypeStruct((b, s, d), BF16),
        scratch_shapes=[pltpu.VMEM((pairs, LANE, LANE), F32),
                        per_chunk(LANE, BF16), per_chunk(LANE, F32),
                        per_chunk(WKV_CHUNK, BF16), per_chunk(WKV_CHUNK, F32), per_chunk(WKV_CHUNK, F32)],
        compiler_params=_cparams(("parallel", "parallel", "arbitrary")),
        name="wkv7",
    )(r, k, v, wl, a, gate, *(t.reshape(1, -1) for t in (k_k, k_a, r_k, lnx_g, lnx_b)))


def kernel(x, ev_w_in, ev_lambda, ev_subln_g, ev_w_out, od_mu, od_w_rkv, od_w0, od_w1, od_w2, od_a0, od_a1, od_a2, od_g1, od_g2, od_k_k, od_k_a, od_r_k, od_lnx_g, od_lnx_b, od_w_out, ln_mix_g, ln_mix_b, ffn_w_up, ffn_conv_w, ffn_conv_b, ffn_w_down, ln_ffn_g, ln_ffn_b):
    b, s, d = x.shape
    m = b * s
    bf = lambda t: t.astype(BF16)

    q_cols = DIFF_HEADS * 2 * DIFF_HEAD_DIM
    col_scale = jnp.where(jnp.arange(EVEN_IN_WIDTH) < q_cols, DIFF_HEAD_DIM ** -0.5 * LOG2_E, 1.0)
    qkv, gate = _in_proj(x.reshape(m, d), bf(ev_w_in[0] * col_scale))
    qkv = qkv.reshape(b, s, EVEN_QKV_WIDTH)
    lambda_init = 0.8 - 0.6 * math.exp(-0.3 * 0)
    a_out = _diff_attention(qkv, ev_lambda[0], ev_subln_g[0], lambda_init)
    b_out = _retention(qkv, gate.reshape(b, s, RET_GATE_WIDTH))
    x2 = _layer_tail([a_out, b_out], x, bf(ev_w_out[0]), ln_mix_g[0], ln_mix_b[0], bf(ffn_w_up[0]),
                     ffn_conv_w[0], ffn_conv_b[0], bf(ffn_w_down[0]), ln_ffn_g[0], ln_ffn_b[0])

    pad = GATE_LORA_PAD - GATE_LORA
    g1 = jnp.pad(bf(od_g1[0]), ((0, 0), (0, pad)))
    g2 = jnp.pad(bf(od_g2[0]), ((0, pad), (0, 0)))
    r, k, v, wl, a, g = _rwkv_proj(x2, od_mu[0], bf(od_w_rkv[0]), od_w0[0], bf(od_w1[0]), bf(od_w2[0]),
                                   od_a0[0], bf(od_a1[0]), bf(od_a2[0]), g1, g2)
    y = _wkv(r, k, v, wl, a, g, od_k_k[0], od_k_a[0], od_r_k[0], od_lnx_g[0], od_lnx_b[0])
    x4 = _layer_tail([y], x2, bf(od_w_out[0]), ln_mix_g[1], ln_mix_b[1], bf(ffn_w_up[1]),
                     ffn_conv_w[1], ffn_conv_b[1], bf(ffn_w_down[1]), ln_ffn_g[1], ln_ffn_b[1])
    return x4
```

```python
import functools
import math

import jax
import jax.numpy as jnp
from jax import lax
from jax.experimental import pallas as pl
from jax.experimental.pallas import tpu as pltpu

F32 = jnp.float32
BF16 = jnp.bfloat16

D_MODEL = 1024
DEPTH = 2
DN_ALPHA = (2.0 * DEPTH) ** 0.25
LN_EPS = 1e-5

DIFF_HEAD_DIM = 64
LOG2_E = math.log2(math.e)
DIFF_HEADS = 4
RET_QK_DIM = 64
RET_V_DIM = 128
RET_HEADS = 4
RET_CHUNK = 128
EVEN_IN_WIDTH = 3072
EVEN_QKV_WIDTH = 2560
RET_GATE_WIDTH = 512

RWKV_HEAD_DIM = 64
RWKV_GN_EPS = 64e-5
GATE_LORA = 160
GATE_LORA_PAD = 256
WKV_CHUNK = 64
WKV_INV_BASE = 8
WKV_TIME_BLOCK = 2048

FFN_HIDDEN = 2816
FFN_CHUNK = 256
CONV_HALO = 16

LANE = 128
VMEM_LIMIT = 56 * 1024 * 1024


def _cparams(sem):
    return pltpu.CompilerParams(dimension_semantics=sem, vmem_limit_bytes=VMEM_LIMIT)


def _dot(a, b):
    return jnp.dot(a, b, preferred_element_type=F32)


def _dot_nt(a, b):
    return lax.dot_general(a, b, (((1,), (1,)), ((), ())), preferred_element_type=F32)


def _dot_tn(a, b):
    return lax.dot_general(a, b, (((0,), (0,)), ((), ())), preferred_element_type=F32)


def _layer_norm(z, g, b):
    mu = jnp.mean(z, axis=-1, keepdims=True)
    zc = z - mu
    var = jnp.mean(zc * zc, axis=-1, keepdims=True)
    return zc * lax.rsqrt(var + LN_EPS) * g + b


def _in_proj_kernel(x_ref, w_ref, qkv_ref, gate_ref):
    h = _dot(x_ref[...].astype(BF16), w_ref[...])
    qkv_ref[...] = h[:, :EVEN_QKV_WIDTH].astype(BF16)
    gate_ref[...] = h[:, EVEN_QKV_WIDTH:]


def _in_proj(x2d, w_bf16, tm=512):
    m = x2d.shape[0]
    return pl.pallas_call(
        _in_proj_kernel,
        grid=(m // tm,),
        in_specs=[pl.BlockSpec((tm, D_MODEL), lambda i: (i, 0)),
                  pl.BlockSpec((D_MODEL, EVEN_IN_WIDTH), lambda i: (0, 0))],
        out_specs=[pl.BlockSpec((tm, EVEN_QKV_WIDTH), lambda i: (i, 0)),
                   pl.BlockSpec((tm, RET_GATE_WIDTH), lambda i: (i, 0))],
        out_shape=[jax.ShapeDtypeStruct((m, EVEN_QKV_WIDTH), BF16),
                   jax.ShapeDtypeStruct((m, RET_GATE_WIDTH), F32)],
        compiler_params=_cparams(("parallel",)),
        name="in_proj",
    )(x2d, w_bf16)


def _diff_attn_kernel(lam_ref, g_ref, q_ref, k_ref, v_ref, o_ref, sa_ref, sb_ref, m_ref, l_ref, acc_ref,
                      *, tq, lambda_init):
    h = pl.program_id(1)
    qi = pl.program_id(2)
    tk = tq
    slope = jnp.where(h == 0, 2.0 ** -2, jnp.where(h == 1, 2.0 ** -4,
                      jnp.where(h == 2, 2.0 ** -6, 2.0 ** -8))).astype(F32)
    lp = lam_ref[...]
    lam = (jnp.exp(jnp.sum(lp[0:1] * lp[1:2], axis=-1, keepdims=True))
           - jnp.exp(jnp.sum(lp[2:3] * lp[3:4], axis=-1, keepdims=True)) + lambda_init)

    lane = lax.broadcasted_iota(jnp.int32, (1, LANE), 1)
    q = q_ref[0]
    zero = jnp.zeros_like(q)
    slope = slope * LOG2_E
    c = jnp.full((1, LANE), slope, F32)
    c_hi = c.astype(BF16).astype(F32)
    q_pos = jnp.broadcast_to(jnp.where(lane < 2, c_hi, jnp.where(lane < 4, c - c_hi, 0.0)).astype(BF16),
                             (tq, LANE))
    q12 = jnp.concatenate([jnp.concatenate([jnp.where(lane < DIFF_HEAD_DIM, q, zero), q_pos], axis=1),
                           jnp.concatenate([jnp.where(lane >= DIFF_HEAD_DIM, q, zero), q_pos], axis=1)],
                          axis=0)
    q12_t = q12.astype(F32).T.astype(BF16)
    key_off = lax.broadcasted_iota(jnp.int32, (tk, LANE), 0)
    k_pos = jnp.where((lane == 0) | (lane == 2), key_off & 255,
                      jnp.where((lane == 1) | (lane == 3), key_off & ~255, 0)).astype(F32).astype(BF16)

    def scores_t(j, dst_ref):
        start = pl.multiple_of(j * tk, tk)
        k = jnp.concatenate([k_ref[0, pl.ds(start, tk), :], k_pos], axis=1)
        dst_ref[...] = _dot(k, q12_t)

    def update(src_ref, j, diagonal):
        s_t = src_ref[...]
        if diagonal:
            key_i = lax.broadcasted_iota(jnp.int32, (tk, 1), 0)
            query_i = lax.broadcasted_iota(jnp.int32, (1, 2 * tq), 1) % tq
            s_t = jnp.where(key_i <= query_i, s_t, -1e30)
        shift = ((j - qi) * tk).astype(F32) * slope
        v = v_ref[0, pl.ds(pl.multiple_of(j * tk, tk), tk), :]
        m = m_ref[...]
        m_new = jnp.maximum(m, jnp.max(s_t, axis=0, keepdims=True) + shift)
        alpha = jnp.exp2(m - m_new)
        p_t = jnp.exp2(s_t - (m_new - shift))
        m_ref[...] = m_new
        l_ref[...] = alpha * l_ref[...] + jnp.sum(p_t, axis=0, keepdims=True)
        acc_ref[...] = alpha * acc_ref[...] + _dot_tn(v, p_t.astype(BF16))

    m_ref[...] = jnp.full(m_ref.shape, -1e30, F32)
    l_ref[...] = jnp.zeros(l_ref.shape, F32)
    acc_ref[...] = jnp.zeros(acc_ref.shape, F32)

    scores_t(0, sa_ref)

    def pair_body(jj, _):
        j0 = 2 * jj
        scores_t(j0 + 1, sb_ref)
        update(sa_ref, j0, False)
        scores_t(j0 + 2, sa_ref)
        update(sb_ref, j0 + 1, False)
        return 0

    lax.fori_loop(0, qi // 2, pair_body, 0)

    @pl.when(qi % 2 == 1)
    def _():
        scores_t(qi, sb_ref)
        update(sa_ref, qi - 1, False)
        update(sb_ref, qi, True)

    @pl.when(qi % 2 == 0)
    def _():
        update(sa_ref, qi, True)

    l = l_ref[...]
    acc_t = acc_ref[...]
    o_t = acc_t[:, :tq] / l[:, :tq] - lam * (acc_t[:, tq:] / l[:, tq:])
    o_t = o_t * lax.rsqrt(jnp.mean(o_t * o_t, axis=0, keepdims=True) + LN_EPS)
    o_ref[0] = (o_t.T * g_ref[...] * (1.0 - lambda_init)).astype(o_ref.dtype)


def _diff_attention(qkv, lam_p, subln_g, lambda_init, tq=512):
    b, s, _ = qkv.shape
    kern = functools.partial(_diff_attn_kernel, tq=tq, lambda_init=lambda_init)
    return pl.pallas_call(
        kern,
        grid=(b, DIFF_HEADS, s // tq),
        in_specs=[pl.BlockSpec((4, DIFF_HEAD_DIM), lambda bi, h, i: (0, 0)),
                  pl.BlockSpec((1, LANE), lambda bi, h, i: (0, 0)),
                  pl.BlockSpec((1, tq, LANE), lambda bi, h, i: (bi, i, h)),
                  pl.BlockSpec((1, s, LANE), lambda bi, h, i: (bi, 0, DIFF_HEADS + h)),
                  pl.BlockSpec((1, s, LANE), lambda bi, h, i: (bi, 0, 2 * DIFF_HEADS + h))],
        out_specs=pl.BlockSpec((1, tq, LANE), lambda bi, h, i: (bi, i, h)),
        out_shape=jax.ShapeDtypeStruct((b, s, DIFF_HEADS * LANE), BF16),
        scratch_shapes=[pltpu.VMEM((tq, 2 * tq), F32), pltpu.VMEM((tq, 2 * tq), F32),
                        pltpu.VMEM((1, 2 * tq), F32), pltpu.VMEM((1, 2 * tq), F32),
                        pltpu.VMEM((LANE, 2 * tq), F32)],
        compiler_params=_cparams(("parallel", "parallel", "arbitrary")),
        name="diff_attn",
    )(lam_p, subln_g.reshape(1, LANE), qkv, qkv, qkv)


def _retention_kernel(q_ref, k_ref, v_ref, g_ref, o_ref, state_ref, *, chunks_per_step):
    C = RET_CHUNK

    @pl.when(pl.program_id(1) == 0)
    def _():
        state_ref[...] = jnp.zeros_like(state_ref)

    lane = lax.broadcasted_iota(jnp.int32, (1, LANE), 1)
    row128 = lax.broadcasted_iota(jnp.int32, (LANE, 1), 0)
    ri = lax.broadcasted_iota(jnp.int32, (C, C), 0)
    ci = lax.broadcasted_iota(jnp.int32, (C, C), 1)
    rel = (ri - ci).astype(F32)
    idx = lax.broadcasted_iota(jnp.int32, (C, 1), 0).astype(F32)
    heads = range(RET_HEADS)
    log_gamma = [math.log1p(-(2.0 ** (-5.0 - h))) for h in heads]
    in_head = [(lane >= (h % 2) * RET_QK_DIM) & (lane < (h % 2 + 1) * RET_QK_DIM) for h in heads]
    in_rows = [(row128 >= (h % 2) * RET_QK_DIM) & (row128 < (h % 2 + 1) * RET_QK_DIM) for h in heads]
    decay = [jnp.where(rel >= 0, jnp.exp(lg * jnp.maximum(rel, 0.0)), 0.0) for lg in log_gamma]
    q_decay = [jnp.exp(lg * (idx + 1.0)) for lg in log_gamma]
    k_decay = [jnp.exp(lg * (C - 1.0 - idx)) for lg in log_gamma]
    v_cols = [slice(h * RET_V_DIM, (h + 1) * RET_V_DIM) for h in heads]

    group = 2
    problems = [(cc, h) for cc in range(group) for h in heads]

    def group_body(gi, _):
        rows = [pl.ds(pl.multiple_of((gi * group + cc) * C, C), C) for cc in range(group)]
        q_all = [q_ref[0, r, :].astype(F32) for r in rows]
        k_all = [k_ref[0, r, :].astype(F32) * (RET_QK_DIM ** -0.5) for r in rows]
        q = [jnp.where(in_head[h], q_all[cc][:, (h // 2) * LANE:(h // 2 + 1) * LANE], 0.0) for cc, h in problems]
        k = [k_all[cc][:, (h // 2) * LANE:(h // 2 + 1) * LANE] for cc, h in problems]
        v = [v_ref[0, rows[cc], v_cols[h]] for cc, h in problems]
        sc = [_dot_nt(q_.astype(BF16), k_.astype(BF16)) * decay[h] for q_, k_, (_, h) in zip(q, k, problems)]
        inner = [_dot(s_.astype(BF16), v_) for s_, v_ in zip(sc, v)]
        upd = [_dot_tn((k_ * k_decay[h]).astype(BF16), v_) for k_, v_, (_, h) in zip(k, v, problems)]
        q_dec = [(q_ * q_decay[h]).astype(BF16) for q_, (_, h) in zip(q, problems)]
        state = [state_ref[h] for h in heads]
        cross = []
        for i, (cc, h) in enumerate(problems):
            cross.append(_dot(q_dec[i], state[h].astype(BF16)))
            state[h] = state[h] * math.exp(log_gamma[h] * C) + jnp.where(in_rows[h], upd[i], 0.0)
        for h in heads:
            state_ref[h] = state[h]
        y = [i_ + c_ for i_, c_ in zip(inner, cross)]
        mu = [jnp.mean(y_, axis=-1, keepdims=True) for y_ in y]
        yc = [y_ - m_ for y_, m_ in zip(y, mu)]
        var = [jnp.mean(c_ * c_, axis=-1, keepdims=True) for c_ in yc]
        for i, (cc, h) in enumerate(problems):
            g = g_ref[0, rows[cc], v_cols[h]]
            o_ref[0, rows[cc], v_cols[h]] = (g * jax.nn.sigmoid(g)
                                             * (yc[i] * lax.rsqrt(var[i] + LN_EPS))).astype(o_ref.dtype)
        return 0

    lax.fori_loop(0, chunks_per_step // group, group_body, 0)


def _retention(qkv, gate, chunks_per_step=4):
    b, s, _ = qkv.shape
    C = RET_CHUNK * chunks_per_step
    qk_w = RET_HEADS * RET_QK_DIM
    v_w = RET_HEADS * RET_V_DIM
    return pl.pallas_call(
        functools.partial(_retention_kernel, chunks_per_step=chunks_per_step),
        grid=(b, s // C),
        in_specs=[pl.BlockSpec((1, C, qk_w), lambda bi, c: (bi, c, 1536 // qk_w)),
                  pl.BlockSpec((1, C, qk_w), lambda bi, c: (bi, c, 1792 // qk_w)),
                  pl.BlockSpec((1, C, v_w), lambda bi, c: (bi, c, 2048 // v_w)),
                  pl.BlockSpec((1, C, v_w), lambda bi, c: (bi, c, 0))],
        out_specs=pl.BlockSpec((1, C, v_w), lambda bi, c: (bi, c, 0)),
        out_shape=jax.ShapeDtypeStruct((b, s, v_w), BF16),
        scratch_shapes=[pltpu.VMEM((RET_HEADS, LANE, RET_V_DIM), F32)],
        compiler_params=_cparams(("parallel", "arbitrary")),
        name="retention",
    )(qkv, qkv, qkv, gate)


def _layer_tail_kernel(*refs, tm, n_pieces):
    piece_refs = [refs[3 * p:3 * p + 3] for p in range(n_pieces)]
    (x_ref, xhalo_ref, mg_ref, mb_ref, wu_ref, cw_ref, cb_ref, wd_ref, fg_ref, fb_ref,
     o_ref, act_ref) = refs[3 * n_pieces:]
    i = pl.program_id(1)
    mix = None
    for tile_ref, halo_ref, w_ref in piece_refs:
        part = _dot(jnp.concatenate([halo_ref[0], tile_ref[0]], axis=0), w_ref[...])
        mix = part if mix is None else mix + part
    xe = jnp.concatenate([xhalo_ref[0], x_ref[0]], axis=0)
    x1e = _layer_norm(DN_ALPHA * xe + mix, mg_ref[...], mb_ref[...])
    x1 = x1e[CONV_HALO:]
    xe = jnp.concatenate([jnp.where(i > 0, x1e[:CONV_HALO], 0.0), x1], axis=0).astype(BF16)
    n_chunks = FFN_HIDDEN // FFN_CHUNK
    for c in range(n_chunks):
        lo = c * FFN_CHUNK
        ue = _dot(xe, wu_ref[:, lo:lo + FFN_CHUNK])
        gate = _dot(xe[CONV_HALO:], wu_ref[:, FFN_HIDDEN + lo:FFN_HIDDEN + lo + FFN_CHUNK])
        cw = cw_ref[:, lo:lo + FFN_CHUNK]
        conv = (cb_ref[:, lo:lo + FFN_CHUNK]
                + ue[CONV_HALO - 2:CONV_HALO - 2 + tm] * cw[0:1]
                + ue[CONV_HALO - 1:CONV_HALO - 1 + tm] * cw[1:2]
                + ue[CONV_HALO:] * cw[2:3])
        act = 0.5 * conv * (1.0 + lax.erf(conv * (2.0 ** -0.5)))
        act_ref[:, lo:lo + FFN_CHUNK] = (act * gate).astype(BF16)
    ffn = _dot(act_ref[...], wd_ref[...])
    o_ref[0] = _layer_norm(DN_ALPHA * x1 + ffn, fg_ref[...], fb_ref[...])


def _layer_tail(pieces, x, w_out_bf16, ln_mix_g, ln_mix_b, w_up_bf16, conv_w, conv_b, w_down_bf16,
                ln_ffn_g, ln_ffn_b, tm=512):
    b, s, _ = x.shape
    assert len({p.shape[-1] for p in pieces}) == 1 and sum(p.shape[-1] for p in pieces) == D_MODEL
    kern = functools.partial(_layer_tail_kernel, tm=tm, n_pieces=len(pieces))
    fixed = lambda bi, i: (0, 0)
    tile = lambda bi, i: (bi, i, 0)
    halo = lambda bi, i: (bi, jnp.maximum(i * (tm // CONV_HALO) - 1, 0), 0)
    single = pl.Buffered(1)
    in_specs, args = [], []
    for row_block, piece in enumerate(pieces):
        width = piece.shape[-1]
        in_specs += [pl.BlockSpec((1, tm, width), tile), pl.BlockSpec((1, CONV_HALO, width), halo),
                     pl.BlockSpec((width, D_MODEL), functools.partial(lambda bi, i, r: (r, 0), r=row_block),
                                  pipeline_mode=single)]
        args += [piece, piece, w_out_bf16]
    in_specs += [pl.BlockSpec((1, tm, D_MODEL), tile), pl.BlockSpec((1, CONV_HALO, D_MODEL), halo),
                 pl.BlockSpec((1, D_MODEL), fixed), pl.BlockSpec((1, D_MODEL), fixed),
                 pl.BlockSpec((D_MODEL, 2 * FFN_HIDDEN), fixed, pipeline_mode=single),
                 pl.BlockSpec((3, FFN_HIDDEN), fixed),
                 pl.BlockSpec((1, FFN_HIDDEN), fixed),
                 pl.BlockSpec((FFN_HIDDEN, D_MODEL), fixed, pipeline_mode=single),
                 pl.BlockSpec((1, D_MODEL), fixed), pl.BlockSpec((1, D_MODEL), fixed)]
    args += [x, x, ln_mix_g.reshape(1, -1), ln_mix_b.reshape(1, -1), w_up_bf16, conv_w, conv_b.reshape(1, -1),
             w_down_bf16, ln_ffn_g.reshape(1, -1), ln_ffn_b.reshape(1, -1)]
    return pl.pallas_call(
        kern,
        grid=(b, s // tm),
        in_specs=in_specs,
        out_specs=pl.BlockSpec((1, tm, D_MODEL), tile),
        out_shape=jax.ShapeDtypeStruct((b, s, D_MODEL), F32),
        scratch_shapes=[pltpu.VMEM((tm, FFN_HIDDEN), BF16)],
        compiler_params=_cparams(("parallel", "arbitrary")),
        name="layer_tail",
    )(*args)


def _rwkv_proj_kernel(x_ref, halo_ref, mu_ref, wrkv_ref, w0_ref, w1_ref, w2_ref, a0_ref, a1_ref, a2_ref,
                      g1_ref, g2_ref, r_ref, k_ref, v_ref, wl_ref, a_ref, g_ref, *, tm):
    i = pl.program_id(1)
    x = x_ref[0]
    prev_row = jnp.where(i > 0, halo_ref[0, 7:8, :], 0.0)
    row = lax.broadcasted_iota(jnp.int32, (tm, 1), 0)
    x_prev = jnp.where(row == 0, prev_row, pltpu.roll(x, 1, 0))
    xx = x_prev - x

    def mix(n):
        return (x + xx * mu_ref[n:n + 1, :]).astype(BF16)

    r_ref[0] = _dot(mix(0), wrkv_ref[0]).astype(r_ref.dtype)
    k_ref[0] = _dot(mix(1), wrkv_ref[1]).astype(k_ref.dtype)
    v_ref[0] = _dot(mix(2), wrkv_ref[2]).astype(v_ref.dtype)
    lw = jnp.tanh(_dot(mix(3), w1_ref[...])).astype(BF16)
    u = w0_ref[...] + _dot(lw, w2_ref[...])
    wl_ref[0] = -math.exp(-0.5) * jax.nn.sigmoid(u)
    la = _dot(mix(4), a1_ref[...]).astype(BF16)
    a_ref[0] = jax.nn.sigmoid(a0_ref[...] + _dot(la, a2_ref[...])).astype(a_ref.dtype)
    lg = jax.nn.sigmoid(_dot(mix(5), g1_ref[...])).astype(BF16)
    g_ref[0] = _dot(lg, g2_ref[...]).astype(g_ref.dtype)


def _rwkv_proj(x, mu, w_rkv, w0, w1, w2, a0, a1, a2, g1, g2, tm=512):
    b, s, d = x.shape
    kern = functools.partial(_rwkv_proj_kernel, tm=tm)
    fixed2 = lambda bi, i: (0, 0)
    tile = pl.BlockSpec((1, tm, d), lambda bi, i: (bi, i, 0))
    lora = w1.shape[1]
    out = jax.ShapeDtypeStruct((b, s, d), F32)
    half = jax.ShapeDtypeStruct((b, s, d), BF16)
    return pl.pallas_call(
        kern,
        grid=(b, s // tm),
        in_specs=[tile,
                  pl.BlockSpec((1, 8, d), lambda bi, i: (bi, jnp.maximum(i * (tm // 8) - 1, 0), 0)),
                  pl.BlockSpec((6, d), fixed2),
                  pl.BlockSpec((3, d, d), lambda bi, i: (0, 0, 0)),
                  pl.BlockSpec((1, d), fixed2), pl.BlockSpec((d, lora), fixed2), pl.BlockSpec((lora, d), fixed2),
                  pl.BlockSpec((1, d), fixed2), pl.BlockSpec((d, lora), fixed2), pl.BlockSpec((lora, d), fixed2),
                  pl.BlockSpec((d, GATE_LORA_PAD), fixed2), pl.BlockSpec((GATE_LORA_PAD, d), fixed2)],
        out_specs=[tile] * 6,
        out_shape=[half, half, half, out, half, half],
        compiler_params=_cparams(("parallel", "arbitrary")),
        name="rwkv_proj",
    )(x, x, mu, w_rkv, w0.reshape(1, -1), w1, w2, a0.reshape(1, -1), a1, a2, g1, g2)


def _each(f, *lists):
    return [f(*xs) for xs in zip(*lists)]


def _wkv_pair_consts():
    T = WKV_CHUNK
    lane = lax.broadcasted_iota(jnp.int32, (1, LANE), 1)
    row = lax.broadcasted_iota(jnp.int32, (T, LANE), 0)
    col = lax.broadcasted_iota(jnp.int32, (T, LANE), 1) % T
    r2 = lax.broadcasted_iota(jnp.int32, (LANE, LANE), 0)
    c2 = lax.broadcasted_iota(jnp.int32, (LANE, LANE), 1)
    tri_r = lax.broadcasted_iota(jnp.int32, (T, T), 0)
    tri_c = lax.broadcasted_iota(jnp.int32, (T, T), 1)
    same_head = (r2 // T) == (c2 // T)
    merges = []
    size = 2 * WKV_INV_BASE
    while size <= T:
        merges.append(((row // size) == (col // size)) & ((row // (size // 2)) != (col // (size // 2))))
        size *= 2
    return dict(
        even=lane < T,
        strict=col < row, incl=col <= row, eye=col == row,
        in_base=(row // WKV_INV_BASE) == (col // WKV_INV_BASE),
        merges=merges,
        same_head=same_head, eye128=r2 == c2,
        tri=jnp.where(tri_c <= tri_r, 1.0, 0.0).astype(BF16),
    )


def _head_sum(x, even):
    s_even = jnp.sum(jnp.where(even, x, 0.0), axis=-1, keepdims=True)
    s_odd = jnp.sum(jnp.where(even, 0.0, x), axis=-1, keepdims=True)
    return jnp.where(even, s_even, s_odd)


def _wkv_prepare(r, k, v, wl, lr, kk_scale, ka, rk, cst):
    T = WKV_CHUNK
    bf = lambda t: t.astype(BF16)
    even, strict, incl, eye = cst["even"], cst["strict"], cst["incl"], cst["eye"]
    tri_incl_bf16 = cst["tri"]

    def stack2(x):
        zero = jnp.zeros_like(x)
        return jnp.concatenate([jnp.where(even, x, zero), jnp.where(even, zero, x)], axis=0)

    def pair_dot(packed, x):
        return _dot(packed, stack2(x))

    def pair_dot2(packed, x1, x2):
        return _dot(packed, jnp.concatenate([stack2(x1), stack2(x2)], axis=1))

    kk = _each(lambda k_, s_: k_ * s_, k, kk_scale)
    ss = _each(lambda t: _head_sum(t * t, even), kk)
    kk = _each(lambda t, s_: t / jnp.maximum(jnp.sqrt(s_), 1e-12), kk, ss)
    k = _each(lambda k_, lr_, ka_: k_ * (1.0 + (lr_ - 1.0) * ka_), k, lr, ka)
    b_vec = _each(lambda kk_, lr_: kk_ * lr_, kk, lr)

    w_hi = _each(bf, wl)
    w_r1 = _each(lambda w_, h_: w_ - h_.astype(F32), wl, w_hi)
    w_mid = _each(bf, w_r1)
    w_lo = _each(lambda r1, m_: (r1 - m_.astype(F32)).astype(BF16), w_r1, w_mid)
    cl = _each(lambda h_, m_, l_: _dot(tri_incl_bf16, h_) + (_dot(tri_incl_bf16, m_) + _dot(tri_incl_bf16, l_)),
               w_hi, w_mid, w_lo)
    cl_last = _each(lambda c_: c_[T - 1:T, :], cl)
    g_inv = _each(lambda c_: jnp.exp(-c_), cl)
    a_hat = _each(lambda kk_, c_, w_: -kk_ * jnp.exp(c_ - w_), kk, cl, wl)
    b_hat = _each(lambda b_, g_: bf(b_ * g_), b_vec, g_inv)
    k_hat = _each(lambda k_, g_: bf(k_ * g_), k, g_inv)
    r_hat = _each(lambda r_, c_: r_ * jnp.exp(c_), r, cl)
    to_end = _each(lambda l_, c_: jnp.exp(l_ - c_), cl_last, cl)
    b_end = _each(lambda b_, e_: bf(b_ * e_), b_vec, to_end)
    k_end = _each(lambda k_, e_: bf(k_ * e_), k, to_end)

    ar = _each(lambda a_, r_: bf(jnp.concatenate([a_, r_], axis=0)), a_hat, r_hat)
    m_bk = _each(lambda ar_, b_, k_: _dot_nt(ar_, jnp.concatenate([stack2(b_), stack2(k_)], axis=0)),
                 ar, b_hat, k_hat)
    m_b = _each(lambda m_: m_[:, :LANE], m_bk)
    m_k = _each(lambda m_: m_[:, LANE:], m_bk)
    a_ab = _each(lambda m_: jnp.where(strict, m_[:T], 0.0), m_b)
    a_rb = _each(lambda m_: bf(jnp.where(incl, m_[T:], 0.0)), m_b)
    a_ak = _each(lambda m_: bf(jnp.where(strict, m_[:T], 0.0)), m_k)
    a_rk = _each(lambda m_: bf(jnp.where(incl, m_[T:], 0.0)), m_k)

    p = _each(lambda a_: jnp.where(cst["in_base"], a_, 0.0), a_ab)
    minv = _each(lambda p_: jnp.where(eye, 1.0, 0.0) + p_, p)
    levels = int(math.log2(WKV_INV_BASE)) - 1
    p = _each(lambda p_: pair_dot(bf(p_), bf(p_)), p)
    for level in range(levels):
        if level + 1 < levels:
            both = _each(lambda p_, m_: pair_dot2(bf(p_), bf(m_), bf(p_)), p, minv)
            minv = _each(lambda m_, x_: m_ + x_[:, :LANE], minv, both)
            p = _each(lambda x_: x_[:, LANE:], both)
        else:
            minv = _each(lambda m_, p_: m_ + pair_dot(bf(p_), bf(m_)), minv, p)
    for off in cst["merges"]:
        a_off = _each(lambda a_: bf(jnp.where(off, a_, 0.0)), a_ab)
        minv = _each(lambda m_, a_: m_ + pair_dot(bf(m_), bf(pair_dot(a_, bf(m_)))), minv, a_off)

    v_bf16 = _each(bf, v)
    minv_bf16 = _each(bf, minv)
    ak_rk_v = _each(lambda ak_, rk_, v_: pair_dot(jnp.concatenate([ak_, rk_], axis=0), v_), a_ak, a_rk, v_bf16)
    akv = _each(lambda x_: bf(x_[:T]), ak_rk_v)
    at_w = _each(lambda m_, a_, x_: bf(pair_dot2(m_, a_[:T], x_)), minv_bf16, ar, akv)
    a_t = _each(lambda x_: x_[:, :LANE], at_w)
    w_bf16 = _each(lambda x_: x_[:, LANE:], at_w)
    rb_at_w = _each(lambda a_, t_, w_: pair_dot2(a_, t_, w_), a_rb, a_t, w_bf16)
    r_t = _each(lambda r_, x_: bf(r_ + x_[:, :LANE]), r_hat, rb_at_w)
    y0 = _each(lambda x_, y_: x_[:, LANE:] + y_[T:], rb_at_w, ak_rk_v)
    same_head, eye128 = cst["same_head"], cst["eye128"]
    b_atw = _each(lambda b_, x_: _dot_tn(b_, x_), b_end, at_w)
    trans = _each(lambda l_, x_: bf(jnp.where(eye128, jnp.exp(l_), 0.0) + jnp.where(same_head, x_[:, :LANE], 0.0)),
                  cl_last, b_atw)
    s_add = _each(lambda x_, k_, v_: jnp.where(same_head, x_[:, LANE:] + _dot_tn(k_, v_), 0.0),
                  b_atw, k_end, v_bf16)
    bonus = _each(lambda r_, k_, rk_, v_: _head_sum(r_ * k_ * rk_, even) * v_, r, k, rk, v)
    return trans, s_add, r_t, y0, bonus


def _wkv_kernel(r_ref, k_ref, v_ref, wl_ref, a_ref, gate_ref, kk_ref, ka_ref, rk_ref, lg_ref, lb_ref, o_ref,
                state_ref, trans_ref, sadd_ref, rt_ref, y0_ref, bonus_ref, *, pairs, unroll):
    @pl.when(pl.program_id(2) == 0)
    def _():
        state_ref[...] = jnp.zeros_like(state_ref)

    T = WKV_CHUNK
    n_chunks = WKV_TIME_BLOCK // T
    cst = _wkv_pair_consts()
    even = cst["even"]
    pair_cols = [slice(pp * LANE, (pp + 1) * LANE) for pp in range(pairs)]

    def prepare(group):
        chunk_ids = [group * unroll + cc for cc in range(unroll)]
        problems = [(cc, pp) for cc in range(unroll) for pp in range(pairs)]

        def tiles(ref):
            return [ref[0, pl.ds(pl.multiple_of(chunk_ids[cc] * T, T), T), pair_cols[pp]].astype(F32)
                    for cc, pp in problems]

        def vecs(ref):
            return [ref[:, pair_cols[pp]] for _, pp in problems]

        outs = _wkv_prepare(tiles(r_ref), tiles(k_ref), tiles(v_ref), tiles(wl_ref), tiles(a_ref),
                            vecs(kk_ref), vecs(ka_ref), vecs(rk_ref), cst)
        for ref, vals in zip((trans_ref, sadd_ref, rt_ref, y0_ref, bonus_ref), outs):
            for (cc, pp), val in zip(problems, vals):
                ref[chunk_ids[cc], pp] = val

    def scan(group):
        for cc in range(unroll):
            c = group * unroll + cc
            rows = pl.ds(pl.multiple_of(c * T, T), T)
            both = [_dot(jnp.concatenate([rt_ref[c, pp], trans_ref[c, pp]], axis=0), state_ref[pp].astype(BF16))
                    for pp in range(pairs)]
            y = [both[pp][:T] + y0_ref[c, pp] for pp in range(pairs)]
            for pp in range(pairs):
                state_ref[pp] = both[pp][T:] + sadd_ref[c, pp]
            mu = _each(lambda y_: _head_sum(y_, even) * (1.0 / RWKV_HEAD_DIM), y)
            yc = _each(lambda y_, m_: y_ - m_, y, mu)
            var = _each(lambda c_: _head_sum(c_ * c_, even) * (1.0 / RWKV_HEAD_DIM), yc)
            for pp in range(pairs):
                out = (yc[pp] * lax.rsqrt(var[pp] + RWKV_GN_EPS) * lg_ref[:, pair_cols[pp]]
                       + lb_ref[:, pair_cols[pp]] + bonus_ref[c, pp])
                gated = out * gate_ref[0, rows, pair_cols[pp]].astype(F32)
                o_ref[0, rows, pair_cols[pp]] = gated.astype(o_ref.dtype)

    n_groups = n_chunks // unroll
    prepare(0)

    def pipelined(group, _):
        scan(group - 1)
        prepare(group)
        return 0

    lax.fori_loop(1, n_groups, pipelined, 0)
    scan(n_groups - 1)


def _wkv(r, k, v, wl, a, gate, k_k, k_a, r_k, lnx_g, lnx_b, pairs=2, unroll=8):
    b, s, d = r.shape
    tb = WKV_TIME_BLOCK
    width = pairs * LANE
    n_chunks = tb // WKV_CHUNK
    tile = pl.BlockSpec((1, tb, width), lambda bi, h, t: (bi, t, h))
    vec = pl.BlockSpec((1, width), lambda bi, h, t: (0, h))
    per_chunk = lambda rows, dt: pltpu.VMEM((n_chunks, pairs, rows, LANE), dt)
    kern = functools.partial(_wkv_kernel, pairs=pairs, unroll=unroll)
    return pl.pallas_call(
        kern,
        grid=(b, d // width, s // tb),
        in_specs=[tile] * 6 + [vec] * 5,
        out_specs=tile,
        out_shape=jax.ShapeDtypeStruct((b, s, d), BF16),
        scratch_shapes=[pltpu.VMEM((pairs, LANE, LANE), F32),
                        per_chunk(LANE, BF16), per_chunk(LANE, F32),
                        per_chunk(WKV_CHUNK, BF16), per_chunk(WKV_CHUNK, F32), per_chunk(WKV_CHUNK, F32)],
        compiler_params=_cparams(("parallel", "parallel", "arbitrary")),
        name="wkv7",
    )(r, k, v, wl, a, gate, *(t.reshape(1, -1) for t in (k_k, k_a, r_k, lnx_g, lnx_b)))


def kernel(x, ev_w_in, ev_lambda, ev_subln_g, ev_w_out, od_mu, od_w_rkv, od_w0, od_w1, od_w2, od_a0, od_a1, od_a2, od_g1, od_g2, od_k_k, od_k_a, od_r_k, od_lnx_g, od_lnx_b, od_w_out, ln_mix_g, ln_mix_b, ffn_w_up, ffn_conv_w, ffn_conv_b, ffn_w_down, ln_ffn_g, ln_ffn_b):
    b, s, d = x.shape
    m = b * s
    bf = lambda t: t.astype(BF16)

    q_cols = DIFF_HEADS * 2 * DIFF_HEAD_DIM
    col_scale = jnp.where(jnp.arange(EVEN_IN_WIDTH) < q_cols, DIFF_HEAD_DIM ** -0.5 * LOG2_E, 1.0)
    qkv, gate = _in_proj(x.reshape(m, d), bf(ev_w_in[0] * col_scale))
    qkv = qkv.reshape(b, s, EVEN_QKV_WIDTH)
    lambda_init = 0.8 - 0.6 * math.exp(-0.3 * 0)
    a_out = _diff_attention(qkv, ev_lambda[0], ev_subln_g[0], lambda_init)
    b_out = _retention(qkv, gate.reshape(b, s, RET_GATE_WIDTH))
    x2 = _layer_tail([a_out, b_out], x, bf(ev_w_out[0]), ln_mix_g[0], ln_mix_b[0], bf(ffn_w_up[0]),
                     ffn_conv_w[0], ffn_conv_b[0], bf(ffn_w_down[0]), ln_ffn_g[0], ln_ffn_b[0])

    pad = GATE_LORA_PAD - GATE_LORA
    g1 = jnp.pad(bf(od_g1[0]), ((0, 0), (0, pad)))
    g2 = jnp.pad(bf(od_g2[0]), ((0, pad), (0, 0)))
    r, k, v, wl, a, g = _rwkv_proj(x2, od_mu[0], bf(od_w_rkv[0]), od_w0[0], bf(od_w1[0]), bf(od_w2[0]),
                                   od_a0[0], bf(od_a1[0]), bf(od_a2[0]), g1, g2)
    y = _wkv(r, k, v, wl, a, g, od_k_k[0], od_k_a[0], od_r_k[0], od_lnx_g[0], od_lnx_b[0])
    x4 = _layer_tail([y], x2, bf(od_w_out[0]), ln_mix_g[1], ln_mix_b[1], bf(ffn_w_up[1]),
                     ffn_conv_w[1], ffn_conv_b[1], bf(ffn_w_down[1]), ln_ffn_g[1], ln_ffn_b[1])
    return x4
```

```python
import functools
import math

import jax
import jax.numpy as jnp
from jax import lax
from jax.experimental import pallas as pl
from jax.experimental.pallas import tpu as pltpu

F32 = jnp.float32
BF16 = jnp.bfloat16

D_MODEL = 1024
DEPTH = 2
DN_ALPHA = (2.0 * DEPTH) ** 0.25
LN_EPS = 1e-5

DIFF_HEAD_DIM = 64
LOG2_E = math.log2(math.e)
DIFF_HEADS = 4
RET_QK_DIM = 64
RET_V_DIM = 128
RET_HEADS = 4
RET_CHUNK = 128
EVEN_IN_WIDTH = 3072
EVEN_QKV_WIDTH = 2560
RET_GATE_WIDTH = 512

RWKV_HEAD_DIM = 64
RWKV_GN_EPS = 64e-5
GATE_LORA = 160
GATE_LORA_PAD = 256
WKV_CHUNK = 64
WKV_INV_BASE = 8
WKV_TIME_BLOCK = 2048

FFN_HIDDEN = 2816
FFN_CHUNK = 256
CONV_HALO = 16

LANE = 128
VMEM_LIMIT = 56 * 1024 * 1024


def _cparams(sem):
    return pltpu.CompilerParams(dimension_semantics=sem, vmem_limit_bytes=VMEM_LIMIT)


def _dot(a, b):
    return jnp.dot(a, b, preferred_element_type=F32)


def _dot_nt(a, b):
    return lax.dot_general(a, b, (((1,), (1,)), ((), ())), preferred_element_type=F32)


def _dot_tn(a, b):
    return lax.dot_general(a, b, (((0,), (0,)), ((), ())), preferred_element_type=F32)


def _layer_norm(z, g, b):
    mu = jnp.mean(z, axis=-1, keepdims=True)
    zc = z - mu
    var = jnp.mean(zc * zc, axis=-1, keepdims=True)
    return zc * lax.rsqrt(var + LN_EPS) * g + b


def _in_proj_kernel(x_ref, w_ref, qkv_ref, gate_ref):
    h = _dot(x_ref[...].astype(BF16), w_ref[...])
    qkv_ref[...] = h[:, :EVEN_QKV_WIDTH].astype(BF16)
    gate_ref[...] = h[:, EVEN_QKV_WIDTH:]


def _in_proj(x2d, w_bf16, tm=512):
    m = x2d.shape[0]
    return pl.pallas_call(
        _in_proj_kernel,
        grid=(m // tm,),
        in_specs=[pl.BlockSpec((tm, D_MODEL), lambda i: (i, 0)),
                  pl.BlockSpec((D_MODEL, EVEN_IN_WIDTH), lambda i: (0, 0))],
        out_specs=[pl.BlockSpec((tm, EVEN_QKV_WIDTH), lambda i: (i, 0)),
                   pl.BlockSpec((tm, RET_GATE_WIDTH), lambda i: (i, 0))],
        out_shape=[jax.ShapeDtypeStruct((m, EVEN_QKV_WIDTH), BF16),
                   jax.ShapeDtypeStruct((m, RET_GATE_WIDTH), F32)],
        compiler_params=_cparams(("parallel",)),
        name="in_proj",
    )(x2d, w_bf16)


def _diff_attn_kernel(lam_ref, g_ref, q_ref, k_ref, v_ref, o_ref, sa_ref, sb_ref, m_ref, l_ref, acc_ref, vt_ref,
                      *, tq, lambda_init):
    h = pl.program_id(1)
    qi = pl.program_id(2)
    tk = tq
    slope = jnp.where(h == 0, 2.0 ** -2, jnp.where(h == 1, 2.0 ** -4,
                      jnp.where(h == 2, 2.0 ** -6, 2.0 ** -8))).astype(F32)
    lp = lam_ref[...]
    lam = (jnp.exp(jnp.sum(lp[0:1] * lp[1:2], axis=-1, keepdims=True))
           - jnp.exp(jnp.sum(lp[2:3] * lp[3:4], axis=-1, keepdims=True)) + lambda_init)

    lane = lax.broadcasted_iota(jnp.int32, (1, LANE), 1)
    q = q_ref[0]
    zero = jnp.zeros_like(q)
    slope = slope * LOG2_E
    c = jnp.full((1, LANE), slope, F32)
    c_hi = c.astype(BF16).astype(F32)
    q_pos = jnp.broadcast_to(jnp.where(lane < 2, c_hi, jnp.where(lane < 4, c - c_hi, 0.0)).astype(BF16),
                             (tq, LANE))
    q12 = jnp.concatenate([jnp.concatenate([jnp.where(lane < DIFF_HEAD_DIM, q, zero), q_pos], axis=1),
                           jnp.concatenate([jnp.where(lane >= DIFF_HEAD_DIM, q, zero), q_pos], axis=1)],
                          axis=0)
    q12_t = q12.astype(F32).T.astype(BF16)
    key_off = lax.broadcasted_iota(jnp.int32, (tk, LANE), 0)
    k_pos = jnp.where((lane == 0) | (lane == 2), key_off & 255,
                      jnp.where((lane == 1) | (lane == 3), key_off & ~255, 0)).astype(F32).astype(BF16)

    def scores_t(j, dst_ref):
        start = pl.multiple_of(j * tk, tk)
        k = jnp.concatenate([k_ref[0, pl.ds(start, tk), :], k_pos], axis=1)
        dst_ref[...] = _dot(k, q12_t)

    def update(src_ref, j, diagonal):
        s_t = src_ref[...]
        if diagonal:
            key_i = lax.broadcasted_iota(jnp.int32, (tk, 1), 0)
            query_i = lax.broadcasted_iota(jnp.int32, (1, 2 * tq), 1) % tq
            s_t = jnp.where(key_i <= query_i, s_t, -1e30)
        shift = ((j - qi) * tk).astype(F32) * slope
        v_t = vt_ref[:, pl.ds(pl.multiple_of(j * tk, tk), tk)]
        m = m_ref[...]
        m_new = jnp.maximum(m, jnp.max(s_t, axis=0, keepdims=True) + shift)
        alpha = jnp.exp2(m - m_new)
        p_t = jnp.exp2(s_t - (m_new - shift))
        m_ref[...] = m_new
        l_ref[...] = alpha * l_ref[...] + jnp.sum(p_t, axis=0, keepdims=True)
        acc_ref[...] = alpha * acc_ref[...] + _dot(v_t, p_t.astype(BF16))

    @pl.when(qi == 0)
    def _():
        vt_ref[...] = v_ref[0].astype(F32).T.astype(BF16)

    m_ref[...] = jnp.full(m_ref.shape, -1e30, F32)
    l_ref[...] = jnp.zeros(l_ref.shape, F32)
    acc_ref[...] = jnp.zeros(acc_ref.shape, F32)

    scores_t(0, sa_ref)

    def pair_body(jj, _):
        j0 = 2 * jj
        scores_t(j0 + 1, sb_ref)
        update(sa_ref, j0, False)
        scores_t(j0 + 2, sa_ref)
        update(sb_ref, j0 + 1, False)
        return 0

    lax.fori_loop(0, qi // 2, pair_body, 0)

    @pl.when(qi % 2 == 1)
    def _():
        scores_t(qi, sb_ref)
        update(sa_ref, qi - 1, False)
        update(sb_ref, qi, True)

    @pl.when(qi % 2 == 0)
    def _():
        update(sa_ref, qi, True)

    l = l_ref[...]
    acc_t = acc_ref[...]
    o_t = acc_t[:, :tq] / l[:, :tq] - lam * (acc_t[:, tq:] / l[:, tq:])
    o_t = o_t * lax.rsqrt(jnp.mean(o_t * o_t, axis=0, keepdims=True) + LN_EPS)
    o_ref[0] = (o_t.T * g_ref[...] * (1.0 - lambda_init)).astype(o_ref.dtype)


def _diff_attention(qkv, lam_p, subln_g, lambda_init, tq=512):
    b, s, _ = qkv.shape
    kern = functools.partial(_diff_attn_kernel, tq=tq, lambda_init=lambda_init)
    return pl.pallas_call(
        kern,
        grid=(b, DIFF_HEADS, s // tq),
        in_specs=[pl.BlockSpec((4, DIFF_HEAD_DIM), lambda bi, h, i: (0, 0)),
                  pl.BlockSpec((1, LANE), lambda bi, h, i: (0, 0)),
                  pl.BlockSpec((1, tq, LANE), lambda bi, h, i: (bi, i, h)),
                  pl.BlockSpec((1, s, LANE), lambda bi, h, i: (bi, 0, DIFF_HEADS + h)),
                  pl.BlockSpec((1, s, LANE), lambda bi, h, i: (bi, 0, 2 * DIFF_HEADS + h))],
        out_specs=pl.BlockSpec((1, tq, LANE), lambda bi, h, i: (bi, i, h)),
        out_shape=jax.ShapeDtypeStruct((b, s, DIFF_HEADS * LANE), BF16),
        scratch_shapes=[pltpu.VMEM((tq, 2 * tq), F32), pltpu.VMEM((tq, 2 * tq), F32),
                        pltpu.VMEM((1, 2 * tq), F32), pltpu.VMEM((1, 2 * tq), F32),
                        pltpu.VMEM((LANE, 2 * tq), F32),
                        pltpu.VMEM((LANE, s), BF16)],
        compiler_params=_cparams(("parallel", "parallel", "arbitrary")),
        name="diff_attn",
    )(lam_p, subln_g.reshape(1, LANE), qkv, qkv, qkv)


def _retention_kernel(q_ref, k_ref, v_ref, g_ref, o_ref, state_ref, *, chunks_per_step):
    C = RET_CHUNK

    @pl.when(pl.program_id(1) == 0)
    def _():
        state_ref[...] = jnp.zeros_like(state_ref)

    lane = lax.broadcasted_iota(jnp.int32, (1, LANE), 1)
    row128 = lax.broadcasted_iota(jnp.int32, (LANE, 1), 0)
    ri = lax.broadcasted_iota(jnp.int32, (C, C), 0)
    ci = lax.broadcasted_iota(jnp.int32, (C, C), 1)
    rel = (ri - ci).astype(F32)
    idx = lax.broadcasted_iota(jnp.int32, (C, 1), 0).astype(F32)
    heads = range(RET_HEADS)
    log_gamma = [math.log1p(-(2.0 ** (-5.0 - h))) for h in heads]
    in_head = [(lane >= (h % 2) * RET_QK_DIM) & (lane < (h % 2 + 1) * RET_QK_DIM) for h in heads]
    in_rows = [(row128 >= (h % 2) * RET_QK_DIM) & (row128 < (h % 2 + 1) * RET_QK_DIM) for h in heads]
    decay = [jnp.where(rel >= 0, jnp.exp(lg * jnp.maximum(rel, 0.0)), 0.0) for lg in log_gamma]
    q_decay = [jnp.exp(lg * (idx + 1.0)) for lg in log_gamma]
    k_decay = [jnp.exp(lg * (C - 1.0 - idx)) for lg in log_gamma]
    v_cols = [slice(h * RET_V_DIM, (h + 1) * RET_V_DIM) for h in heads]

    group = 2
    problems = [(cc, h) for cc in range(group) for h in heads]

    def group_body(gi, _):
        rows = [pl.ds(pl.multiple_of((gi * group + cc) * C, C), C) for cc in range(group)]
        q_all = [q_ref[0, r, :].astype(F32) for r in rows]
        k_all = [k_ref[0, r, :].astype(F32) * (RET_QK_DIM ** -0.5) for r in rows]
        q = [jnp.where(in_head[h], q_all[cc][:, (h // 2) * LANE:(h // 2 + 1) * LANE], 0.0) for cc, h in problems]
        k = [k_all[cc][:, (h // 2) * LANE:(h // 2 + 1) * LANE] for cc, h in problems]
        v = [v_ref[0, rows[cc], v_cols[h]] for cc, h in problems]
        sc = [_dot_nt(q_.astype(BF16), k_.astype(BF16)) * decay[h] for q_, k_, (_, h) in zip(q, k, problems)]
        inner = [_dot(s_.astype(BF16), v_) for s_, v_ in zip(sc, v)]
        upd = [_dot_tn((k_ * k_decay[h]).astype(BF16), v_) for k_, v_, (_, h) in zip(k, v, problems)]
        q_dec = [(q_ * q_decay[h]).astype(BF16) for q_, (_, h) in zip(q, problems)]
        state = [state_ref[h] for h in heads]
        cross = []
        for i, (cc, h) in enumerate(problems):
            cross.append(_dot(q_dec[i], state[h].astype(BF16)))
            state[h] = state[h] * math.exp(log_gamma[h] * C) + jnp.where(in_rows[h], upd[i], 0.0)
        for h in heads:
            state_ref[h] = state[h]
        y = [i_ + c_ for i_, c_ in zip(inner, cross)]
        mu = [jnp.mean(y_, axis=-1, keepdims=True) for y_ in y]
        yc = [y_ - m_ for y_, m_ in zip(y, mu)]
        var = [jnp.mean(c_ * c_, axis=-1, keepdims=True) for c_ in yc]
        for i, (cc, h) in enumerate(problems):
            g = g_ref[0, rows[cc], v_cols[h]]
            o_ref[0, rows[cc], v_cols[h]] = (g * jax.nn.sigmoid(g)
                                             * (yc[i] * lax.rsqrt(var[i] + LN_EPS))).astype(o_ref.dtype)
        return 0

    lax.fori_loop(0, chunks_per_step // group, group_body, 0)


def _retention(qkv, gate, chunks_per_step=4):
    b, s, _ = qkv.shape
    C = RET_CHUNK * chunks_per_step
    qk_w = RET_HEADS * RET_QK_DIM
    v_w = RET_HEADS * RET_V_DIM
    return pl.pallas_call(
        functools.partial(_retention_kernel, chunks_per_step=chunks_per_step),
        grid=(b, s // C),
        in_specs=[pl.BlockSpec((1, C, qk_w), lambda bi, c: (bi, c, 1536 // qk_w)),
                  pl.BlockSpec((1, C, qk_w), lambda bi, c: (bi, c, 1792 // qk_w)),
                  pl.BlockSpec((1, C, v_w), lambda bi, c: (bi, c, 2048 // v_w)),
                  pl.BlockSpec((1, C, v_w), lambda bi, c: (bi, c, 0))],
        out_specs=pl.BlockSpec((1, C, v_w), lambda bi, c: (bi, c, 0)),
        out_shape=jax.ShapeDtypeStruct((b, s, v_w), BF16),
        scratch_shapes=[pltpu.VMEM((RET_HEADS, LANE, RET_V_DIM), F32)],
        compiler_params=_cparams(("parallel", "arbitrary")),
        name="retention",
    )(qkv, qkv, qkv, gate)


def _layer_tail_kernel(*refs, tm, n_pieces):
    piece_refs = [refs[3 * p:3 * p + 3] for p in range(n_pieces)]
    (x_ref, xhalo_ref, mg_ref, mb_ref, wu_ref, cw_ref, cb_ref, wd_ref, fg_ref, fb_ref,
     o_ref, act_ref) = refs[3 * n_pieces:]
    i = pl.program_id(1)
    mix = None
    for tile_ref, halo_ref, w_ref in piece_refs:
        part = _dot(jnp.concatenate([halo_ref[0], tile_ref[0]], axis=0), w_ref[...])
        mix = part if mix is None else mix + part
    xe = jnp.concatenate([xhalo_ref[0], x_ref[0]], axis=0)
    x1e = _layer_norm(DN_ALPHA * xe + mix, mg_ref[...], mb_ref[...])
    x1 = x1e[CONV_HALO:]
    xe = jnp.concatenate([jnp.where(i > 0, x1e[:CONV_HALO], 0.0), x1], axis=0).astype(BF16)
    n_chunks = FFN_HIDDEN // FFN_CHUNK
    for c in range(n_chunks):
        lo = c * FFN_CHUNK
        ue = _dot(xe, wu_ref[:, lo:lo + FFN_CHUNK])
        gate = _dot(xe[CONV_HALO:], wu_ref[:, FFN_HIDDEN + lo:FFN_HIDDEN + lo + FFN_CHUNK])
        cw = cw_ref[:, lo:lo + FFN_CHUNK]
        conv = (cb_ref[:, lo:lo + FFN_CHUNK]
                + ue[CONV_HALO - 2:CONV_HALO - 2 + tm] * cw[0:1]
                + ue[CONV_HALO - 1:CONV_HALO - 1 + tm] * cw[1:2]
                + ue[CONV_HALO:] * cw[2:3])
        act = 0.5 * conv * (1.0 + lax.erf(conv * (2.0 ** -0.5)))
        act_ref[:, lo:lo + FFN_CHUNK] = (act * gate).astype(BF16)
    ffn = _dot(act_ref[...], wd_ref[...])
    o_ref[0] = _layer_norm(DN_ALPHA * x1 + ffn, fg_ref[...], fb_ref[...])


def _layer_tail(pieces, x, w_out_bf16, ln_mix_g, ln_mix_b, w_up_bf16, conv_w, conv_b, w_down_bf16,
                ln_ffn_g, ln_ffn_b, tm=512):
    b, s, _ = x.shape
    assert len({p.shape[-1] for p in pieces}) == 1 and sum(p.shape[-1] for p in pieces) == D_MODEL
    kern = functools.partial(_layer_tail_kernel, tm=tm, n_pieces=len(pieces))
    fixed = lambda bi, i: (0, 0)
    tile = lambda bi, i: (bi, i, 0)
    halo = lambda bi, i: (bi, jnp.maximum(i * (tm // CONV_HALO) - 1, 0), 0)
    single = pl.Buffered(1)
    in_specs, args = [], []
    for row_block, piece in enumerate(pieces):
        width = piece.shape[-1]
        in_specs += [pl.BlockSpec((1, tm, width), tile), pl.BlockSpec((1, CONV_HALO, width), halo),
                     pl.BlockSpec((width, D_MODEL), functools.partial(lambda bi, i, r: (r, 0), r=row_block),
                                  pipeline_mode=single)]
        args += [piece, piece, w_out_bf16]
    in_specs += [pl.BlockSpec((1, tm, D_MODEL), tile), pl.BlockSpec((1, CONV_HALO, D_MODEL), halo),
                 pl.BlockSpec((1, D_MODEL), fixed), pl.BlockSpec((1, D_MODEL), fixed),
                 pl.BlockSpec((D_MODEL, 2 * FFN_HIDDEN), fixed, pipeline_mode=single),
                 pl.BlockSpec((3, FFN_HIDDEN), fixed),
                 pl.BlockSpec((1, FFN_HIDDEN), fixed),
                 pl.BlockSpec((FFN_HIDDEN, D_MODEL), fixed, pipeline_mode=single),
                 pl.BlockSpec((1, D_MODEL), fixed), pl.BlockSpec((1, D_MODEL), fixed)]
    args += [x, x, ln_mix_g.reshape(1, -1), ln_mix_b.reshape(1, -1), w_up_bf16, conv_w, conv_b.reshape(1, -1),
             w_down_bf16, ln_ffn_g.reshape(1, -1), ln_ffn_b.reshape(1, -1)]
    return pl.pallas_call(
        kern,
        grid=(b, s // tm),
        in_specs=in_specs,
        out_specs=pl.BlockSpec((1, tm, D_MODEL), tile),
        out_shape=jax.ShapeDtypeStruct((b, s, D_MODEL), F32),
        scratch_shapes=[pltpu.VMEM((tm, FFN_HIDDEN), BF16)],
        compiler_params=_cparams(("parallel", "arbitrary")),
        name="layer_tail",
    )(*args)


def _rwkv_proj_kernel(x_ref, halo_ref, mu_ref, wrkv_ref, w0_ref, w1_ref, w2_ref, a0_ref, a1_ref, a2_ref,
                      g1_ref, g2_ref, r_ref, k_ref, v_ref, wl_ref, a_ref, g_ref, *, tm):
    i = pl.program_id(1)
    x = x_ref[0]
    prev_row = jnp.where(i > 0, halo_ref[0, 7:8, :], 0.0)
    row = lax.broadcasted_iota(jnp.int32, (tm, 1), 0)
    x_prev = jnp.where(row == 0, prev_row, pltpu.roll(x, 1, 0))
    xx = x_prev - x

    def mix(n):
        return (x + xx * mu_ref[n:n + 1, :]).astype(BF16)

    r_ref[0] = _dot(mix(0), wrkv_ref[0]).astype(r_ref.dtype)
    k_ref[0] = _dot(mix(1), wrkv_ref[1]).astype(k_ref.dtype)
    v_ref[0] = _dot(mix(2), wrkv_ref[2]).astype(v_ref.dtype)
    lw = jnp.tanh(_dot(mix(3), w1_ref[...])).astype(BF16)
    u = w0_ref[...] + _dot(lw, w2_ref[...])
    wl_ref[0] = -math.exp(-0.5) * jax.nn.sigmoid(u)
    la = _dot(mix(4), a1_ref[...]).astype(BF16)
    a_ref[0] = jax.nn.sigmoid(a0_ref[...] + _dot(la, a2_ref[...])).astype(a_ref.dtype)
    lg = jax.nn.sigmoid(_dot(mix(5), g1_ref[...])).astype(BF16)
    g_ref[0] = _dot(lg, g2_ref[...]).astype(g_ref.dtype)


def _rwkv_proj(x, mu, w_rkv, w0, w1, w2, a0, a1, a2, g1, g2, tm=512):
    b, s, d = x.shape
    kern = functools.partial(_rwkv_proj_kernel, tm=tm)
    fixed2 = lambda bi, i: (0, 0)
    tile = pl.BlockSpec((1, tm, d), lambda bi, i: (bi, i, 0))
    lora = w1.shape[1]
    out = jax.ShapeDtypeStruct((b, s, d), F32)
    half = jax.ShapeDtypeStruct((b, s, d), BF16)
    return pl.pallas_call(
        kern,
        grid=(b, s // tm),
        in_specs=[tile,
                  pl.BlockSpec((1, 8, d), lambda bi, i: (bi, jnp.maximum(i * (tm // 8) - 1, 0), 0)),
                  pl.BlockSpec((6, d), fixed2),
                  pl.BlockSpec((3, d, d), lambda bi, i: (0, 0, 0)),
                  pl.BlockSpec((1, d), fixed2), pl.BlockSpec((d, lora), fixed2), pl.BlockSpec((lora, d), fixed2),
                  pl.BlockSpec((1, d), fixed2), pl.BlockSpec((d, lora), fixed2), pl.BlockSpec((lora, d), fixed2),
                  pl.BlockSpec((d, GATE_LORA_PAD), fixed2), pl.BlockSpec((GATE_LORA_PAD, d), fixed2)],
        out_specs=[tile] * 6,
        out_shape=[half, half, half, out, half, half],
        compiler_params=_cparams(("parallel", "arbitrary")),
        name="rwkv_proj",
    )(x, x, mu, w_rkv, w0.reshape(1, -1), w1, w2, a0.reshape(1, -1), a1, a2, g1, g2)


def _each(f, *lists):
    return [f(*xs) for xs in zip(*lists)]


def _wkv_pair_consts():
    T = WKV_CHUNK
    lane = lax.broadcasted_iota(jnp.int32, (1, LANE), 1)
    row = lax.broadcasted_iota(jnp.int32, (T, LANE), 0)
    col = lax.broadcasted_iota(jnp.int32, (T, LANE), 1) % T
    r2 = lax.broadcasted_iota(jnp.int32, (LANE, LANE), 0)
    c2 = lax.broadcasted_iota(jnp.int32, (LANE, LANE), 1)
    tri_r = lax.broadcasted_iota(jnp.int32, (T, T), 0)
    tri_c = lax.broadcasted_iota(jnp.int32, (T, T), 1)
    same_head = (r2 // T) == (c2 // T)
    merges = []
    size = 2 * WKV_INV_BASE
    while size <= T:
        merges.append(((row // size) == (col // size)) & ((row // (size // 2)) != (col // (size // 2))))
        size *= 2
    return dict(
        even=lane < T,
        strict=col < row, incl=col <= row, eye=col == row,
        in_base=(row // WKV_INV_BASE) == (col // WKV_INV_BASE),
        merges=merges,
        same_head=same_head, eye128=r2 == c2,
        tri=jnp.where(tri_c <= tri_r, 1.0, 0.0).astype(BF16),
    )


def _head_sum(x, even):
    s_even = jnp.sum(jnp.where(even, x, 0.0), axis=-1, keepdims=True)
    s_odd = jnp.sum(jnp.where(even, 0.0, x), axis=-1, keepdims=True)
    return jnp.where(even, s_even, s_odd)


def _wkv_prepare(r, k, v, wl, lr, kk_scale, ka, rk, cst):
    T = WKV_CHUNK
    bf = lambda t: t.astype(BF16)
    even, strict, incl, eye = cst["even"], cst["strict"], cst["incl"], cst["eye"]
    tri_incl_bf16 = cst["tri"]

    def stack2(x):
        zero = jnp.zeros_like(x)
        return jnp.concatenate([jnp.where(even, x, zero), jnp.where(even, zero, x)], axis=0)

    def pair_dot(packed, x):
        return _dot(packed, stack2(x))

    def pair_dot2(packed, x1, x2):
        return _dot(packed, jnp.concatenate([stack2(x1), stack2(x2)], axis=1))

    kk = _each(lambda k_, s_: k_ * s_, k, kk_scale)
    ss = _each(lambda t: _head_sum(t * t, even), kk)
    kk = _each(lambda t, s_: t / jnp.maximum(jnp.sqrt(s_), 1e-12), kk, ss)
    k = _each(lambda k_, lr_, ka_: k_ * (1.0 + (lr_ - 1.0) * ka_), k, lr, ka)
    b_vec = _each(lambda kk_, lr_: kk_ * lr_, kk, lr)

    w_hi = _each(bf, wl)
    w_r1 = _each(lambda w_, h_: w_ - h_.astype(F32), wl, w_hi)
    w_mid = _each(bf, w_r1)
    w_lo = _each(lambda r1, m_: (r1 - m_.astype(F32)).astype(BF16), w_r1, w_mid)
    cl = _each(lambda h_, m_, l_: _dot(tri_incl_bf16, h_) + (_dot(tri_incl_bf16, m_) + _dot(tri_incl_bf16, l_)),
               w_hi, w_mid, w_lo)
    cl_last = _each(lambda c_: c_[T - 1:T, :], cl)
    g_inv = _each(lambda c_: jnp.exp(-c_), cl)
    a_hat = _each(lambda kk_, c_, w_: -kk_ * jnp.exp(c_ - w_), kk, cl, wl)
    b_hat = _each(lambda b_, g_: bf(b_ * g_), b_vec, g_inv)
    k_hat = _each(lambda k_, g_: bf(k_ * g_), k, g_inv)
    r_hat = _each(lambda r_, c_: r_ * jnp.exp(c_), r, cl)
    to_end = _each(lambda l_, c_: jnp.exp(l_ - c_), cl_last, cl)
    b_end = _each(lambda b_, e_: bf(b_ * e_), b_vec, to_end)
    k_end = _each(lambda k_, e_: bf(k_ * e_), k, to_end)

    ar = _each(lambda a_, r_: bf(jnp.concatenate([a_, r_], axis=0)), a_hat, r_hat)
    m_bk = _each(lambda ar_, b_, k_: _dot_nt(ar_, jnp.concatenate([stack2(b_), stack2(k_)], axis=0)),
                 ar, b_hat, k_hat)
    m_b = _each(lambda m_: m_[:, :LANE], m_bk)
    m_k = _each(lambda m_: m_[:, LANE:], m_bk)
    a_ab = _each(lambda m_: jnp.where(strict, m_[:T], 0.0), m_b)
    a_rb = _each(lambda m_: bf(jnp.where(incl, m_[T:], 0.0)), m_b)
    a_ak = _each(lambda m_: bf(jnp.where(strict, m_[:T], 0.0)), m_k)
    a_rk = _each(lambda m_: bf(jnp.where(incl, m_[T:], 0.0)), m_k)

    p = _each(lambda a_: jnp.where(cst["in_base"], a_, 0.0), a_ab)
    minv = _each(lambda p_: jnp.where(eye, 1.0, 0.0) + p_, p)
    levels = int(math.log2(WKV_INV_BASE)) - 1
    p = _each(lambda p_: pair_dot(bf(p_), bf(p_)), p)
    for level in range(levels):
        if level + 1 < levels:
            both = _each(lambda p_, m_: pair_dot2(bf(p_), bf(m_), bf(p_)), p, minv)
            minv = _each(lambda m_, x_: m_ + x_[:, :LANE], minv, both)
            p = _each(lambda x_: x_[:, LANE:], both)
        else:
            minv = _each(lambda m_, p_: m_ + pair_dot(bf(p_), bf(m_)), minv, p)
    for off in cst["merges"]:
        a_off = _each(lambda a_: bf(jnp.where(off, a_, 0.0)), a_ab)
        minv = _each(lambda m_, a_: m_ + pair_dot(bf(m_), bf(pair_dot(a_, bf(m_)))), minv, a_off)

    v_bf16 = _each(bf, v)
    minv_bf16 = _each(bf, minv)
    ak_rk_v = _each(lambda ak_, rk_, v_: pair_dot(jnp.concatenate([ak_, rk_], axis=0), v_), a_ak, a_rk, v_bf16)
    akv = _each(lambda x_: bf(x_[:T]), ak_rk_v)
    at_w = _each(lambda m_, a_, x_: bf(pair_dot2(m_, a_[:T], x_)), minv_bf16, ar, akv)
    a_t = _each(lambda x_: x_[:, :LANE], at_w)
    w_bf16 = _each(lambda x_: x_[:, LANE:], at_w)
    rb_at_w = _each(lambda a_, t_, w_: pair_dot2(a_, t_, w_), a_rb, a_t, w_bf16)
    r_t = _each(lambda r_, x_: bf(r_ + x_[:, :LANE]), r_hat, rb_at_w)
    y0 = _each(lambda x_, y_: x_[:, LANE:] + y_[T:], rb_at_w, ak_rk_v)
    same_head, eye128 = cst["same_head"], cst["eye128"]
    b_atw = _each(lambda b_, x_: _dot_tn(b_, x_), b_end, at_w)
    trans = _each(lambda l_, x_: bf(jnp.where(eye128, jnp.exp(l_), 0.0) + jnp.where(same_head, x_[:, :LANE], 0.0)),
                  cl_last, b_atw)
    s_add = _each(lambda x_, k_, v_: jnp.where(same_head, x_[:, LANE:] + _dot_tn(k_, v_), 0.0),
                  b_atw, k_end, v_bf16)
    bonus = _each(lambda r_, k_, rk_, v_: _head_sum(r_ * k_ * rk_, even) * v_, r, k, rk, v)
    return trans, s_add, r_t, y0, bonus


def _wkv_kernel(r_ref, k_ref, v_ref, wl_ref, a_ref, gate_ref, kk_ref, ka_ref, rk_ref, lg_ref, lb_ref, o_ref,
                state_ref, trans_ref, sadd_ref, rt_ref, y0_ref, bonus_ref, *, pairs, unroll):
    @pl.when(pl.program_id(2) == 0)
    def _():
        state_ref[...] = jnp.zeros_like(state_ref)

    T = WKV_CHUNK
    n_chunks = WKV_TIME_BLOCK // T
    cst = _wkv_pair_consts()
    even = cst["even"]
    pair_cols = [slice(pp * LANE, (pp + 1) * LANE) for pp in range(pairs)]

    def prepare(group):
        chunk_ids = [group * unroll + cc for cc in range(unroll)]
        problems = [(cc, pp) for cc in range(unroll) for pp in range(pairs)]

        def tiles(ref):
            return [ref[0, pl.ds(pl.multiple_of(chunk_ids[cc] * T, T), T), pair_cols[pp]].astype(F32)
                    for cc, pp in problems]

        def vecs(ref):
            return [ref[:, pair_cols[pp]] for _, pp in problems]

        outs = _wkv_prepare(tiles(r_ref), tiles(k_ref), tiles(v_ref), tiles(wl_ref), tiles(a_ref),
                            vecs(kk_ref), vecs(ka_ref), vecs(rk_ref), cst)
        for ref, vals in zip((trans_ref, sadd_ref, rt_ref, y0_ref, bonus_ref), outs):
            for (cc, pp), val in zip(problems, vals):
                ref[chunk_ids[cc], pp] = val

    def scan(group):
        for cc in range(unroll):
            c = group * unroll + cc
            rows = pl.ds(pl.multiple_of(c * T, T), T)
            both = [_dot(jnp.concatenate([rt_ref[c, pp], trans_ref[c, pp]], axis=0), state_ref[pp].astype(BF16))
                    for pp in range(pairs)]
            y = [both[pp][:T] + y0_ref[c, pp] for pp in range(pairs)]
            for pp in range(pairs):
                state_ref[pp] = both[pp][T:] + sadd_ref[c, pp]
            mu = _each(lambda y_: _head_sum(y_, even) * (1.0 / RWKV_HEAD_DIM), y)
            yc = _each(lambda y_, m_: y_ - m_, y, mu)
            var = _each(lambda c_: _head_sum(c_ * c_, even) * (1.0 / RWKV_HEAD_DIM), yc)
            for pp in range(pairs):
                out = (yc[pp] * lax.rsqrt(var[pp] + RWKV_GN_EPS) * lg_ref[:, pair_cols[pp]]
                       + lb_ref[:, pair_cols[pp]] + bonus_ref[c, pp])
                gated = out * gate_ref[0, rows, pair_cols[pp]].astype(F32)
                o_ref[0, rows, pair_cols[pp]] = gated.astype(o_ref.dtype)

    n_groups = n_chunks // unroll
    prepare(0)

    def pipelined(group, _):
        scan(group - 1)
        prepare(group)
        return 0

    lax.fori_loop(1, n_groups, pipelined, 0)
    scan(n_groups - 1)


def _wkv(r, k, v, wl, a, gate, k_k, k_a, r_k, lnx_g, lnx_b, pairs=2, unroll=8):
    b, s, d = r.shape
    tb = WKV_TIME_BLOCK
    width = pairs * LANE
    n_chunks = tb // WKV_CHUNK
    tile = pl.BlockSpec((1, tb, width), lambda bi, h, t: (bi, t, h))
    vec = pl.BlockSpec((1, width), lambda bi, h, t: (0, h))
    per_chunk = lambda rows, dt: pltpu.VMEM((n_chunks, pairs, rows, LANE), dt)
    kern = functools.partial(_wkv_kernel, pairs=pairs, unroll=unroll)
    return pl.pallas_call(
        kern,
        grid=(b, d // width, s // tb),
        in_specs=[tile] * 6 + [vec] * 5,
        out_specs=tile,
        out_shape=jax.ShapeDtypeStruct((b, s, d), BF16),
        scratch_shapes=[pltpu.VMEM((pairs, LANE, LANE), F32),
                        per_chunk(LANE, BF16), per_chunk(LANE, F32),
                        per_chunk(WKV_CHUNK, BF16), per_chunk(WKV_CHUNK, F32), per_chunk(WKV_CHUNK, F32)],
        compiler_params=_cparams(("parallel", "parallel", "arbitrary")),
        name="wkv7",
    )(r, k, v, wl, a, gate, *(t.reshape(1, -1) for t in (k_k, k_a, r_k, lnx_g, lnx_b)))


def kernel(x, ev_w_in, ev_lambda, ev_subln_g, ev_w_out, od_mu, od_w_rkv, od_w0, od_w1, od_w2, od_a0, od_a1, od_a2, od_g1, od_g2, od_k_k, od_k_a, od_r_k, od_lnx_g, od_lnx_b, od_w_out, ln_mix_g, ln_mix_b, ffn_w_up, ffn_conv_w, ffn_conv_b, ffn_w_down, ln_ffn_g, ln_ffn_b):
    b, s, d = x.shape
    m = b * s
    bf = lambda t: t.astype(BF16)

    q_cols = DIFF_HEADS * 2 * DIFF_HEAD_DIM
    col_scale = jnp.where(jnp.arange(EVEN_IN_WIDTH) < q_cols, DIFF_HEAD_DIM ** -0.5 * LOG2_E, 1.0)
    qkv, gate = _in_proj(x.reshape(m, d), bf(ev_w_in[0] * col_scale))
    qkv = qkv.reshape(b, s, EVEN_QKV_WIDTH)
    lambda_init = 0.8 - 0.6 * math.exp(-0.3 * 0)
    a_out = _diff_attention(qkv, ev_lambda[0], ev_subln_g[0], lambda_init)
    b_out = _retention(qkv, gate.reshape(b, s, RET_GATE_WIDTH))
    x2 = _layer_tail([a_out, b_out], x, bf(ev_w_out[0]), ln_mix_g[0], ln_mix_b[0], bf(ffn_w_up[0]),
                     ffn_conv_w[0], ffn_conv_b[0], bf(ffn_w_down[0]), ln_ffn_g[0], ln_ffn_b[0])

    pad = GATE_LORA_PAD - GATE_LORA
    g1 = jnp.pad(bf(od_g1[0]), ((0, 0), (0, pad)))
    g2 = jnp.pad(bf(od_g2[0]), ((0, pad), (0, 0)))
    r, k, v, wl, a, g = _rwkv_proj(x2, od_mu[0], bf(od_w_rkv[0]), od_w0[0], bf(od_w1[0]), bf(od_w2[0]),
                                   od_a0[0], bf(od_a1[0]), bf(od_a2[0]), g1, g2)
    y = _wkv(r, k, v, wl, a, g, od_k_k[0], od_k_a[0], od_r_k[0], od_lnx_g[0], od_lnx_b[0])
    x4 = _layer_tail([y], x2, bf(od_w_out[0]), ln_mix_g[1], ln_mix_b[1], bf(ffn_w_up[1]),
                     ffn_conv_w[1], ffn_conv_b[1], bf(ffn_w_down[1]), ln_ffn_g[1], ln_ffn_b[1])
    return x4
```
